```python
import jax, jax.numpy as jnp
from jax import lax
import numpy as np

D_MODEL = 1024
BATCH = 8
SEQ = 2048
DEPTH = 2
DEC_BATCH = 32
DEC_SEQ = 4
PAST_LEN = 16384
PAGE_SIZE = 128

N_EVEN = (DEPTH + 1) // 2
N_ODD = DEPTH // 2
EPS = 1e-6
A_CH = 256
A_CONV = 31
B_HEAD_DIM = 64
B_HPG = 4
B_WINDOWS = (128, 512, 2048)
B_DILATIONS = (1, 4, 16)
B_GROUPS = 3
B_HEADS = B_GROUPS * B_HPG
B_SUB_WINDOW = 128
B_BLOCK = 128
N_BUCKETS = 32
MAX_DISTANCE = 2048
C_CH = 512
C_CONV = 3
D_CH = 512
D_GROUPS = 4
D_GROUP_CH = D_CH // D_GROUPS
D_CHUNK = 128
FFN_HIDDEN = -(-8 * D_MODEL // (3 * 256)) * 256
E_IN = 2 * A_CH + 3 * B_HEADS * B_HEAD_DIM
E_OUT_IN = A_CH + B_HPG * B_HEAD_DIM
O_IN = 3 * C_CH + 2 * D_CH
O_OUT_IN = C_CH + D_CH

kernel_name = "hybrid_confconv_dilattn_shortconv_gmlp_step"


def _rmsnorm(x, g):
    xf = x.astype(jnp.float32)
    y = xf * lax.rsqrt(jnp.mean(xf * xf, axis=-1, keepdims=True) + EPS)
    return (y * g.astype(jnp.float32)).astype(x.dtype)


def _layernorm(x, g, b):
    xf = x.astype(jnp.float32)
    mu = jnp.mean(xf, axis=-1, keepdims=True)
    var = jnp.mean(jnp.square(xf - mu), axis=-1, keepdims=True)
    y = (xf - mu) * lax.rsqrt(var + EPS)
    return (y * g.astype(jnp.float32) + b.astype(jnp.float32)).astype(x.dtype)


def _t5_bucket(dist):
    n = np.maximum(np.asarray(dist, dtype=np.int64), 0)
    max_exact = N_BUCKETS // 2
    large = max_exact + (np.log(np.maximum(n, 1) / max_exact) / np.log(MAX_DISTANCE / max_exact)
                         * (N_BUCKETS - max_exact)).astype(np.int32)
    return np.where(n < max_exact, n, np.minimum(large, N_BUCKETS - 1)).astype(np.int32)


def _group_bias(rel_bias_table, grp, dist):
    tab = rel_bias_table[:, grp * B_HPG:(grp + 1) * B_HPG]
    return jnp.moveaxis(tab[_t5_bucket(dist)], -1, 0).astype(jnp.float32)


def _causal_dwconv(x_ext, w, b):
    y = lax.conv_general_dilated(x_ext, w[:, None, :], window_strides=(1,), padding='VALID',
                                 dimension_numbers=('NWC', 'WIO', 'NWC'),
                                 feature_group_count=w.shape[1])
    return y + b


def _dilated_attn_prompt(q, k, v, d, rel_bias_table, grp):
    N, T, H, Dh = q.shape
    L = T // d
    nb = -(-L // B_BLOCK)
    Lp = nb * B_BLOCK

    def to_sub(x):
        x = x.reshape(N, L, d, H, Dh).transpose(0, 2, 1, 3, 4).reshape(N * d, L, H, Dh)
        x = jnp.pad(x, ((0, 0), (0, Lp - L), (0, 0), (0, 0)))
        return x.reshape(N * d, nb, B_BLOCK, H, Dh)

    def with_prev(xb):
        prev = jnp.pad(xb, ((0, 0), (1, 0), (0, 0), (0, 0), (0, 0)))[:, :-1]
        return jnp.concatenate([prev, xb], axis=2)

    qb = to_sub(q)
    kk = with_prev(to_sub(k))
    vv = with_prev(to_sub(v))
    qi = np.arange(B_BLOCK)[:, None]
    ki = np.arange(2 * B_BLOCK)[None, :]
    rel = qi + B_BLOCK - ki
    band = (rel >= 0) & (rel <= B_SUB_WINDOW)
    key_ok = (np.arange(nb)[:, None, None] > 0) | (ki >= B_BLOCK)[None]
    mask = band[None] & key_ok
    bias = _group_bias(rel_bias_table, grp, np.clip(rel, 0, None) * d)
    s = jnp.einsum('znqhd,znkhd->znhqk', qb, kk).astype(jnp.float32) * (Dh ** -0.5) + bias[None, None]
    s = jnp.where(mask[None, :, None], s, -jnp.inf)
    lse = jax.nn.logsumexp(s, axis=-1)
    p = jnp.exp(s - lse[..., None])
    o = jnp.einsum('znhqk,znkhd->znqhd', p.astype(v.dtype), vv)
    o = o.reshape(N * d, Lp, H, Dh)[:, :L].reshape(N, d, L, H, Dh).transpose(0, 2, 1, 3, 4).reshape(N, T, H, Dh)
    lse = lse.transpose(0, 1, 3, 2).reshape(N * d, Lp, H)[:, :L]
    lse = lse.reshape(N, d, L, H).transpose(0, 2, 1, 3).reshape(N, T, H)
    return o, lse


def _dilated_attn_sample(q, k_ext, v_ext, d, rel_bias_table, grp):
    N, S, H, Dh = q.shape
    Wb = k_ext.shape[1] - S
    steps = np.arange(B_SUB_WINDOW + 1)
    idx = Wb + np.arange(S)[:, None] - steps[None, :] * d
    valid = idx >= 0
    idx_c = np.maximum(idx, 0)
    kg = k_ext[:, idx_c]
    vg = v_ext[:, idx_c]
    bias = _group_bias(rel_bias_table, grp, steps * d)
    s = jnp.einsum('nshd,nskhd->nhsk', q, kg).astype(jnp.float32) * (Dh ** -0.5) + bias[None, :, None, :]
    s = jnp.where(valid[None, None], s, -jnp.inf)
    lse = jax.nn.logsumexp(s, axis=-1)
    p = jnp.exp(s - lse[..., None])
    o = jnp.einsum('nhsk,nskhd->nshd', p.astype(v_ext.dtype), vg)
    return o, lse.transpose(0, 2, 1)


def _even_mixer(h, P, e, a_buf, kv_bufs):
    N, T, _ = h.shape
    proj = h @ P['e_w_in'][e]
    a_in = proj[..., :2 * A_CH]
    qkv = proj[..., 2 * A_CH:].reshape(N, T, 3, B_GROUPS, B_HPG, B_HEAD_DIM)
    glu = a_in[..., :A_CH] * jax.nn.sigmoid(a_in[..., A_CH:])
    if a_buf is None:
        a_ext = jnp.pad(glu, ((0, 0), (A_CONV - 1, 0), (0, 0)))
    else:
        a_ext = jnp.concatenate([a_buf.astype(glu.dtype), glu], axis=1)
    a_state = a_ext[:, -(A_CONV - 1):]
    a = _causal_dwconv(a_ext, P['a_conv_w'][e], P['a_conv_b'][e])
    a = jax.nn.silu(_layernorm(a, P['a_ln_g'][e], P['a_ln_b'][e]))
    outs, lses, kv_states = [], [], []
    for grp in range(B_GROUPS):
        q, k, v = qkv[:, :, 0, grp], qkv[:, :, 1, grp], qkv[:, :, 2, grp]
        d = B_DILATIONS[grp]
        if kv_bufs is None:
            o, lse = _dilated_attn_prompt(q, k, v, d, P['rel_bias_table'], grp)
            kv_states.append(jnp.stack([k, v], axis=2)[:, -min(B_WINDOWS[grp], T):])
        else:
            kv_ext = jnp.concatenate([kv_bufs[grp].astype(q.dtype), jnp.stack([k, v], axis=2)], axis=1)
            o, lse = _dilated_attn_sample(q, kv_ext[:, :, 0], kv_ext[:, :, 1], d, P['rel_bias_table'], grp)
            kv_states.append(kv_ext[:, T:])
        outs.append(o)
        lses.append(lse)
    alpha = jax.nn.softmax(jnp.stack(lses, axis=0), axis=0)
    attn = jnp.sum(alpha[..., None] * jnp.stack(outs, axis=0).astype(jnp.float32), axis=0)
    attn = attn.astype(h.dtype).reshape(N, T, B_HPG * B_HEAD_DIM)
    out = jnp.concatenate([a, attn], axis=-1) @ P['e_w_out'][e]
    return out, a_state, kv_states


def _odd_mixer(h, P, oi, c_buf):
    N, T, _ = h.shape
    proj = h @ P['o_w_in'][oi]
    bg = proj[..., :C_CH]
    cg = proj[..., C_CH:2 * C_CH]
    xi = proj[..., 2 * C_CH:3 * C_CH]
    uv = jax.nn.gelu(proj[..., 3 * C_CH:])
    ci = cg * xi
    if c_buf is None:
        c_ext = jnp.pad(ci, ((0, 0), (C_CONV - 1, 0), (0, 0)))
    else:
        c_ext = jnp.concatenate([c_buf.astype(ci.dtype), ci], axis=1)
    c_state = c_ext[:, -(C_CONV - 1):]
    yc = bg * _causal_dwconv(c_ext, P['c_conv_w'][oi], P['c_conv_b'][oi])
    u = uv[..., :D_CH]
    v = _layernorm(uv[..., D_CH:], P['d_ln_g'][oi], P['d_ln_b'][oi])
    ws = jnp.where(np.tril(np.ones((D_CHUNK, D_CHUNK), dtype=bool))[None], P['d_spatial_w'][oi], 0)
    bs = P['d_spatial_b'][oi]
    if c_buf is None:
        vb = v.reshape(N, T // D_CHUNK, D_CHUNK, D_GROUPS, D_GROUP_CH)
        mix = jnp.einsum('gij,nkjgc->nkigc', ws, vb) + bs.T[None, None, :, :, None]
    else:
        vb = v.reshape(N, T, D_GROUPS, D_GROUP_CH)
        mix = jnp.einsum('gij,njgc->nigc', ws[:, :T, :T], vb) + bs.T[None, :T, :, None]
    yd = u * mix.reshape(N, T, D_CH)
    out = jnp.concatenate([yc, yd], axis=-1) @ P['o_w_out'][oi]
    return out, c_state, v


def _trunk(x, c, P, a_bufs, kv_bufs, c_bufs):
    N = x.shape[0]
    a_st, kv_st, c_st, d_st = [], [[] for _ in range(B_GROUPS)], [], []
    cs = jax.nn.silu(c)
    for layer in range(DEPTH):
        mod = (cs @ P['ada_w'][layer] + P['ada_b'][layer]).reshape(N, 6, 1, D_MODEL)
        h = _rmsnorm(x, P['norm_mix_g'][layer]) * (1 + mod[:, 1]) + mod[:, 0]
        if layer % 2 == 0:
            e = layer // 2
            mix, a_new, kv_new = _even_mixer(h, P, e, None if a_bufs is None else a_bufs[e],
                                             None if kv_bufs is None else [buf[e] for buf in kv_bufs])
            a_st.append(a_new)
            for grp in range(B_GROUPS):
                kv_st[grp].append(kv_new[grp])
        else:
            oi = layer // 2
            mix, c_new, v_new = _odd_mixer(h, P, oi, None if c_bufs is None else c_bufs[oi])
            c_st.append(c_new)
            d_st.append(v_new)
        x = x + mod[:, 2] * mix
        h = _rmsnorm(x, P['norm_ffn_g'][layer]) * (1 + mod[:, 4]) + mod[:, 3]
        gu = h @ P['ffn_w_gate_up'][layer]
        x = x + mod[:, 5] * ((jax.nn.silu(gu[..., :FFN_HIDDEN]) * gu[..., FFN_HIDDEN:]) @ P['ffn_w_down'][layer])
    y = _rmsnorm(x, P['final_norm_g'])
    return y, jnp.stack(a_st), [jnp.stack(s) for s in kv_st], jnp.stack(c_st), d_st


def setup_inputs(seed: int = 0) -> dict:
    key = jax.random.key(seed)
    ks = jax.random.split(key, 40)
    nrm = jax.random.normal
    f32 = jnp.float32
    out = {}
    out['x_prompt'] = nrm(ks[0], (BATCH, SEQ, D_MODEL), f32)
    out['x_sample'] = nrm(ks[1], (DEC_BATCH, DEC_SEQ, D_MODEL), f32)
    out['c_prompt'] = nrm(ks[2], (BATCH, D_MODEL), f32)
    out['c_sample'] = nrm(ks[3], (DEC_BATCH, D_MODEL), f32)
    out['state_a_conv'] = 0.5 * nrm(ks[4], (N_EVEN, DEC_BATCH, A_CONV - 1, A_CH), f32)
    for grp in range(B_GROUPS):
        wb = min(B_WINDOWS[grp], PAST_LEN)
        out['cache_b_kv%d' % grp] = nrm(ks[5 + grp], (N_EVEN, DEC_BATCH, wb, 2, B_HPG, B_HEAD_DIM), f32)
    out['state_c_conv'] = 0.5 * nrm(ks[8], (N_ODD, DEC_BATCH, C_CONV - 1, C_CH), f32)
    out['rel_bias_table'] = 0.3 * nrm(ks[9], (N_BUCKETS, B_HEADS), f32)
    out['ada_w'] = 0.5 * D_MODEL ** -0.5 * nrm(ks[10], (DEPTH, D_MODEL, 6 * D_MODEL), f32)
    out['ada_b'] = 0.02 * nrm(ks[11], (DEPTH, 6 * D_MODEL), f32)
    out['norm_mix_g'] = 1.0 + 0.1 * nrm(ks[12], (DEPTH, D_MODEL), f32)
    out['norm_ffn_g'] = 1.0 + 0.1 * nrm(ks[13], (DEPTH, D_MODEL), f32)
    out['ffn_w_gate_up'] = D_MODEL ** -0.5 * nrm(ks[14], (DEPTH, D_MODEL, 2 * FFN_HIDDEN), f32)
    out['ffn_w_down'] = FFN_HIDDEN ** -0.5 * nrm(ks[15], (DEPTH, FFN_HIDDEN, D_MODEL), f32)
    out['final_norm_g'] = 1.0 + 0.1 * nrm(ks[16], (D_MODEL,), f32)
    out['e_w_in'] = D_MODEL ** -0.5 * nrm(ks[17], (N_EVEN, D_MODEL, E_IN), f32)
    out['a_conv_w'] = A_CONV ** -0.5 * nrm(ks[18], (N_EVEN, A_CONV, A_CH), f32)
    out['a_conv_b'] = 0.02 * nrm(ks[19], (N_EVEN, A_CH), f32)
    out['a_ln_g'] = 1.0 + 0.1 * nrm(ks[20], (N_EVEN, A_CH), f32)
    out['a_ln_b'] = 0.02 * nrm(ks[21], (N_EVEN, A_CH), f32)
    out['e_w_out'] = E_OUT_IN ** -0.5 * nrm(ks[22], (N_EVEN, E_OUT_IN, D_MODEL), f32)
    out['o_w_in'] = D_MODEL ** -0.5 * nrm(ks[23], (N_ODD, D_MODEL, O_IN), f32)
    out['c_conv_w'] = C_CONV ** -0.5 * nrm(ks[24], (N_ODD, C_CONV, C_CH), f32)
    out['c_conv_b'] = 0.02 * nrm(ks[25], (N_ODD, C_CH), f32)
    out['d_ln_g'] = 1.0 + 0.1 * nrm(ks[26], (N_ODD, D_CH), f32)
    out['d_ln_b'] = 0.02 * nrm(ks[27], (N_ODD, D_CH), f32)
    out['d_spatial_w'] = D_CHUNK ** -0.5 * nrm(ks[28], (N_ODD, D_GROUPS, D_CHUNK, D_CHUNK), f32)
    out['d_spatial_b'] = 1.0 + 0.1 * nrm(ks[29], (N_ODD, D_GROUPS, D_CHUNK), f32)
    out['o_w_out'] = O_OUT_IN ** -0.5 * nrm(ks[30], (N_ODD, O_OUT_IN, D_MODEL), f32)
    return out


def reference(x_prompt, x_sample, c_prompt, c_sample, state_a_conv, cache_b_kv0, cache_b_kv1, cache_b_kv2,
              state_c_conv, rel_bias_table, ada_w, ada_b, norm_mix_g, norm_ffn_g, ffn_w_gate_up, ffn_w_down,
              final_norm_g, e_w_in, a_conv_w, a_conv_b, a_ln_g, a_ln_b, e_w_out, o_w_in, c_conv_w, c_conv_b,
              d_ln_g, d_ln_b, d_spatial_w, d_spatial_b, o_w_out):
    P = {'rel_bias_table': rel_bias_table, 'ada_w': ada_w, 'ada_b': ada_b, 'norm_mix_g': norm_mix_g,
         'norm_ffn_g': norm_ffn_g, 'ffn_w_gate_up': ffn_w_gate_up, 'ffn_w_down': ffn_w_down,
         'final_norm_g': final_norm_g, 'e_w_in': e_w_in, 'a_conv_w': a_conv_w, 'a_conv_b': a_conv_b,
         'a_ln_g': a_ln_g, 'a_ln_b': a_ln_b, 'e_w_out': e_w_out, 'o_w_in': o_w_in, 'c_conv_w': c_conv_w,
         'c_conv_b': c_conv_b, 'd_ln_g': d_ln_g, 'd_ln_b': d_ln_b, 'd_spatial_w': d_spatial_w,
         'd_spatial_b': d_spatial_b, 'o_w_out': o_w_out}
    y_prompt, a_p, kv_p, c_p, _ = _trunk(x_prompt, c_prompt, P, None, None, None)
    y_sample, a_s, kv_s, c_s, d_s = _trunk(x_sample, c_sample, P, state_a_conv,
                                           [cache_b_kv0, cache_b_kv1, cache_b_kv2], state_c_conv)
    return (y_prompt, y_sample, a_p, a_s, kv_p[0], kv_s[0], kv_p[1], kv_s[1], kv_p[2], kv_s[2],
            c_p, c_s, jnp.stack(d_s))
```

```python
import functools

import numpy as np
import jax
import jax.numpy as jnp
from jax import lax
from jax.experimental import pallas as pl
from jax.experimental.pallas import tpu as pltpu

F32 = jnp.float32
BF16 = jnp.bfloat16

D_MODEL = 1024
EPS = 1e-6
A_CH = 256
A_CONV = 31
HEAD_DIM = 64
HPG = 4
GROUP_CH = HPG * HEAD_DIM
WINDOWS = (128, 512, 2048)
DILATIONS = (1, 4, 16)
N_GROUPS = 3
SUB_WINDOW = 128
Q_BLOCK = 128
N_BUCKETS = 32
MAX_DISTANCE = 2048
C_CH = 512
C_CONV = 3
D_CH = 512
D_GROUPS = 4
D_GROUP_CH = D_CH // D_GROUPS
D_CHUNK = 128
FFN_HIDDEN = 2816
Q_OFF = 2 * A_CH
KV_OFF = Q_OFF + N_GROUPS * GROUP_CH
E_IN = KV_OFF + 2 * N_GROUPS * GROUP_CH
O_IN = 3 * C_CH + 2 * D_CH
LANES = 128
PERM_TILES = 3 * GROUP_CH // LANES
NEG = -1e30
ROW_TILE = 512
CONV_ROWS = 64
VMEM_LIMIT = 56 * 1024 * 1024


def _params(n_axes):
    return pltpu.CompilerParams(dimension_semantics=("arbitrary",) * n_axes, vmem_limit_bytes=VMEM_LIMIT)


def _const_spec(shape):
    return pl.BlockSpec(shape, lambda *_: (0,) * len(shape), pipeline_mode=pl.Buffered(1))


def _rms_mod(x, g, scale, shift):
    y = x * lax.rsqrt(jnp.mean(x * x, axis=-1, keepdims=True) + EPS) * g
    return y * (1.0 + scale) + shift


def _layernorm(x, g, b):
    mu = jnp.mean(x, axis=-1, keepdims=True)
    xc = x - mu
    var = jnp.mean(xc * xc, axis=-1, keepdims=True)
    return xc * lax.rsqrt(var + EPS) * g + b


def _silu(x):
    return x * jax.nn.sigmoid(x)


def _mm(a, b):
    return jnp.dot(a, b, preferred_element_type=F32)


def _mm_nt(a, b):
    return lax.dot_general(a, b, (((1,), (1,)), ((), ())), preferred_element_type=F32)


def _ada_kernel(c_ref, w_ref, b_ref, o_ref):
    cs = _silu(c_ref[...]).astype(BF16)
    o_ref[0] = _mm(cs, w_ref[0].astype(BF16)) + b_ref[0]


def _ada(c_all, ada_w, ada_b):
    depth, _, width = ada_w.shape
    nb = c_all.shape[0]
    tn = 1536
    return pl.pallas_call(
        _ada_kernel,
        grid=(depth, width // tn),
        in_specs=[_const_spec((nb, D_MODEL)),
                  pl.BlockSpec((1, D_MODEL, tn), lambda l, j: (l, 0, j)),
                  pl.BlockSpec((1, 1, tn), lambda l, j: (l, 0, j))],
        out_specs=pl.BlockSpec((1, nb, tn), lambda l, j: (l, 0, j)),
        out_shape=jax.ShapeDtypeStruct((depth, nb, width), F32),
        compiler_params=_params(2),
        name="ada_mod",
    )(c_all, ada_w, ada_b.reshape(depth, 1, width))


def _conv_ln_silu(ext_ref, base, rows, cw_ref, cb_ref, lg_ref, lb_ref, taps):
    acc = jnp.broadcast_to(cb_ref[...], (rows, cb_ref.shape[-1]))
    for k in range(taps):
        acc = acc + cw_ref[k:k + 1, :] * ext_ref[pl.ds(base + k, rows), :]
    return _silu(_layernorm(acc, lg_ref[...], lb_ref[...]))


def _in0_prompt_kernel(x_ref, mod_ref, g_ref, w_ref, cw_ref, cb_ref, lg_ref, lb_ref,
                       a_ref, ast_ref, q0_ref, q1_ref, q2_ref, kv0_ref, kv1_ref, kv2_ref,
                       st0_ref, st1_ref, st2_ref, ext_ref, perm_ref, *, tm):
    t = pl.program_id(1)
    last = pl.num_programs(1) - 1
    halo = A_CONV - 1
    pad = 32
    h = _rms_mod(x_ref[0], g_ref[...], mod_ref[0, 1], mod_ref[0, 0]).astype(BF16)
    proj = _mm(h, w_ref[...])

    @pl.when(t == 0)
    def _():
        ext_ref[0:pad, :] = jnp.zeros((pad, A_CH), F32)

    ext_ref[pad:pad + tm, :] = proj[:, :A_CH] * jax.nn.sigmoid(proj[:, A_CH:2 * A_CH])
    for c in range(tm // CONV_ROWS):
        r0 = c * CONV_ROWS
        y = _conv_ln_silu(ext_ref, pad - halo + r0, CONV_ROWS, cw_ref, cb_ref, lg_ref, lb_ref, A_CONV)
        a_ref[0, r0:r0 + CONV_ROWS, :] = y.astype(BF16)
    tail = ext_ref[pl.ds(tm + pad - halo, halo), :]
    ext_ref[pl.ds(pad - halo, halo), :] = tail

    @pl.when(t == last)
    def _():
        ast_ref[0] = tail

    q0_ref[0, 0] = proj[:, Q_OFF:Q_OFF + GROUP_CH].astype(BF16)
    kv0_ref[0, 0] = proj[:, KV_OFF:KV_OFF + 2 * GROUP_CH].astype(BF16)
    for g, (d, q_ref, kv_ref) in enumerate(zip(DILATIONS, (q0_ref, q1_ref, q2_ref), (kv0_ref, kv1_ref, kv2_ref))):
        if d == 1:
            continue
        for dst, off, tiles, slot in ((q_ref, Q_OFF + g * GROUP_CH, GROUP_CH // LANES, 0),
                                      (kv_ref, KV_OFF + 2 * g * GROUP_CH, 2 * GROUP_CH // LANES, GROUP_CH // LANES)):
            for c in range(tiles):
                tile = perm_ref.at[(g - 1) * PERM_TILES + slot + c]
                tile[...] = proj[:, off + c * LANES:off + (c + 1) * LANES]
                for r in range(d):
                    dst[0, r, :, c * LANES:(c + 1) * LANES] = tile[pl.ds(r, tm // d, stride=d), :].astype(BF16)

    st2_ref[0] = proj[:, KV_OFF + 4 * GROUP_CH:KV_OFF + 6 * GROUP_CH]

    @pl.when(t == last)
    def _():
        st0_ref[0] = proj[tm - WINDOWS[0]:, KV_OFF:KV_OFF + 2 * GROUP_CH]
        st1_ref[0] = proj[tm - WINDOWS[1]:, KV_OFF + 2 * GROUP_CH:KV_OFF + 4 * GROUP_CH]


def _in0_prompt(x, mod, g, w, cw, cb, lg, lb):
    nb, t, _ = x.shape
    tm = ROW_TILE
    assert t % tm == 0 and tm >= WINDOWS[1] and WINDOWS[2] == t
    row = lambda n, i: (n, i, 0)
    per_n = lambda n, i: (n, 0, 0)
    res = lambda n, i: (n, 0, i, 0)
    out_shape = [jax.ShapeDtypeStruct((nb, t, A_CH), BF16), jax.ShapeDtypeStruct((nb, A_CONV - 1, A_CH), F32)]
    out_specs = [pl.BlockSpec((1, tm, A_CH), row), pl.BlockSpec((1, A_CONV - 1, A_CH), per_n)]
    for width in (GROUP_CH, 2 * GROUP_CH):
        for d in DILATIONS:
            out_shape.append(jax.ShapeDtypeStruct((nb, d, t // d, width), BF16))
            out_specs.append(pl.BlockSpec((1, d, tm // d, width), res))
    out_shape += [jax.ShapeDtypeStruct((nb, WINDOWS[0], 2 * GROUP_CH), F32),
                  jax.ShapeDtypeStruct((nb, WINDOWS[1], 2 * GROUP_CH), F32),
                  jax.ShapeDtypeStruct((nb, t, 2 * GROUP_CH), F32)]
    out_specs += [pl.BlockSpec((1, WINDOWS[0], 2 * GROUP_CH), per_n),
                  pl.BlockSpec((1, WINDOWS[1], 2 * GROUP_CH), per_n),
                  pl.BlockSpec((1, tm, 2 * GROUP_CH), row)]
    return pl.pallas_call(
        functools.partial(_in0_prompt_kernel, tm=tm),
        grid=(nb, t // tm),
        in_specs=[pl.BlockSpec((1, tm, D_MODEL), row),
                  pl.BlockSpec((1, 6, 1, D_MODEL), lambda n, i: (n, 0, 0, 0)),
                  _const_spec((1, D_MODEL)), _const_spec((D_MODEL, E_IN)),
                  _const_spec((A_CONV, A_CH)), _const_spec((1, A_CH)),
                  _const_spec((1, A_CH)), _const_spec((1, A_CH))],
        out_specs=out_specs,
        out_shape=out_shape,
        scratch_shapes=[pltpu.VMEM((tm + 32, A_CH), F32), pltpu.VMEM((2 * PERM_TILES, tm, LANES), F32)],
        compiler_params=_params(2),
        name="l0_in_prompt",
    )(x, mod, g, w, cw, cb, lg, lb)


def _head_masks(rows):
    col_head = lax.broadcasted_iota(jnp.int32, (rows, GROUP_CH), 1) // HEAD_DIM
    return [col_head == h for h in range(HPG)]


def _attn_prompt_kernel(q_ref, kv_ref, bias_ref, o_ref, lse_ref, *, d, nblk):
    masks = _head_masks(Q_BLOCK)
    masks_bf = [jnp.where(m, 1.0, 0.0).astype(BF16) for m in masks]
    key_idx = lax.broadcasted_iota(jnp.int32, (Q_BLOCK, 2 * Q_BLOCK), 1)

    def block(j, carry):
        r = j // nblk
        i = j % nblk
        rows = pl.ds(pl.multiple_of(i * Q_BLOCK, Q_BLOCK), Q_BLOCK)
        prev = pl.ds(pl.multiple_of(jnp.maximum(i - 1, 0) * Q_BLOCK, Q_BLOCK), Q_BLOCK)
        q = q_ref[0, r, rows, :]
        kk = jnp.concatenate([kv_ref[0, r, prev, :], kv_ref[0, r, rows, :]], axis=0)
        k = kk[:, :GROUP_CH]
        v = kk[:, GROUP_CH:]
        pen = jnp.where(key_idx < jnp.where(i == 0, Q_BLOCK, 0), NEG, 0.0)
        o_acc = jnp.zeros((Q_BLOCK, GROUP_CH), F32)
        lse_acc = jnp.zeros((Q_BLOCK, GROUP_CH), F32)
        for h in range(HPG):
            s = _mm_nt(q * masks_bf[h], k) * (HEAD_DIM ** -0.5) + bias_ref[h] + pen
            m = jnp.max(s, axis=-1, keepdims=True)
            e = jnp.exp(s - m)
            l = jnp.sum(e, axis=-1, keepdims=True)
            p = (e * (1.0 / l)).astype(BF16)
            o_acc = jnp.where(masks[h], _mm(p, v), o_acc)
            lse_acc = jnp.where(masks[h], m + jnp.log(l), lse_acc)
        o_ref[0, r, rows, :] = o_acc
        lse_ref[0, r, rows, :] = lse_acc
        return carry

    lax.fori_loop(0, d * nblk, block, 0)


def _attn_prompt(q, kv, bias, d):
    nb, _, l, _ = q.shape
    blk = lambda n: (n, 0, 0, 0)
    out = jax.ShapeDtypeStruct((nb, d, l, GROUP_CH), F32)
    return pl.pallas_call(
        functools.partial(_attn_prompt_kernel, d=d, nblk=l // Q_BLOCK),
        grid=(nb,),
        in_specs=[pl.BlockSpec((1, d, l, GROUP_CH), blk), pl.BlockSpec((1, d, l, 2 * GROUP_CH), blk),
                  _const_spec((HPG, Q_BLOCK, 2 * Q_BLOCK))],
        out_specs=[pl.BlockSpec((1, d, l, GROUP_CH), blk)] * 2,
        out_shape=[out, out],
        compiler_params=_params(1),
        name="l0_attn_prompt_d%d" % d,
    )(q, kv, bias)


def _mix_groups(outs, lses):
    m = jnp.maximum(jnp.maximum(lses[0], lses[1]), lses[2])
    es = [jnp.exp(l - m) for l in lses]
    inv = 1.0 / (es[0] + es[1] + es[2])
    return (es[0] * inv) * outs[0] + (es[1] * inv) * outs[1] + (es[2] * inv) * outs[2]


def _out0_prompt_kernel(x_ref, mod_ref, a_ref, o0_ref, o1_ref, o2_ref, l0_ref, l1_ref, l2_ref, w_ref,
                        y_ref, os1_ref, os2_ref, ls1_ref, ls2_ref, *, tm):
    tiles = GROUP_CH // LANES
    for d, src, dst in ((DILATIONS[1], o1_ref, os1_ref), (DILATIONS[1], l1_ref, ls1_ref),
                        (DILATIONS[2], o2_ref, os2_ref), (DILATIONS[2], l2_ref, ls2_ref)):
        for r in range(d):
            for c in range(tiles):
                dst[c, pl.ds(r, tm // d, stride=d), :] = src[0, r, :, c * LANES:(c + 1) * LANES]

    def ordered(ref):
        return jnp.concatenate([ref[c] for c in range(tiles)], axis=1)

    attn = _mix_groups([o0_ref[0, 0], ordered(os1_ref), ordered(os2_ref)],
                       [l0_ref[0, 0], ordered(ls1_ref), ordered(ls2_ref)])
    cat = jnp.concatenate([a_ref[0], attn.astype(BF16)], axis=1)
    y_ref[0] = x_ref[0] + mod_ref[0, 2] * _mm(cat, w_ref[...])


def _out0_prompt(x, mod, a, outs, lses, w):
    nb, t, _ = x.shape
    tm = ROW_TILE
    row = lambda n, i: (n, i, 0)
    res = lambda n, i: (n, 0, i, 0)
    grp_specs = [pl.BlockSpec((1, d, tm // d, GROUP_CH), res) for d in DILATIONS]
    return pl.pallas_call(
        functools.partial(_out0_prompt_kernel, tm=tm),
        grid=(nb, t // tm),
        in_specs=[pl.BlockSpec((1, tm, D_MODEL), row),
                  pl.BlockSpec((1, 6, 1, D_MODEL), lambda n, i: (n, 0, 0, 0)),
                  pl.BlockSpec((1, tm, A_CH), row)] + grp_specs + grp_specs
                 + [_const_spec((A_CH + GROUP_CH, D_MODEL))],
        out_specs=pl.BlockSpec((1, tm, D_MODEL), row),
        out_shape=jax.ShapeDtypeStruct((nb, t, D_MODEL), F32),
        scratch_shapes=[pltpu.VMEM((GROUP_CH // LANES, tm, LANES), F32)] * 4,
        compiler_params=_params(2),
        name="l0_out_prompt",
    )(x, mod, a, *outs, *lses, w)


def _ffn_kernel(x_ref, mod_ref, g_ref, wgu_ref, wd_ref, fg_ref, y_ref, *, final, chunk):
    x = x_ref[0]
    h = _rms_mod(x, g_ref[...], mod_ref[0, 4], mod_ref[0, 3]).astype(BF16)
    acc = jnp.zeros(x.shape, F32)
    for c0 in range(0, FFN_HIDDEN, chunk):
        c1 = min(c0 + chunk, FFN_HIDDEN)
        gate = _mm(h, wgu_ref[:, c0:c1])
        up = _mm(h, wgu_ref[:, FFN_HIDDEN + c0:FFN_HIDDEN + c1])
        acc = acc + _mm((_silu(gate) * up).astype(BF16), wd_ref[c0:c1, :])
    y = x + mod_ref[0, 5] * acc
    if final:
        y = y * lax.rsqrt(jnp.mean(y * y, axis=-1, keepdims=True) + EPS) * fg_ref[...]
    y_ref[0] = y


def _ffn(x, mod, g, wgu, wd, final_g, final):
    nb, t, _ = x.shape
    tm = min(ROW_TILE, t)
    rm = mod.shape[2]
    row = lambda n, i: (n, i, 0)
    mod_map = (lambda n, i: (n, 0, 0, 0)) if rm == 1 else (lambda n, i: (n, 0, i, 0))
    return pl.pallas_call(
        functools.partial(_ffn_kernel, final=final, chunk=512),
        grid=(nb, t // tm),
        in_specs=[pl.BlockSpec((1, tm, D_MODEL), row),
                  pl.BlockSpec((1, 6, min(rm, tm), D_MODEL), mod_map),
                  _const_spec((1, D_MODEL)),
                  _const_spec((D_MODEL, 2 * FFN_HIDDEN)), _const_spec((FFN_HIDDEN, D_MODEL)),
                  _const_spec((1, D_MODEL))],
        out_specs=pl.BlockSpec((1, tm, D_MODEL), row),
        out_shape=jax.ShapeDtypeStruct((nb, t, D_MODEL), F32),
        compiler_params=_params(2),
        name="ffn_final" if final else "ffn",
    )(x, mod, g, wgu, wd, final_g)


def _spatial_tril(ws_ref, grp):
    i = lax.broadcasted_iota(jnp.int32, (D_CHUNK, D_CHUNK), 0)
    j = lax.broadcasted_iota(jnp.int32, (D_CHUNK, D_CHUNK), 1)
    return jnp.where(j <= i, ws_ref[grp], 0.0).astype(BF16)


def _in1_prompt_kernel(x_ref, mod_ref, g_ref, w_ref, cw_ref, cb_ref, lg_ref, lb_ref, ws_ref, bs_ref, wo_ref,
                       y_ref, cst_ref, ext_ref, cat_ref, *, tm):
    t = pl.program_id(1)
    last = pl.num_programs(1) - 1
    halo = C_CONV - 1
    pad = 8
    x = x_ref[0]
    h = _rms_mod(x, g_ref[...], mod_ref[0, 1], mod_ref[0, 0]).astype(BF16)
    proj = _mm(h, w_ref[...])
    bg = proj[:, :C_CH]

    @pl.when(t == 0)
    def _():
        ext_ref[0:pad, :] = jnp.zeros((pad, C_CH), F32)

    ext_ref[pad:pad + tm, :] = proj[:, C_CH:2 * C_CH] * proj[:, 2 * C_CH:3 * C_CH]
    conv = jnp.broadcast_to(cb_ref[...], (tm, C_CH))
    for k in range(C_CONV):
        conv = conv + cw_ref[k:k + 1, :] * ext_ref[pl.ds(pad - halo + k, tm), :]
    cat_ref[:, :C_CH] = (bg * conv).astype(BF16)
    tail = ext_ref[pl.ds(tm + pad - halo, halo), :]
    ext_ref[pl.ds(pad - halo, halo), :] = tail

    @pl.when(t == last)
    def _():
        cst_ref[0] = tail

    uv = jax.nn.gelu(proj[:, 3 * C_CH:], approximate=True)
    u = uv[:, :D_CH]
    v = _layernorm(uv[:, D_CH:], lg_ref[...], lb_ref[...]).astype(BF16)
    for grp in range(D_GROUPS):
        ws = _spatial_tril(ws_ref, grp)
        cols = slice(grp * D_GROUP_CH, (grp + 1) * D_GROUP_CH)
        for c in range(tm // D_CHUNK):
            rows = slice(c * D_CHUNK, (c + 1) * D_CHUNK)
            mix = _mm(ws, v[rows, cols]) + bs_ref[:, cols]
            cat_ref[rows, C_CH + grp * D_GROUP_CH:C_CH + (grp + 1) * D_GROUP_CH] = (u[rows, cols] * mix).astype(BF16)
    y_ref[0] = x + mod_ref[0, 2] * _mm(cat_ref[...], wo_ref[...])


def _in1_prompt(x, mod, g, w, cw, cb, lg, lb, ws, bs_full, wo):
    nb, t, _ = x.shape
    tm = ROW_TILE
    row = lambda n, i: (n, i, 0)
    return pl.pallas_call(
        functools.partial(_in1_prompt_kernel, tm=tm),
        grid=(nb, t // tm),
        in_specs=[pl.BlockSpec((1, tm, D_MODEL), row),
                  pl.BlockSpec((1, 6, 1, D_MODEL), lambda n, i: (n, 0, 0, 0)),
                  _const_spec((1, D_MODEL)), _const_spec((D_MODEL, O_IN)),
                  _const_spec((C_CONV, C_CH)), _const_spec((1, C_CH)),
                  _const_spec((1, D_CH)), _const_spec((1, D_CH)),
                  _const_spec((D_GROUPS, D_CHUNK, D_CHUNK)), _const_spec((D_CHUNK, D_CH)),
                  _const_spec((C_CH + D_CH, D_MODEL))],
        out_specs=[pl.BlockSpec((1, tm, D_MODEL), row),
                   pl.BlockSpec((1, C_CONV - 1, C_CH), lambda n, i: (n, 0, 0))],
        out_shape=[jax.ShapeDtypeStruct((nb, t, D_MODEL), F32),
                   jax.ShapeDtypeStruct((nb, C_CONV - 1, C_CH), F32)],
        scratch_shapes=[pltpu.VMEM((tm + 8, C_CH), F32), pltpu.VMEM((tm, C_CH + D_CH), BF16)],
        compiler_params=_params(2),
        name="l1_mixer_prompt",
    )(x, mod, g, w, cw, cb, lg, lb, ws, bs_full, wo)


def _in0_sample_kernel(x_ref, mod_ref, g_ref, w_ref, abuf_ref, cw_ref, cb_ref, lg_ref, lb_ref,
                       a_ref, ast_ref, q_ref, kv_ref, ext_ref, *, nb, steps):
    rows = nb * steps
    hist = (A_CONV - 1) * nb
    h = _rms_mod(x_ref[0], g_ref[...], mod_ref[0, 1], mod_ref[0, 0]).astype(BF16)
    proj = _mm(h, w_ref[...])
    ext_ref[0:hist, :] = abuf_ref[...]
    ext_ref[hist:hist + rows, :] = proj[:, :A_CH] * jax.nn.sigmoid(proj[:, A_CH:2 * A_CH])
    acc = jnp.broadcast_to(cb_ref[...], (rows, A_CH))
    for k in range(A_CONV):
        acc = acc + cw_ref[k:k + 1, :] * ext_ref[k * nb:k * nb + rows, :]
    a_ref[...] = _silu(_layernorm(acc, lg_ref[...], lb_ref[...])).astype(BF16)
    ast_ref[...] = ext_ref[rows:rows + hist, :]
    q_ref[...] = proj[:, Q_OFF:KV_OFF]
    kv_ref[...] = proj[:, KV_OFF:]


def _in0_sample(x, mod, g, w, abuf_t, cw, cb, lg, lb, nb, steps):
    rows = nb * steps
    hist = (A_CONV - 1) * nb
    return pl.pallas_call(
        functools.partial(_in0_sample_kernel, nb=nb, steps=steps),
        grid=(1,),
        in_specs=[_const_spec((1, rows, D_MODEL)), _const_spec((1, 6, rows, D_MODEL)),
                  _const_spec((1, D_MODEL)), _const_spec((D_MODEL, E_IN)), _const_spec((hist, A_CH)),
                  _const_spec((A_CONV, A_CH)), _const_spec((1, A_CH)), _const_spec((1, A_CH)),
                  _const_spec((1, A_CH))],
        out_specs=[_const_spec((rows, A_CH)), _const_spec((hist, A_CH)),
                   _const_spec((rows, N_GROUPS * GROUP_CH)), _const_spec((rows, 2 * N_GROUPS * GROUP_CH))],
        out_shape=[jax.ShapeDtypeStruct((rows, A_CH), BF16), jax.ShapeDtypeStruct((hist, A_CH), F32),
                   jax.ShapeDtypeStruct((rows, N_GROUPS * GROUP_CH), F32),
                   jax.ShapeDtypeStruct((rows, 2 * N_GROUPS * GROUP_CH), F32)],
        scratch_shapes=[pltpu.VMEM((hist + rows, A_CH), F32)],
        compiler_params=_params(1),
        name="l0_in_sample",
    )(x, mod, g, w, abuf_t, cw, cb, lg, lb)


def _attn_sample_kernel(q_ref, new_ref, c0_ref, c1_ref, c2_ref, b0_ref, b1_ref, b2_ref,
                        attn_ref, s0_ref, s1_ref, s2_ref, *, steps, pad):
    rows = HPG * steps
    row_head = lax.broadcasted_iota(jnp.int32, (rows, GROUP_CH), 0) // steps
    col_head = lax.broadcasted_iota(jnp.int32, (rows, GROUP_CH), 1) // HEAD_DIM
    step_masks = _head_masks(steps)
    outs, lses = [], []
    for g, (c_ref, b_ref, s_ref) in enumerate(zip((c0_ref, c1_ref, c2_ref), (b0_ref, b1_ref, b2_ref),
                                                  (s0_ref, s1_ref, s2_ref))):
        w = c_ref.shape[-1]
        comb = jnp.concatenate([c_ref[0], new_ref[0, 2 * g * GROUP_CH:2 * (g + 1) * GROUP_CH, :]], axis=1)
        s_ref[0] = pltpu.roll(comb, w + pad - steps, axis=1)[:, :w]
        comb_bf = comb.astype(BF16)
        q = q_ref[0][:, g * GROUP_CH:(g + 1) * GROUP_CH]
        q_bd = jnp.where(row_head == col_head, jnp.concatenate([q] * HPG, axis=0), 0.0).astype(BF16)
        s = _mm(q_bd, comb_bf[:GROUP_CH]) * (HEAD_DIM ** -0.5) + b_ref[...]
        m = jnp.max(s, axis=-1, keepdims=True)
        e = jnp.exp(s - m)
        l = jnp.sum(e, axis=-1, keepdims=True)
        p = (e * (1.0 / l)).astype(BF16)
        o_all = _mm_nt(p, comb_bf[GROUP_CH:])
        lse_all = jnp.broadcast_to(m + jnp.log(l), (rows, GROUP_CH))
        o = jnp.zeros((steps, GROUP_CH), F32)
        lse = jnp.zeros((steps, GROUP_CH), F32)
        for h in range(HPG):
            o = jnp.where(step_masks[h], o_all[h * steps:(h + 1) * steps], o)
            lse = jnp.where(step_masks[h], lse_all[h * steps:(h + 1) * steps], lse)
        outs.append(o)
        lses.append(lse)
    attn_ref[0] = _mix_groups(outs, lses)


def _attn_sample(q, new_t, caches, biases, steps, pad):
    nb = q.shape[0]
    per_n = lambda n: (n, 0, 0)
    cache_specs = [pl.BlockSpec((1, 2 * GROUP_CH, w), per_n) for w in WINDOWS]
    return pl.pallas_call(
        functools.partial(_attn_sample_kernel, steps=steps, pad=pad),
        grid=(nb,),
        in_specs=[pl.BlockSpec((1, steps, N_GROUPS * GROUP_CH), per_n),
                  pl.BlockSpec((1, 2 * N_GROUPS * GROUP_CH, pad), per_n)] + cache_specs
                 + [_const_spec((HPG * steps, w + pad)) for w in WINDOWS],
        out_specs=[pl.BlockSpec((1, steps, GROUP_CH), per_n)] + cache_specs,
        out_shape=[jax.ShapeDtypeStruct((nb, steps, GROUP_CH), F32)]
                  + [jax.ShapeDtypeStruct((nb, 2 * GROUP_CH, w), F32) for w in WINDOWS],
        compiler_params=_params(1),
        name="l0_attn_sample",
    )(q, new_t, *caches, *biases)


def _out0_sample_kernel(x_ref, mod_ref, a_ref, attn_ref, w_ref, y_ref):
    cat = jnp.concatenate([a_ref[...], attn_ref[...].astype(BF16)], axis=1)
    y_ref[0] = x_ref[0] + mod_ref[0, 2] * _mm(cat, w_ref[...])


def _out0_sample(x, mod, a, attn, w):
    _, rows, _ = x.shape
    return pl.pallas_call(
        _out0_sample_kernel,
        grid=(1,),
        in_specs=[_const_spec((1, rows, D_MODEL)), _const_spec((1, 6, rows, D_MODEL)),
                  _const_spec((rows, A_CH)), _const_spec((rows, GROUP_CH)),
                  _const_spec((A_CH + GROUP_CH, D_MODEL))],
        out_specs=_const_spec((1, rows, D_MODEL)),
        out_shape=jax.ShapeDtypeStruct((1, rows, D_MODEL), F32),
        compiler_params=_params(1),
        name="l0_out_sample",
    )(x, mod, a, attn, w)


def _in1_sample_kernel(x_ref, mod_ref, g_ref, w_ref, cbuf_ref, cw_ref, cb_ref, lg_ref, lb_ref, coef_ref, bs_ref,
                       wo_ref, y_ref, cst_ref, dv_ref, ext_ref, *, nb, steps):
    rows = nb * steps
    hist = (C_CONV - 1) * nb
    x = x_ref[0]
    h = _rms_mod(x, g_ref[...], mod_ref[0, 1], mod_ref[0, 0]).astype(BF16)
    proj = _mm(h, w_ref[...])
    ext_ref[0:hist, :] = cbuf_ref[...]
    ext_ref[hist:hist + rows, :] = proj[:, C_CH:2 * C_CH] * proj[:, 2 * C_CH:3 * C_CH]
    conv = jnp.broadcast_to(cb_ref[...], (rows, C_CH))
    for k in range(C_CONV):
        conv = conv + cw_ref[k:k + 1, :] * ext_ref[k * nb:k * nb + rows, :]
    yc = proj[:, :C_CH] * conv
    cst_ref[...] = ext_ref[rows:rows + hist, :]
    uv = jax.nn.gelu(proj[:, 3 * C_CH:], approximate=True)
    v = _layernorm(uv[:, D_CH:], lg_ref[...], lb_ref[...])
    dv_ref[...] = v
    v_r = v.astype(BF16).astype(F32)
    coef = coef_ref[...].astype(BF16).astype(F32)
    mixes = []
    for i in range(steps):
        mix = jnp.broadcast_to(bs_ref[i:i + 1, :], (nb, D_CH))
        for j in range(i + 1):
            mix = mix + coef[i * steps + j:i * steps + j + 1, :] * v_r[j * nb:(j + 1) * nb, :]
        mixes.append(mix)
    yd = uv[:, :D_CH] * jnp.concatenate(mixes, axis=0)
    cat = jnp.concatenate([yc.astype(BF16), yd.astype(BF16)], axis=1)
    y_ref[0] = x + mod_ref[0, 2] * _mm(cat, wo_ref[...])


def _in1_sample(x, mod, g, w, cbuf_t, cw, cb, lg, lb, coef, bs_rows, wo, nb, steps):
    rows = nb * steps
    hist = (C_CONV - 1) * nb
    return pl.pallas_call(
        functools.partial(_in1_sample_kernel, nb=nb, steps=steps),
        grid=(1,),
        in_specs=[_const_spec((1, rows, D_MODEL)), _const_spec((1, 6, rows, D_MODEL)),
                  _const_spec((1, D_MODEL)), _const_spec((D_MODEL, O_IN)), _const_spec((hist, C_CH)),
                  _const_spec((C_CONV, C_CH)), _const_spec((1, C_CH)), _const_spec((1, D_CH)),
                  _const_spec((1, D_CH)), _const_spec((steps * steps, D_CH)), _const_spec((steps, D_CH)),
                  _const_spec((C_CH + D_CH, D_MODEL))],
        out_specs=[_const_spec((1, rows, D_MODEL)), _const_spec((hist, C_CH)), _const_spec((rows, D_CH))],
        out_shape=[jax.ShapeDtypeStruct((1, rows, D_MODEL), F32), jax.ShapeDtypeStruct((hist, C_CH), F32),
                   jax.ShapeDtypeStruct((rows, D_CH), F32)],
        scratch_shapes=[pltpu.VMEM((hist + rows, C_CH), F32)],
        compiler_params=_params(1),
        name="l1_mixer_sample",
    )(x, mod, g, w, cbuf_t, cw, cb, lg, lb, coef, bs_rows, wo)


def _t5_bucket(dist):
    n = np.maximum(np.asarray(dist, dtype=np.int64), 0)
    max_exact = N_BUCKETS // 2
    large = max_exact + (np.log(np.maximum(n, 1) / max_exact) / np.log(MAX_DISTANCE / max_exact)
                         * (N_BUCKETS - max_exact)).astype(np.int32)
    return np.where(n < max_exact, n, np.minimum(large, N_BUCKETS - 1)).astype(np.int32)


def _prompt_bias(table, grp):
    qi = np.arange(Q_BLOCK)[:, None]
    ki = np.arange(2 * Q_BLOCK)[None, :]
    rel = qi + Q_BLOCK - ki
    band = (rel >= 0) & (rel <= SUB_WINDOW)
    tab = table[:, grp * HPG:(grp + 1) * HPG]
    bias = jnp.moveaxis(tab[_t5_bucket(np.clip(rel, 0, None) * DILATIONS[grp])], -1, 0)
    return jnp.where(band[None], bias, NEG).astype(F32)


def _sample_bias(table, grp, steps, pad):
    w, d = WINDOWS[grp], DILATIONS[grp]
    pos = np.arange(w + pad)[None, :]
    delta = w + np.arange(steps)[:, None] - pos
    valid = (pos < w + steps) & (delta >= 0) & (delta % d == 0) & (delta // d <= SUB_WINDOW)
    tab = table[:, grp * HPG:(grp + 1) * HPG]
    bias = jnp.moveaxis(tab[_t5_bucket(np.clip(delta, 0, None))], -1, 0)
    return jnp.where(valid[None], bias, NEG).astype(F32).reshape(HPG * steps, w + pad)


def _step_major(x):
    return jnp.swapaxes(x, 0, 1).reshape((x.shape[0] * x.shape[1],) + x.shape[2:])


def _batch_major(x, nb, steps):
    return jnp.swapaxes(x.reshape((steps, nb) + x.shape[1:]), 0, 1)


def kernel(x_prompt, x_sample, c_prompt, c_sample, state_a_conv, cache_b_kv0, cache_b_kv1, cache_b_kv2, state_c_conv, rel_bias_table, ada_w, ada_b, norm_mix_g, norm_ffn_g, ffn_w_gate_up, ffn_w_down, final_norm_g, e_w_in, a_conv_w, a_conv_b, a_ln_g, a_ln_b, e_w_out, o_w_in, c_conv_w, c_conv_b, d_ln_g, d_ln_b, d_spatial_w, d_spatial_b, o_w_out):
    nb_p = x_prompt.shape[0]
    nb_s, steps, _ = x_sample.shape
    pad = 128

    w_e = e_w_in[0]
    qkv = w_e[:, Q_OFF:].reshape(D_MODEL, 3, N_GROUPS, GROUP_CH)
    kv_cols = jnp.stack([qkv[:, 1], qkv[:, 2]], axis=2).reshape(D_MODEL, 2 * N_GROUPS * GROUP_CH)
    w_in0 = jnp.concatenate([w_e[:, :Q_OFF], qkv[:, 0].reshape(D_MODEL, N_GROUPS * GROUP_CH), kv_cols],
                            axis=1).astype(BF16)
    w_out0 = e_w_out[0].astype(BF16)
    w_in1 = o_w_in[0].astype(BF16)
    w_out1 = o_w_out[0].astype(BF16)
    w_gu = ffn_w_gate_up.astype(BF16)
    w_dn = ffn_w_down.astype(BF16)
    fin_g = final_norm_g.reshape(1, D_MODEL)

    mod = _ada(jnp.concatenate([c_prompt, c_sample], axis=0), ada_w, ada_b)
    mod = mod.reshape(mod.shape[0], nb_p + nb_s, 6, D_MODEL)
    mod_p = [mod[l, :nb_p].reshape(nb_p, 6, 1, D_MODEL) for l in range(2)]
    mod_s = [jnp.swapaxes(jnp.tile(mod[l, nb_p:], (steps, 1, 1)), 0, 1)[None] for l in range(2)]

    (a, a_st_p, q0, q1, q2, kv0, kv1, kv2, st0, st1, st2) = _in0_prompt(
        x_prompt, mod_p[0], norm_mix_g[0:1], w_in0, a_conv_w[0], a_conv_b, a_ln_g, a_ln_b)
    outs, lses = [], []
    for grp, (q, kv) in enumerate(((q0, kv0), (q1, kv1), (q2, kv2))):
        o, lse = _attn_prompt(q, kv, _prompt_bias(rel_bias_table, grp), DILATIONS[grp])
        outs.append(o)
        lses.append(lse)
    xp = _out0_prompt(x_prompt, mod_p[0], a, outs, lses, w_out0)
    xp = _ffn(xp, mod_p[0], norm_ffn_g[0:1], w_gu[0], w_dn[0], fin_g, False)
    bs_full = jnp.repeat(d_spatial_b[0].T, D_GROUP_CH, axis=1)
    xp, c_st_p = _in1_prompt(xp, mod_p[1], norm_mix_g[1:2], w_in1, c_conv_w[0], c_conv_b, d_ln_g, d_ln_b,
                             d_spatial_w[0], bs_full, w_out1)
    y_prompt = _ffn(xp, mod_p[1], norm_ffn_g[1:2], w_gu[1], w_dn[1], fin_g, True)

    xs = _step_major(x_sample)[None]
    abuf_t = _step_major(state_a_conv[0])
    a_s, a_st_s, q_s, kv_s = _in0_sample(xs, mod_s[0], norm_mix_g[0:1], w_in0, abuf_t, a_conv_w[0], a_conv_b,
                                         a_ln_g, a_ln_b, nb_s, steps)
    q_b = _batch_major(q_s, nb_s, steps)
    new_t = jnp.transpose(kv_s.reshape(steps, nb_s, -1), (1, 2, 0))
    new_t = jnp.pad(new_t, ((0, 0), (0, 0), (0, pad - steps)))
    caches = [jnp.transpose(c[0], (0, 2, 3, 4, 1)).reshape(nb_s, 2 * GROUP_CH, c.shape[2])
              for c in (cache_b_kv0, cache_b_kv1, cache_b_kv2)]
    biases = [_sample_bias(rel_bias_table, grp, steps, pad) for grp in range(N_GROUPS)]
    attn_s, s0, s1, s2 = _attn_sample(q_b, new_t, caches, biases, steps, pad)
    xs = _out0_sample(xs, mod_s[0], a_s, _step_major(attn_s), w_out0)
    xs = _ffn(xs, mod_s[0], norm_ffn_g[0:1], w_gu[0], w_dn[0], fin_g, False)
    cbuf_t = _step_major(state_c_conv[0])
    coef = jnp.repeat(jnp.transpose(d_spatial_w[0][:, :steps, :steps], (1, 2, 0)).reshape(steps * steps, D_GROUPS),
                      D_GROUP_CH, axis=1)
    xs, c_st_s, dv_s = _in1_sample(xs, mod_s[1], norm_mix_g[1:2], w_in1, cbuf_t, c_conv_w[0], c_conv_b, d_ln_g,
                                   d_ln_b, coef, bs_full[:steps], w_out1, nb_s, steps)
    y_sample = _ffn(xs, mod_s[1], norm_ffn_g[1:2], w_gu[1], w_dn[1], fin_g, True)

    def kv_state_prompt(st):
        return st.reshape(1, nb_p, st.shape[1], 2, HPG, HEAD_DIM)

    def kv_state_sample(st):
        return jnp.transpose(st.reshape(nb_s, 2, HPG, HEAD_DIM, st.shape[-1]), (0, 4, 1, 2, 3))[None]

    return (y_prompt, _batch_major(y_sample[0], nb_s, steps),
            a_st_p[None], _batch_major(a_st_s, nb_s, A_CONV - 1)[None],
            kv_state_prompt(st0), kv_state_sample(s0), kv_state_prompt(st1), kv_state_sample(s1),
            kv_state_prompt(st2), kv_state_sample(s2),
            c_st_p[None], _batch_major(c_st_s, nb_s, C_CONV - 1)[None],
            _batch_major(dv_s, nb_s, steps)[None])
```

```python
import functools

import numpy as np
import jax
import jax.numpy as jnp
from jax import lax
from jax.experimental import pallas as pl
from jax.experimental.pallas import tpu as pltpu

F32 = jnp.float32
BF16 = jnp.bfloat16

D_MODEL = 1024
EPS = 1e-6
A_CH = 256
A_CONV = 31
HEAD_DIM = 64
HPG = 4
GROUP_CH = HPG * HEAD_DIM
WINDOWS = (128, 512, 2048)
DILATIONS = (1, 4, 16)
N_GROUPS = 3
SUB_WINDOW = 128
Q_BLOCK = 128
N_BUCKETS = 32
MAX_DISTANCE = 2048
C_CH = 512
C_CONV = 3
D_CH = 512
D_GROUPS = 4
D_GROUP_CH = D_CH // D_GROUPS
D_CHUNK = 128
FFN_HIDDEN = 2816
Q_OFF = 2 * A_CH
KV_OFF = Q_OFF + N_GROUPS * GROUP_CH
E_IN = KV_OFF + 2 * N_GROUPS * GROUP_CH
O_IN = 3 * C_CH + 2 * D_CH
LANES = 128
SUBLANES = 8
PERM_TILES = 3 * GROUP_CH // LANES
NEG = -1e30
ROW_TILE = 512
CONV_ROWS = 64
VMEM_LIMIT = 56 * 1024 * 1024


def _params(n_axes):
    return pltpu.CompilerParams(dimension_semantics=("arbitrary",) * n_axes, vmem_limit_bytes=VMEM_LIMIT)


def _const_spec(shape):
    return pl.BlockSpec(shape, lambda *_: (0,) * len(shape), pipeline_mode=pl.Buffered(1))


def _rms_mod(x, g, scale, shift):
    return (x * lax.rsqrt(jnp.mean(x * x, axis=-1, keepdims=True) + EPS)) * (g * (1.0 + scale)) + shift


def _layernorm(x, g, b):
    mu = jnp.mean(x, axis=-1, keepdims=True)
    xc = x - mu
    var = jnp.mean(xc * xc, axis=-1, keepdims=True)
    return xc * lax.rsqrt(var + EPS) * g + b


def _silu(x):
    return x * jax.nn.sigmoid(x)


def _mm(a, b):
    return jnp.dot(a, b, preferred_element_type=F32)


def _mm_nt(a, b):
    return lax.dot_general(a, b, (((1,), (1,)), ((), ())), preferred_element_type=F32)


def _ada_kernel(c_ref, w_ref, b_ref, o_ref):
    cs = _silu(c_ref[...]).astype(BF16)
    o_ref[0] = _mm(cs, w_ref[0].astype(BF16)) + b_ref[0]


def _ada(c_all, ada_w, ada_b):
    depth, _, width = ada_w.shape
    nb = c_all.shape[0]
    tn = 1536
    return pl.pallas_call(
        _ada_kernel,
        grid=(depth, width // tn),
        in_specs=[_const_spec((nb, D_MODEL)),
                  pl.BlockSpec((1, D_MODEL, tn), lambda l, j: (l, 0, j)),
                  pl.BlockSpec((1, 1, tn), lambda l, j: (l, 0, j))],
        out_specs=pl.BlockSpec((1, nb, tn), lambda l, j: (l, 0, j)),
        out_shape=jax.ShapeDtypeStruct((depth, nb, width), F32),
        compiler_params=_params(2),
        name="ada_mod",
    )(c_all, ada_w, ada_b.reshape(depth, 1, width))


def _conv_ln_silu(ext_ref, shift_ref, base, rows, cw_ref, cb_ref, lg_ref, lb_ref, taps):
    acc = jnp.broadcast_to(cb_ref[...], (rows, cb_ref.shape[-1]))
    for k in range(taps):
        m = (base + k) % SUBLANES
        src = ext_ref if m == 0 else shift_ref.at[m - 1]
        acc = acc + cw_ref[k:k + 1, :] * src[pl.ds(base + k - m, rows), :]
    return _silu(_layernorm(acc, lg_ref[...], lb_ref[...]))


def _in0_prompt_kernel(x_ref, mod_ref, g_ref, w_ref, cw_ref, cb_ref, lg_ref, lb_ref,
                       a_ref, ast_ref, q0_ref, q1_ref, q2_ref, kv0_ref, kv1_ref, kv2_ref,
                       st0_ref, st1_ref, st2_ref, ext_ref, shift_ref, perm_ref, *, tm):
    t = pl.program_id(1)
    last = pl.num_programs(1) - 1
    halo = A_CONV - 1
    pad = 32
    h = _rms_mod(x_ref[0], g_ref[...], mod_ref[0, 1], mod_ref[0, 0]).astype(BF16)
    proj = _mm(h, w_ref[...])

    @pl.when(t == 0)
    def _():
        ext_ref[0:pad, :] = jnp.zeros((pad, A_CH), F32)

    ext_ref[pad:pad + tm, :] = proj[:, :A_CH] * jax.nn.sigmoid(proj[:, A_CH:2 * A_CH])
    for m in range(1, SUBLANES):
        shift_ref[m - 1] = ext_ref[pl.ds(m, tm + pad - SUBLANES), :]
    for c in range(tm // CONV_ROWS):
        r0 = c * CONV_ROWS
        y = _conv_ln_silu(ext_ref, shift_ref, pad - halo + r0, CONV_ROWS, cw_ref, cb_ref, lg_ref, lb_ref, A_CONV)
        a_ref[0, r0:r0 + CONV_ROWS, :] = y.astype(BF16)
    tail = ext_ref[pl.ds(tm + pad - halo, halo), :]
    ext_ref[pl.ds(pad - halo, halo), :] = tail

    @pl.when(t == last)
    def _():
        ast_ref[0] = tail

    q0_ref[0, 0] = proj[:, Q_OFF:Q_OFF + GROUP_CH].astype(BF16)
    kv0_ref[0, 0] = proj[:, KV_OFF:KV_OFF + 2 * GROUP_CH].astype(BF16)
    for g, (d, q_ref, kv_ref) in enumerate(zip(DILATIONS, (q0_ref, q1_ref, q2_ref), (kv0_ref, kv1_ref, kv2_ref))):
        if d == 1:
            continue
        for dst, off, tiles, slot in ((q_ref, Q_OFF + g * GROUP_CH, GROUP_CH // LANES, 0),
                                      (kv_ref, KV_OFF + 2 * g * GROUP_CH, 2 * GROUP_CH // LANES, GROUP_CH // LANES)):
            for c in range(tiles):
                tile = perm_ref.at[(g - 1) * PERM_TILES + slot + c]
                tile[...] = proj[:, off + c * LANES:off + (c + 1) * LANES]
                for r in range(d):
                    dst[0, r, :, c * LANES:(c + 1) * LANES] = tile[pl.ds(r, tm // d, stride=d), :].astype(BF16)

    st2_ref[0] = proj[:, KV_OFF + 4 * GROUP_CH:KV_OFF + 6 * GROUP_CH]

    @pl.when(t == last)
    def _():
        st0_ref[0] = proj[tm - WINDOWS[0]:, KV_OFF:KV_OFF + 2 * GROUP_CH]
        st1_ref[0] = proj[tm - WINDOWS[1]:, KV_OFF + 2 * GROUP_CH:KV_OFF + 4 * GROUP_CH]


def _in0_prompt(x, mod, g, w, cw, cb, lg, lb):
    nb, t, _ = x.shape
    tm = ROW_TILE
    assert t % tm == 0 and tm >= WINDOWS[1] and WINDOWS[2] == t
    row = lambda n, i: (n, i, 0)
    per_n = lambda n, i: (n, 0, 0)
    res = lambda n, i: (n, 0, i, 0)
    out_shape = [jax.ShapeDtypeStruct((nb, t, A_CH), BF16), jax.ShapeDtypeStruct((nb, A_CONV - 1, A_CH), F32)]
    out_specs = [pl.BlockSpec((1, tm, A_CH), row), pl.BlockSpec((1, A_CONV - 1, A_CH), per_n)]
    for width in (GROUP_CH, 2 * GROUP_CH):
        for d in DILATIONS:
            out_shape.append(jax.ShapeDtypeStruct((nb, d, t // d, width), BF16))
            out_specs.append(pl.BlockSpec((1, d, tm // d, width), res))
    out_shape += [jax.ShapeDtypeStruct((nb, WINDOWS[0], 2 * GROUP_CH), F32),
                  jax.ShapeDtypeStruct((nb, WINDOWS[1], 2 * GROUP_CH), F32),
                  jax.ShapeDtypeStruct((nb, t, 2 * GROUP_CH), F32)]
    out_specs += [pl.BlockSpec((1, WINDOWS[0], 2 * GROUP_CH), per_n),
                  pl.BlockSpec((1, WINDOWS[1], 2 * GROUP_CH), per_n),
                  pl.BlockSpec((1, tm, 2 * GROUP_CH), row)]
    return pl.pallas_call(
        functools.partial(_in0_prompt_kernel, tm=tm),
        grid=(nb, t // tm),
        in_specs=[pl.BlockSpec((1, tm, D_MODEL), row),
                  pl.BlockSpec((1, 6, 1, D_MODEL), lambda n, i: (n, 0, 0, 0)),
                  _const_spec((1, D_MODEL)), _const_spec((D_MODEL, E_IN)),
                  _const_spec((A_CONV, A_CH)), _const_spec((1, A_CH)),
                  _const_spec((1, A_CH)), _const_spec((1, A_CH))],
        out_specs=out_specs,
        out_shape=out_shape,
        scratch_shapes=[pltpu.VMEM((tm + 32, A_CH), F32), pltpu.VMEM((SUBLANES - 1, tm + 32 - SUBLANES, A_CH), F32),
                        pltpu.VMEM((2 * PERM_TILES, tm, LANES), F32)],
        compiler_params=_params(2),
        name="l0_in_prompt",
    )(x, mod, g, w, cw, cb, lg, lb)


def _head_masks(rows):
    col_head = lax.broadcasted_iota(jnp.int32, (rows, GROUP_CH), 1) // HEAD_DIM
    return [col_head == h for h in range(HPG)]


def _attn_prompt_kernel(q_ref, kv_ref, base_ref, o_ref, lse_ref, bias_ref, *, d, nblk):
    masks = _head_masks(Q_BLOCK)
    masks_bf = [jnp.where(m, HEAD_DIM ** -0.5, 0.0).astype(BF16) for m in masks]
    single = nblk == 1
    nkeys = Q_BLOCK if single else 2 * Q_BLOCK
    key_idx = lax.broadcasted_iota(jnp.int32, (HPG * Q_BLOCK, nkeys), 1)

    @pl.when(pl.program_id(0) == 0)
    def _():
        for h in range(HPG):
            row = jnp.broadcast_to(base_ref[h:h + 1, :], (Q_BLOCK, 2 * Q_BLOCK))
            bias_ref[h * Q_BLOCK:(h + 1) * Q_BLOCK, :] = pltpu.roll(row, 0, 1, stride=1, stride_axis=0)

    def block(j, carry):
        r = j // nblk
        i = j % nblk
        rows = pl.ds(pl.multiple_of(i * Q_BLOCK, Q_BLOCK), Q_BLOCK)
        q = q_ref[0, r, rows, :]
        qs = jnp.concatenate([q * mb for mb in masks_bf], axis=0)
        if single:
            kk = kv_ref[0, r, rows, :]
            bias = bias_ref[:, Q_BLOCK:]
        else:
            prev = pl.ds(pl.multiple_of(jnp.maximum(i - 1, 0) * Q_BLOCK, Q_BLOCK), Q_BLOCK)
            kk = jnp.concatenate([kv_ref[0, r, prev, :], kv_ref[0, r, rows, :]], axis=0)
            bias = bias_ref[...] + jnp.where(key_idx < jnp.where(i == 0, Q_BLOCK, 0), NEG, 0.0)
        s = _mm_nt(qs, kk[:, :GROUP_CH]) + bias
        m = jnp.max(s, axis=-1, keepdims=True)
        e = jnp.exp(s - m)
        l = jnp.sum(e, axis=-1, keepdims=True)
        o_all = _mm((e * (1.0 / l)).astype(BF16), kk[:, GROUP_CH:])
        lse_all = jnp.broadcast_to(m + jnp.log(l), (HPG * Q_BLOCK, GROUP_CH))
        o = o_all[:Q_BLOCK]
        lse = lse_all[:Q_BLOCK]
        for h in range(1, HPG):
            o = jnp.where(masks[h], o_all[h * Q_BLOCK:(h + 1) * Q_BLOCK], o)
            lse = jnp.where(masks[h], lse_all[h * Q_BLOCK:(h + 1) * Q_BLOCK], lse)
        o_ref[0, r, rows, :] = o
        lse_ref[0, r, rows, :] = lse
        return carry

    lax.fori_loop(0, d * nblk, block, 0, unroll=4)


def _attn_prompt(q, kv, base, d):
    nb, _, l, _ = q.shape
    blk = lambda n: (n, 0, 0, 0)
    out = jax.ShapeDtypeStruct((nb, d, l, GROUP_CH), F32)
    return pl.pallas_call(
        functools.partial(_attn_prompt_kernel, d=d, nblk=l // Q_BLOCK),
        grid=(nb,),
        in_specs=[pl.BlockSpec((1, d, l, GROUP_CH), blk), pl.BlockSpec((1, d, l, 2 * GROUP_CH), blk),
                  _const_spec((HPG, 2 * Q_BLOCK))],
        out_specs=[pl.BlockSpec((1, d, l, GROUP_CH), blk)] * 2,
        out_shape=[out, out],
        scratch_shapes=[pltpu.VMEM((HPG * Q_BLOCK, 2 * Q_BLOCK), F32)],
        compiler_params=_params(1),
        name="l0_attn_prompt_d%d" % d,
    )(q, kv, base)


def _mix_groups(outs, lses):
    m = jnp.maximum(jnp.maximum(lses[0], lses[1]), lses[2])
    es = [jnp.exp(l - m) for l in lses]
    inv = 1.0 / (es[0] + es[1] + es[2])
    return (es[0] * inv) * outs[0] + (es[1] * inv) * outs[1] + (es[2] * inv) * outs[2]


def _out0_prompt_kernel(x_ref, mod_ref, a_ref, o0_ref, o1_ref, o2_ref, l0_ref, l1_ref, l2_ref, w_ref,
                        y_ref, os1_ref, os2_ref, ls1_ref, ls2_ref, *, tm):
    tiles = GROUP_CH // LANES
    for d, src, dst in ((DILATIONS[1], o1_ref, os1_ref), (DILATIONS[1], l1_ref, ls1_ref),
                        (DILATIONS[2], o2_ref, os2_ref), (DILATIONS[2], l2_ref, ls2_ref)):
        for r in range(d):
            for c in range(tiles):
                dst[c, pl.ds(r, tm // d, stride=d), :] = src[0, r, :, c * LANES:(c + 1) * LANES]

    def ordered(ref):
        return jnp.concatenate([ref[c] for c in range(tiles)], axis=1)

    attn = _mix_groups([o0_ref[0, 0], ordered(os1_ref), ordered(os2_ref)],
                       [l0_ref[0, 0], ordered(ls1_ref), ordered(ls2_ref)])
    cat = jnp.concatenate([a_ref[0], attn.astype(BF16)], axis=1)
    y_ref[0] = x_ref[0] + mod_ref[0, 2] * _mm(cat, w_ref[...])


def _out0_prompt(x, mod, a, outs, lses, w):
    nb, t, _ = x.shape
    tm = ROW_TILE
    row = lambda n, i: (n, i, 0)
    res = lambda n, i: (n, 0, i, 0)
    grp_specs = [pl.BlockSpec((1, d, tm // d, GROUP_CH), res) for d in DILATIONS]
    return pl.pallas_call(
        functools.partial(_out0_prompt_kernel, tm=tm),
        grid=(nb, t // tm),
        in_specs=[pl.BlockSpec((1, tm, D_MODEL), row),
                  pl.BlockSpec((1, 6, 1, D_MODEL), lambda n, i: (n, 0, 0, 0)),
                  pl.BlockSpec((1, tm, A_CH), row)] + grp_specs + grp_specs
                 + [_const_spec((A_CH + GROUP_CH, D_MODEL))],
        out_specs=pl.BlockSpec((1, tm, D_MODEL), row),
        out_shape=jax.ShapeDtypeStruct((nb, t, D_MODEL), F32),
        scratch_shapes=[pltpu.VMEM((GROUP_CH // LANES, tm, LANES), F32)] * 4,
        compiler_params=_params(2),
        name="l0_out_prompt",
    )(x, mod, a, *outs, *lses, w)


def _ffn_kernel(x_ref, mod_ref, g_ref, wgu_ref, wd_ref, fg_ref, y_ref, *, final, chunk):
    x = x_ref[0]
    h = _rms_mod(x, g_ref[...], mod_ref[0, 4], mod_ref[0, 3]).astype(BF16)
    acc = jnp.zeros(x.shape, F32)
    for c0 in range(0, FFN_HIDDEN, chunk):
        c1 = min(c0 + chunk, FFN_HIDDEN)
        gate = _mm(h, wgu_ref[:, c0:c1])
        up = _mm(h, wgu_ref[:, FFN_HIDDEN + c0:FFN_HIDDEN + c1])
        acc = acc + _mm((_silu(gate) * up).astype(BF16), wd_ref[c0:c1, :])
    y = x + mod_ref[0, 5] * acc
    if final:
        y = y * lax.rsqrt(jnp.mean(y * y, axis=-1, keepdims=True) + EPS) * fg_ref[...]
    y_ref[0] = y


def _ffn(x, mod, g, wgu, wd, final_g, final):
    nb, t, _ = x.shape
    tm = min(ROW_TILE, t)
    rm = mod.shape[2]
    row = lambda n, i: (n, i, 0)
    mod_map = (lambda n, i: (n, 0, 0, 0)) if rm == 1 else (lambda n, i: (n, 0, i, 0))
    return pl.pallas_call(
        functools.partial(_ffn_kernel, final=final, chunk=512),
        grid=(nb, t // tm),
        in_specs=[pl.BlockSpec((1, tm, D_MODEL), row),
                  pl.BlockSpec((1, 6, min(rm, tm), D_MODEL), mod_map),
                  _const_spec((1, D_MODEL)),
                  _const_spec((D_MODEL, 2 * FFN_HIDDEN)), _const_spec((FFN_HIDDEN, D_MODEL)),
                  _const_spec((1, D_MODEL))],
        out_specs=pl.BlockSpec((1, tm, D_MODEL), row),
        out_shape=jax.ShapeDtypeStruct((nb, t, D_MODEL), F32),
        compiler_params=_params(2),
        name="ffn_final" if final else "ffn",
    )(x, mod, g, wgu, wd, final_g)


def _spatial_tril(ws_ref, grp):
    i = lax.broadcasted_iota(jnp.int32, (D_CHUNK, D_CHUNK), 0)
    j = lax.broadcasted_iota(jnp.int32, (D_CHUNK, D_CHUNK), 1)
    return jnp.where(j <= i, ws_ref[grp], 0.0).astype(BF16)


def _in1_prompt_kernel(x_ref, mod_ref, g_ref, w_ref, cw_ref, cb_ref, lg_ref, lb_ref, ws_ref, bs_ref, wo_ref,
                       y_ref, cst_ref, ext_ref, cat_ref, *, tm):
    t = pl.program_id(1)
    last = pl.num_programs(1) - 1
    halo = C_CONV - 1
    pad = 8
    x = x_ref[0]
    h = _rms_mod(x, g_ref[...], mod_ref[0, 1], mod_ref[0, 0]).astype(BF16)
    proj = _mm(h, w_ref[...])
    bg = proj[:, :C_CH]

    @pl.when(t == 0)
    def _():
        ext_ref[0:pad, :] = jnp.zeros((pad, C_CH), F32)

    ext_ref[pad:pad + tm, :] = proj[:, C_CH:2 * C_CH] * proj[:, 2 * C_CH:3 * C_CH]
    conv = jnp.broadcast_to(cb_ref[...], (tm, C_CH))
    for k in range(C_CONV):
        conv = conv + cw_ref[k:k + 1, :] * ext_ref[pl.ds(pad - halo + k, tm), :]
    cat_ref[:, :C_CH] = (bg * conv).astype(BF16)
    tail = ext_ref[pl.ds(tm + pad - halo, halo), :]
    ext_ref[pl.ds(pad - halo, halo), :] = tail

    @pl.when(t == last)
    def _():
        cst_ref[0] = tail

    uv = jax.nn.gelu(proj[:, 3 * C_CH:], approximate=True)
    u = uv[:, :D_CH]
    v = _layernorm(uv[:, D_CH:], lg_ref[...], lb_ref[...]).astype(BF16)
    for grp in range(D_GROUPS):
        ws = _spatial_tril(ws_ref, grp)
        cols = slice(grp * D_GROUP_CH, (grp + 1) * D_GROUP_CH)
        for c in range(tm // D_CHUNK):
            rows = slice(c * D_CHUNK, (c + 1) * D_CHUNK)
            mix = _mm(ws, v[rows, cols]) + bs_ref[:, cols]
            cat_ref[rows, C_CH + grp * D_GROUP_CH:C_CH + (grp + 1) * D_GROUP_CH] = (u[rows, cols] * mix).astype(BF16)
    y_ref[0] = x + mod_ref[0, 2] * _mm(cat_ref[...], wo_ref[...])


def _in1_prompt(x, mod, g, w, cw, cb, lg, lb, ws, bs_full, wo):
    nb, t, _ = x.shape
    tm = ROW_TILE
    row = lambda n, i: (n, i, 0)
    return pl.pallas_call(
        functools.partial(_in1_prompt_kernel, tm=tm),
        grid=(nb, t // tm),
        in_specs=[pl.BlockSpec((1, tm, D_MODEL), row),
                  pl.BlockSpec((1, 6, 1, D_MODEL), lambda n, i: (n, 0, 0, 0)),
                  _const_spec((1, D_MODEL)), _const_spec((D_MODEL, O_IN)),
                  _const_spec((C_CONV, C_CH)), _const_spec((1, C_CH)),
                  _const_spec((1, D_CH)), _const_spec((1, D_CH)),
                  _const_spec((D_GROUPS, D_CHUNK, D_CHUNK)), _const_spec((D_CHUNK, D_CH)),
                  _const_spec((C_CH + D_CH, D_MODEL))],
        out_specs=[pl.BlockSpec((1, tm, D_MODEL), row),
                   pl.BlockSpec((1, C_CONV - 1, C_CH), lambda n, i: (n, 0, 0))],
        out_shape=[jax.ShapeDtypeStruct((nb, t, D_MODEL), F32),
                   jax.ShapeDtypeStruct((nb, C_CONV - 1, C_CH), F32)],
        scratch_shapes=[pltpu.VMEM((tm + 8, C_CH), F32), pltpu.VMEM((tm, C_CH + D_CH), BF16)],
        compiler_params=_params(2),
        name="l1_mixer_prompt",
    )(x, mod, g, w, cw, cb, lg, lb, ws, bs_full, wo)


def _in0_sample_kernel(x_ref, mod_ref, g_ref, w_ref, abuf_ref, cw_ref, cb_ref, lg_ref, lb_ref,
                       a_ref, ast_ref, q_ref, kv_ref, ext_ref, *, nb, steps):
    rows = nb * steps
    hist = (A_CONV - 1) * nb
    h = _rms_mod(x_ref[0], g_ref[...], mod_ref[0, 1], mod_ref[0, 0]).astype(BF16)
    proj = _mm(h, w_ref[...])
    ext_ref[0:hist, :] = abuf_ref[...]
    ext_ref[hist:hist + rows, :] = proj[:, :A_CH] * jax.nn.sigmoid(proj[:, A_CH:2 * A_CH])
    acc = jnp.broadcast_to(cb_ref[...], (rows, A_CH))
    for k in range(A_CONV):
        acc = acc + cw_ref[k:k + 1, :] * ext_ref[k * nb:k * nb + rows, :]
    a_ref[...] = _silu(_layernorm(acc, lg_ref[...], lb_ref[...])).astype(BF16)
    ast_ref[...] = ext_ref[rows:rows + hist, :]
    q_ref[...] = proj[:, Q_OFF:KV_OFF]
    kv_ref[...] = proj[:, KV_OFF:]


def _in0_sample(x, mod, g, w, abuf_t, cw, cb, lg, lb, nb, steps):
    rows = nb * steps
    hist = (A_CONV - 1) * nb
    return pl.pallas_call(
        functools.partial(_in0_sample_kernel, nb=nb, steps=steps),
        grid=(1,),
        in_specs=[_const_spec((1, rows, D_MODEL)), _const_spec((1, 6, rows, D_MODEL)),
                  _const_spec((1, D_MODEL)), _const_spec((D_MODEL, E_IN)), _const_spec((hist, A_CH)),
                  _const_spec((A_CONV, A_CH)), _const_spec((1, A_CH)), _const_spec((1, A_CH)),
                  _const_spec((1, A_CH))],
        out_specs=[_const_spec((rows, A_CH)), _const_spec((hist, A_CH)),
                   _const_spec((rows, N_GROUPS * GROUP_CH)), _const_spec((rows, 2 * N_GROUPS * GROUP_CH))],
        out_shape=[jax.ShapeDtypeStruct((rows, A_CH), BF16), jax.ShapeDtypeStruct((hist, A_CH), F32),
                   jax.ShapeDtypeStruct((rows, N_GROUPS * GROUP_CH), F32),
                   jax.ShapeDtypeStruct((rows, 2 * N_GROUPS * GROUP_CH), F32)],
        scratch_shapes=[pltpu.VMEM((hist + rows, A_CH), F32)],
        compiler_params=_params(1),
        name="l0_in_sample",
    )(x, mod, g, w, abuf_t, cw, cb, lg, lb)


def _attn_sample_kernel(q_ref, new_ref, c0_ref, c1_ref, c2_ref, b0_ref, b1_ref, b2_ref,
                        attn_ref, s0_ref, s1_ref, s2_ref, *, steps, pad):
    rows = HPG * steps
    row_head = lax.broadcasted_iota(jnp.int32, (rows, GROUP_CH), 0) // steps
    col_head = lax.broadcasted_iota(jnp.int32, (rows, GROUP_CH), 1) // HEAD_DIM
    step_masks = _head_masks(steps)
    outs, lses = [], []
    for g, (c_ref, b_ref, s_ref) in enumerate(zip((c0_ref, c1_ref, c2_ref), (b0_ref, b1_ref, b2_ref),
                                                  (s0_ref, s1_ref, s2_ref))):
        w = c_ref.shape[-1]
        comb = jnp.concatenate([c_ref[0], new_ref[0, 2 * g * GROUP_CH:2 * (g + 1) * GROUP_CH, :]], axis=1)
        s_ref[0] = pltpu.roll(comb, w + pad - steps, axis=1)[:, :w]
        comb_bf = comb.astype(BF16)
        q = q_ref[0][:, g * GROUP_CH:(g + 1) * GROUP_CH]
        q_bd = jnp.where(row_head == col_head, jnp.concatenate([q] * HPG, axis=0), 0.0).astype(BF16)
        s = _mm(q_bd, comb_bf[:GROUP_CH]) * (HEAD_DIM ** -0.5) + b_ref[...]
        m = jnp.max(s, axis=-1, keepdims=True)
        e = jnp.exp(s - m)
        l = jnp.sum(e, axis=-1, keepdims=True)
        p = (e * (1.0 / l)).astype(BF16)
        o_all = _mm_nt(p, comb_bf[GROUP_CH:])
        lse_all = jnp.broadcast_to(m + jnp.log(l), (rows, GROUP_CH))
        o = jnp.zeros((steps, GROUP_CH), F32)
        lse = jnp.zeros((steps, GROUP_CH), F32)
        for h in range(HPG):
            o = jnp.where(step_masks[h], o_all[h * steps:(h + 1) * steps], o)
            lse = jnp.where(step_masks[h], lse_all[h * steps:(h + 1) * steps], lse)
        outs.append(o)
        lses.append(lse)
    attn_ref[0] = _mix_groups(outs, lses)


def _attn_sample(q, new_t, caches, biases, steps, pad):
    nb = q.shape[0]
    per_n = lambda n: (n, 0, 0)
    cache_specs = [pl.BlockSpec((1, 2 * GROUP_CH, w), per_n) for w in WINDOWS]
    return pl.pallas_call(
        functools.partial(_attn_sample_kernel, steps=steps, pad=pad),
        grid=(nb,),
        in_specs=[pl.BlockSpec((1, steps, N_GROUPS * GROUP_CH), per_n),
                  pl.BlockSpec((1, 2 * N_GROUPS * GROUP_CH, pad), per_n)] + cache_specs
                 + [_const_spec((HPG * steps, w + pad)) for w in WINDOWS],
        out_specs=[pl.BlockSpec((1, steps, GROUP_CH), per_n)] + cache_specs,
        out_shape=[jax.ShapeDtypeStruct((nb, steps, GROUP_CH), F32)]
                  + [jax.ShapeDtypeStruct((nb, 2 * GROUP_CH, w), F32) for w in WINDOWS],
        compiler_params=_params(1),
        name="l0_attn_sample",
    )(q, new_t, *caches, *biases)


def _out0_sample_kernel(x_ref, mod_ref, a_ref, attn_ref, w_ref, y_ref):
    cat = jnp.concatenate([a_ref[...], attn_ref[...].astype(BF16)], axis=1)
    y_ref[0] = x_ref[0] + mod_ref[0, 2] * _mm(cat, w_ref[...])


def _out0_sample(x, mod, a, attn, w):
    _, rows, _ = x.shape
    return pl.pallas_call(
        _out0_sample_kernel,
        grid=(1,),
        in_specs=[_const_spec((1, rows, D_MODEL)), _const_spec((1, 6, rows, D_MODEL)),
                  _const_spec((rows, A_CH)), _const_spec((rows, GROUP_CH)),
                  _const_spec((A_CH + GROUP_CH, D_MODEL))],
        out_specs=_const_spec((1, rows, D_MODEL)),
        out_shape=jax.ShapeDtypeStruct((1, rows, D_MODEL), F32),
        compiler_params=_params(1),
        name="l0_out_sample",
    )(x, mod, a, attn, w)


def _in1_sample_kernel(x_ref, mod_ref, g_ref, w_ref, cbuf_ref, cw_ref, cb_ref, lg_ref, lb_ref, coef_ref, bs_ref,
                       wo_ref, y_ref, cst_ref, dv_ref, ext_ref, *, nb, steps):
    rows = nb * steps
    hist = (C_CONV - 1) * nb
    x = x_ref[0]
    h = _rms_mod(x, g_ref[...], mod_ref[0, 1], mod_ref[0, 0]).astype(BF16)
    proj = _mm(h, w_ref[...])
    ext_ref[0:hist, :] = cbuf_ref[...]
    ext_ref[hist:hist + rows, :] = proj[:, C_CH:2 * C_CH] * proj[:, 2 * C_CH:3 * C_CH]
    conv = jnp.broadcast_to(cb_ref[...], (rows, C_CH))
    for k in range(C_CONV):
        conv = conv + cw_ref[k:k + 1, :] * ext_ref[k * nb:k * nb + rows, :]
    yc = proj[:, :C_CH] * conv
    cst_ref[...] = ext_ref[rows:rows + hist, :]
    uv = jax.nn.gelu(proj[:, 3 * C_CH:], approximate=True)
    v = _layernorm(uv[:, D_CH:], lg_ref[...], lb_ref[...])
    dv_ref[...] = v
    v_r = v.astype(BF16).astype(F32)
    coef = coef_ref[...].astype(BF16).astype(F32)
    mixes = []
    for i in range(steps):
        mix = jnp.broadcast_to(bs_ref[i:i + 1, :], (nb, D_CH))
        for j in range(i + 1):
            mix = mix + coef[i * steps + j:i * steps + j + 1, :] * v_r[j * nb:(j + 1) * nb, :]
        mixes.append(mix)
    yd = uv[:, :D_CH] * jnp.concatenate(mixes, axis=0)
    cat = jnp.concatenate([yc.astype(BF16), yd.astype(BF16)], axis=1)
    y_ref[0] = x + mod_ref[0, 2] * _mm(cat, wo_ref[...])


def _in1_sample(x, mod, g, w, cbuf_t, cw, cb, lg, lb, coef, bs_rows, wo, nb, steps):
    rows = nb * steps
    hist = (C_CONV - 1) * nb
    return pl.pallas_call(
        functools.partial(_in1_sample_kernel, nb=nb, steps=steps),
        grid=(1,),
        in_specs=[_const_spec((1, rows, D_MODEL)), _const_spec((1, 6, rows, D_MODEL)),
                  _const_spec((1, D_MODEL)), _const_spec((D_MODEL, O_IN)), _const_spec((hist, C_CH)),
                  _const_spec((C_CONV, C_CH)), _const_spec((1, C_CH)), _const_spec((1, D_CH)),
                  _const_spec((1, D_CH)), _const_spec((steps * steps, D_CH)), _const_spec((steps, D_CH)),
                  _const_spec((C_CH + D_CH, D_MODEL))],
        out_specs=[_const_spec((1, rows, D_MODEL)), _const_spec((hist, C_CH)), _const_spec((rows, D_CH))],
        out_shape=[jax.ShapeDtypeStruct((1, rows, D_MODEL), F32), jax.ShapeDtypeStruct((hist, C_CH), F32),
                   jax.ShapeDtypeStruct((rows, D_CH), F32)],
        scratch_shapes=[pltpu.VMEM((hist + rows, C_CH), F32)],
        compiler_params=_params(1),
        name="l1_mixer_sample",
    )(x, mod, g, w, cbuf_t, cw, cb, lg, lb, coef, bs_rows, wo)


def _t5_bucket(dist):
    n = np.maximum(np.asarray(dist, dtype=np.int64), 0)
    max_exact = N_BUCKETS // 2
    large = max_exact + (np.log(np.maximum(n, 1) / max_exact) / np.log(MAX_DISTANCE / max_exact)
                         * (N_BUCKETS - max_exact)).astype(np.int32)
    return np.where(n < max_exact, n, np.minimum(large, N_BUCKETS - 1)).astype(np.int32)


def _prompt_bias_base(table, grp):
    rel = Q_BLOCK - np.arange(2 * Q_BLOCK)
    band = (rel >= 0) & (rel <= SUB_WINDOW)
    tab = table[:, grp * HPG:(grp + 1) * HPG]
    bias = tab[_t5_bucket(np.clip(rel, 0, None) * DILATIONS[grp])].T
    return jnp.where(band[None], bias, NEG).astype(F32)


def _sample_bias(table, grp, steps, pad):
    w, d = WINDOWS[grp], DILATIONS[grp]
    delta = w + steps - 1 - np.arange(w + pad + steps - 1)
    valid = (delta >= 0) & (delta % d == 0) & (delta // d <= SUB_WINDOW)
    tab = table[:, grp * HPG:(grp + 1) * HPG]
    vec = jnp.where(valid[None], tab[_t5_bucket(np.clip(delta, 0, None))].T, NEG).astype(F32)
    per_step = [vec[:, steps - 1 - s:steps - 1 - s + w + pad] for s in range(steps)]
    return jnp.stack(per_step, axis=1).reshape(HPG * steps, w + pad)


def _step_major(x):
    return jnp.swapaxes(x, 0, 1).reshape((x.shape[0] * x.shape[1],) + x.shape[2:])


def _batch_major(x, nb, steps):
    return jnp.swapaxes(x.reshape((steps, nb) + x.shape[1:]), 0, 1)


def kernel(x_prompt, x_sample, c_prompt, c_sample, state_a_conv, cache_b_kv0, cache_b_kv1, cache_b_kv2, state_c_conv, rel_bias_table, ada_w, ada_b, norm_mix_g, norm_ffn_g, ffn_w_gate_up, ffn_w_down, final_norm_g, e_w_in, a_conv_w, a_conv_b, a_ln_g, a_ln_b, e_w_out, o_w_in, c_conv_w, c_conv_b, d_ln_g, d_ln_b, d_spatial_w, d_spatial_b, o_w_out):
    nb_p = x_prompt.shape[0]
    nb_s, steps, _ = x_sample.shape
    pad = 128

    w_e = e_w_in[0]
    qkv = w_e[:, Q_OFF:].reshape(D_MODEL, 3, N_GROUPS, GROUP_CH)
    kv_cols = jnp.stack([qkv[:, 1], qkv[:, 2]], axis=2).reshape(D_MODEL, 2 * N_GROUPS * GROUP_CH)
    w_in0 = jnp.concatenate([w_e[:, :Q_OFF], qkv[:, 0].reshape(D_MODEL, N_GROUPS * GROUP_CH), kv_cols],
                            axis=1).astype(BF16)
    w_out0 = e_w_out[0].astype(BF16)
    w_in1 = o_w_in[0].astype(BF16)
    w_out1 = o_w_out[0].astype(BF16)
    w_gu = ffn_w_gate_up.astype(BF16)
    w_dn = ffn_w_down.astype(BF16)
    fin_g = final_norm_g.reshape(1, D_MODEL)

    mod = _ada(jnp.concatenate([c_prompt, c_sample], axis=0), ada_w, ada_b)
    mod = mod.reshape(mod.shape[0], nb_p + nb_s, 6, D_MODEL)
    mod_p = [mod[l, :nb_p].reshape(nb_p, 6, 1, D_MODEL) for l in range(2)]
    mod_s = [jnp.swapaxes(jnp.tile(mod[l, nb_p:], (steps, 1, 1)), 0, 1)[None] for l in range(2)]

    (a, a_st_p, q0, q1, q2, kv0, kv1, kv2, st0, st1, st2) = _in0_prompt(
        x_prompt, mod_p[0], norm_mix_g[0:1], w_in0, a_conv_w[0], a_conv_b, a_ln_g, a_ln_b)
    outs, lses = [], []
    for grp, (q, kv) in enumerate(((q0, kv0), (q1, kv1), (q2, kv2))):
        o, lse = _attn_prompt(q, kv, _prompt_bias_base(rel_bias_table, grp), DILATIONS[grp])
        outs.append(o)
        lses.append(lse)
    xp = _out0_prompt(x_prompt, mod_p[0], a, outs, lses, w_out0)
    xp = _ffn(xp, mod_p[0], norm_ffn_g[0:1], w_gu[0], w_dn[0], fin_g, False)
    bs_full = jnp.repeat(d_spatial_b[0].T, D_GROUP_CH, axis=1)
    xp, c_st_p = _in1_prompt(xp, mod_p[1], norm_mix_g[1:2], w_in1, c_conv_w[0], c_conv_b, d_ln_g, d_ln_b,
                             d_spatial_w[0], bs_full, w_out1)
    y_prompt = _ffn(xp, mod_p[1], norm_ffn_g[1:2], w_gu[1], w_dn[1], fin_g, True)

    xs = _step_major(x_sample)[None]
    abuf_t = _step_major(state_a_conv[0])
    a_s, a_st_s, q_s, kv_s = _in0_sample(xs, mod_s[0], norm_mix_g[0:1], w_in0, abuf_t, a_conv_w[0], a_conv_b,
                                         a_ln_g, a_ln_b, nb_s, steps)
    q_b = _batch_major(q_s, nb_s, steps)
    new_t = jnp.transpose(kv_s.reshape(steps, nb_s, -1), (1, 2, 0))
    new_t = jnp.pad(new_t, ((0, 0), (0, 0), (0, pad - steps)))
    caches = [jnp.transpose(c[0], (0, 2, 3, 4, 1)).reshape(nb_s, 2 * GROUP_CH, c.shape[2])
              for c in (cache_b_kv0, cache_b_kv1, cache_b_kv2)]
    biases = [_sample_bias(rel_bias_table, grp, steps, pad) for grp in range(N_GROUPS)]
    attn_s, s0, s1, s2 = _attn_sample(q_b, new_t, caches, biases, steps, pad)
    xs = _out0_sample(xs, mod_s[0], a_s, _step_major(attn_s), w_out0)
    xs = _ffn(xs, mod_s[0], norm_ffn_g[0:1], w_gu[0], w_dn[0], fin_g, False)
    cbuf_t = _step_major(state_c_conv[0])
    coef = jnp.repeat(jnp.transpose(d_spatial_w[0][:, :steps, :steps], (1, 2, 0)).reshape(steps * steps, D_GROUPS),
                      D_GROUP_CH, axis=1)
    xs, c_st_s, dv_s = _in1_sample(xs, mod_s[1], norm_mix_g[1:2], w_in1, cbuf_t, c_conv_w[0], c_conv_b, d_ln_g,
                                   d_ln_b, coef, bs_full[:steps], w_out1, nb_s, steps)
    y_sample = _ffn(xs, mod_s[1], norm_ffn_g[1:2], w_gu[1], w_dn[1], fin_g, True)

    def kv_state_prompt(st):
        return st.reshape(1, nb_p, st.shape[1], 2, HPG, HEAD_DIM)

    def kv_state_sample(st):
        return jnp.transpose(st.reshape(nb_s, 2, HPG, HEAD_DIM, st.shape[-1]), (0, 4, 1, 2, 3))[None]

    return (y_prompt, _batch_major(y_sample[0], nb_s, steps),
            a_st_p[None], _batch_major(a_st_s, nb_s, A_CONV - 1)[None],
            kv_state_prompt(st0), kv_state_sample(s0), kv_state_prompt(st1), kv_state_sample(s1),
            kv_state_prompt(st2), kv_state_sample(s2),
            c_st_p[None], _batch_major(c_st_s, nb_s, C_CONV - 1)[None],
            _batch_major(dv_s, nb_s, steps)[None])
```

```python
import functools

import numpy as np
import jax
import jax.numpy as jnp
from jax import lax
from jax.experimental import pallas as pl
from jax.experimental.pallas import tpu as pltpu

F32 = jnp.float32
BF16 = jnp.bfloat16

D_MODEL = 1024
EPS = 1e-6
A_CH = 256
A_CONV = 31
HEAD_DIM = 64
HPG = 4
GROUP_CH = HPG * HEAD_DIM
WINDOWS = (128, 512, 2048)
DILATIONS = (1, 4, 16)
N_GROUPS = 3
SUB_WINDOW = 128
Q_BLOCK = 128
N_BUCKETS = 32
MAX_DISTANCE = 2048
C_CH = 512
C_CONV = 3
D_CH = 512
D_GROUPS = 4
D_GROUP_CH = D_CH // D_GROUPS
D_CHUNK = 128
FFN_HIDDEN = 2816
Q_OFF = 2 * A_CH
K_OFF = Q_OFF + N_GROUPS * GROUP_CH
V_OFF = K_OFF + N_GROUPS * GROUP_CH
E_IN = V_OFF + N_GROUPS * GROUP_CH
O_IN = 3 * C_CH + 2 * D_CH
LANES = 128
SUBLANES = 8
GROUP_TILES = GROUP_CH // LANES
NEG = -1e30
ROW_TILE = 512
CONV_ROWS = 64
MIX_ROWS = 256
VMEM_LIMIT = 56 * 1024 * 1024


def _params(n_axes):
    return pltpu.CompilerParams(dimension_semantics=("arbitrary",) * n_axes, vmem_limit_bytes=VMEM_LIMIT)


def _const_spec(shape):
    return pl.BlockSpec(shape, lambda *_: (0,) * len(shape), pipeline_mode=pl.Buffered(1))


def _layer_spec(shape, layer):
    return pl.BlockSpec((1,) + shape, lambda *_: (layer,) + (0,) * len(shape), pipeline_mode=pl.Buffered(1))


def _rms_mod(x, g, scale, shift):
    return (x * lax.rsqrt(jnp.mean(x * x, axis=-1, keepdims=True) + EPS)) * (g * (1.0 + scale)) + shift


def _layernorm(x, g, b):
    mu = jnp.mean(x, axis=-1, keepdims=True)
    xc = x - mu
    var = jnp.mean(xc * xc, axis=-1, keepdims=True)
    return xc * lax.rsqrt(var + EPS) * g + b


def _silu(x):
    return x * jax.nn.sigmoid(x)


def _mm(a, b):
    return jnp.dot(a, b, preferred_element_type=F32)


def _mm_nt(a, b):
    return lax.dot_general(a, b, (((1,), (1,)), ((), ())), preferred_element_type=F32)


def _group_cols(proj, off, g):
    return proj[:, off + g * GROUP_CH:off + (g + 1) * GROUP_CH]


def _ada_kernel(c_ref, w_ref, b_ref, o_ref):
    cs = _silu(c_ref[...]).astype(BF16)
    o_ref[0] = _mm(cs, w_ref[0].astype(BF16)) + b_ref[0]


def _ada(c_all, ada_w, ada_b):
    depth, _, width = ada_w.shape
    nb = c_all.shape[0]
    tn = 1536
    return pl.pallas_call(
        _ada_kernel,
        grid=(depth, width // tn),
        in_specs=[_const_spec((nb, D_MODEL)),
                  pl.BlockSpec((1, D_MODEL, tn), lambda l, j: (l, 0, j)),
                  pl.BlockSpec((1, 1, tn), lambda l, j: (l, 0, j))],
        out_specs=pl.BlockSpec((1, nb, tn), lambda l, j: (l, 0, j)),
        out_shape=jax.ShapeDtypeStruct((depth, nb, width), F32),
        compiler_params=_params(2),
        name="ada_mod",
    )(c_all, ada_w, ada_b.reshape(depth, 1, width))


def _conv_ln_silu(ext_ref, shift_ref, base, rows, cw_ref, cb_ref, lg_ref, lb_ref, taps):
    acc = jnp.broadcast_to(cb_ref[...], (rows, cb_ref.shape[-1]))
    for k in range(taps):
        m = (base + k) % SUBLANES
        src = ext_ref if m == 0 else shift_ref.at[m - 1]
        acc = acc + cw_ref[k:k + 1, :] * src[pl.ds(base + k - m, rows), :]
    return _silu(_layernorm(acc, lg_ref[...], lb_ref[...]))


def _in0_prompt_kernel(x_ref, mod_ref, g_ref, w_ref, cw_ref, cb_ref, lg_ref, lb_ref,
                       a_ref, ast_ref, q0_ref, q1_ref, q2_ref, kv0_ref, kv1_ref, kv2_ref,
                       st0_ref, st1_ref, st2_ref, ext_ref, shift_ref, perm_ref, *, tm):
    t = pl.program_id(1)
    last = pl.num_programs(1) - 1
    halo = A_CONV - 1
    pad = 32
    h = _rms_mod(x_ref[0], g_ref[...], mod_ref[0, 1], mod_ref[0, 0]).astype(BF16)
    proj = _mm(h, w_ref[...])

    @pl.when(t == 0)
    def _():
        ext_ref[0:pad, :] = jnp.zeros((pad, A_CH), F32)

    ext_ref[pad:pad + tm, :] = proj[:, :A_CH] * jax.nn.sigmoid(proj[:, A_CH:2 * A_CH])
    for m in range(1, SUBLANES):
        shift_ref[m - 1] = ext_ref[pl.ds(m, tm + pad - SUBLANES), :]
    for c in range(tm // CONV_ROWS):
        r0 = c * CONV_ROWS
        y = _conv_ln_silu(ext_ref, shift_ref, pad - halo + r0, CONV_ROWS, cw_ref, cb_ref, lg_ref, lb_ref, A_CONV)
        a_ref[0, r0:r0 + CONV_ROWS, :] = y.astype(BF16)
    tail = ext_ref[pl.ds(tm + pad - halo, halo), :]
    ext_ref[pl.ds(pad - halo, halo), :] = tail

    @pl.when(t == last)
    def _():
        ast_ref[0] = tail

    q0_ref[0, 0] = _group_cols(proj, Q_OFF, 0).astype(BF16)
    kv0_ref[0, 0, :, :GROUP_CH] = _group_cols(proj, K_OFF, 0).astype(BF16)
    kv0_ref[0, 0, :, GROUP_CH:] = _group_cols(proj, V_OFF, 0).astype(BF16)
    slot = 0
    for g, (d, q_ref, kv_ref) in enumerate(zip(DILATIONS, (q0_ref, q1_ref, q2_ref), (kv0_ref, kv1_ref, kv2_ref))):
        if d == 1:
            continue
        for dst, off, col0 in ((q_ref, Q_OFF, 0), (kv_ref, K_OFF, 0), (kv_ref, V_OFF, GROUP_CH)):
            for c in range(GROUP_TILES):
                tile = perm_ref.at[slot]
                slot += 1
                src = off + g * GROUP_CH + c * LANES
                tile[...] = proj[:, src:src + LANES]
                for r in range(d):
                    dst[0, r, :, col0 + c * LANES:col0 + (c + 1) * LANES] = (
                        tile[pl.ds(r, tm // d, stride=d), :].astype(BF16))

    st2_ref[0, :, :GROUP_CH] = _group_cols(proj, K_OFF, 2)
    st2_ref[0, :, GROUP_CH:] = _group_cols(proj, V_OFF, 2)

    @pl.when(t == last)
    def _():
        for g, st_ref in ((0, st0_ref), (1, st1_ref)):
            st_ref[0, :, :GROUP_CH] = _group_cols(proj, K_OFF, g)[tm - WINDOWS[g]:]
            st_ref[0, :, GROUP_CH:] = _group_cols(proj, V_OFF, g)[tm - WINDOWS[g]:]


def _in0_prompt(x, mod, g, w, cw, cb, lg, lb):
    nb, t, _ = x.shape
    tm = ROW_TILE
    assert t % tm == 0 and tm >= WINDOWS[1] and WINDOWS[2] == t
    row = lambda n, i: (n, i, 0)
    per_n = lambda n, i: (n, 0, 0)
    res = lambda n, i: (n, 0, i, 0)
    out_shape = [jax.ShapeDtypeStruct((nb, t, A_CH), BF16), jax.ShapeDtypeStruct((nb, A_CONV - 1, A_CH), F32)]
    out_specs = [pl.BlockSpec((1, tm, A_CH), row), pl.BlockSpec((1, A_CONV - 1, A_CH), per_n)]
    for width in (GROUP_CH, 2 * GROUP_CH):
        for d in DILATIONS:
            out_shape.append(jax.ShapeDtypeStruct((nb, d, t // d, width), BF16))
            out_specs.append(pl.BlockSpec((1, d, tm // d, width), res))
    out_shape += [jax.ShapeDtypeStruct((nb, WINDOWS[0], 2 * GROUP_CH), F32),
                  jax.ShapeDtypeStruct((nb, WINDOWS[1], 2 * GROUP_CH), F32),
                  jax.ShapeDtypeStruct((nb, t, 2 * GROUP_CH), F32)]
    out_specs += [pl.BlockSpec((1, WINDOWS[0], 2 * GROUP_CH), per_n),
                  pl.BlockSpec((1, WINDOWS[1], 2 * GROUP_CH), per_n),
                  pl.BlockSpec((1, tm, 2 * GROUP_CH), row)]
    return pl.pallas_call(
        functools.partial(_in0_prompt_kernel, tm=tm),
        grid=(nb, t // tm),
        in_specs=[pl.BlockSpec((1, tm, D_MODEL), row),
                  pl.BlockSpec((1, 6, 1, D_MODEL), lambda n, i: (n, 0, 0, 0)),
                  _const_spec((1, D_MODEL)), _const_spec((D_MODEL, E_IN)),
                  _const_spec((A_CONV, A_CH)), _const_spec((1, A_CH)),
                  _const_spec((1, A_CH)), _const_spec((1, A_CH))],
        out_specs=out_specs,
        out_shape=out_shape,
        scratch_shapes=[pltpu.VMEM((tm + 32, A_CH), F32), pltpu.VMEM((SUBLANES - 1, tm + 32 - SUBLANES, A_CH), F32),
                        pltpu.VMEM((2 * 3 * GROUP_TILES, tm, LANES), F32)],
        compiler_params=_params(2),
        name="l0_in_prompt",
    )(x, mod, g, w, cw, cb, lg, lb)


def _head_masks(rows):
    col_head = lax.broadcasted_iota(jnp.int32, (rows, GROUP_CH), 1) // HEAD_DIM
    return [col_head == h for h in range(HPG)]


def _mix_groups(outs, lses):
    m = jnp.maximum(jnp.maximum(lses[0], lses[1]), lses[2])
    es = [jnp.exp(l - m) for l in lses]
    inv = 1.0 / (es[0] + es[1] + es[2])
    return (es[0] * inv) * outs[0] + (es[1] * inv) * outs[1] + (es[2] * inv) * outs[2]


def _attn_prompt_kernel(q0_ref, kv0_ref, q1_ref, kv1_ref, q2_ref, kv2_ref, base_ref, attn_ref,
                        bias_ref, o_ref, lse_ref, *, t):
    masks = _head_masks(Q_BLOCK)
    masks_bf = [jnp.where(m, HEAD_DIM ** -0.5, 0.0).astype(BF16) for m in masks]
    key_idx = lax.broadcasted_iota(jnp.int32, (HPG * Q_BLOCK, 2 * Q_BLOCK), 1)

    @pl.when(pl.program_id(0) == 0)
    def _():
        for g in range(N_GROUPS):
            for h in range(HPG):
                row = jnp.broadcast_to(base_ref[g, h:h + 1, :], (Q_BLOCK, 2 * Q_BLOCK))
                bias_ref[g, h * Q_BLOCK:(h + 1) * Q_BLOCK, :] = pltpu.roll(row, 0, 1, stride=1, stride_axis=0)

    for g, (d, q_ref, kv_ref) in enumerate(zip(DILATIONS, (q0_ref, q1_ref, q2_ref), (kv0_ref, kv1_ref, kv2_ref))):
        nblk = t // d // Q_BLOCK
        single = nblk == 1

        def block(j, carry, g=g, d=d, q_ref=q_ref, kv_ref=kv_ref, nblk=nblk, single=single):
            r = j // nblk
            i = j % nblk
            rows = pl.ds(pl.multiple_of(i * Q_BLOCK, Q_BLOCK), Q_BLOCK)
            q = q_ref[0, r, rows, :]
            qs = jnp.concatenate([q * mb for mb in masks_bf], axis=0)
            if single:
                kk = kv_ref[0, r, rows, :]
                bias = bias_ref[g, :, Q_BLOCK:]
            else:
                prev = pl.ds(pl.multiple_of(jnp.maximum(i - 1, 0) * Q_BLOCK, Q_BLOCK), Q_BLOCK)
                kk = jnp.concatenate([kv_ref[0, r, prev, :], kv_ref[0, r, rows, :]], axis=0)
                bias = bias_ref[g] + jnp.where(key_idx < jnp.where(i == 0, Q_BLOCK, 0), NEG, 0.0)
            s = _mm_nt(qs, kk[:, :GROUP_CH]) + bias
            m = jnp.max(s, axis=-1, keepdims=True)
            e = jnp.exp(s - m)
            l = jnp.sum(e, axis=-1, keepdims=True)
            o_all = _mm((e * (1.0 / l)).astype(BF16), kk[:, GROUP_CH:])
            lse_all = jnp.broadcast_to(m + jnp.log(l), (HPG * Q_BLOCK, GROUP_CH))
            o = o_all[:Q_BLOCK]
            lse = lse_all[:Q_BLOCK]
            for h in range(1, HPG):
                o = jnp.where(masks[h], o_all[h * Q_BLOCK:(h + 1) * Q_BLOCK], o)
                lse = jnp.where(masks[h], lse_all[h * Q_BLOCK:(h + 1) * Q_BLOCK], lse)
            tokens = rows if d == 1 else pl.ds(r + i * (Q_BLOCK * d), Q_BLOCK, stride=d)
            for c in range(GROUP_TILES):
                o_ref[g, c, tokens, :] = o[:, c * LANES:(c + 1) * LANES]
                lse_ref[g, c, tokens, :] = lse[:, c * LANES:(c + 1) * LANES]
            return carry

        lax.fori_loop(0, d * nblk, block, 0, unroll=4)

    for r0 in range(0, t, MIX_ROWS):
        for c in range(GROUP_TILES):
            outs = [o_ref[g, c, r0:r0 + MIX_ROWS, :] for g in range(N_GROUPS)]
            lses = [lse_ref[g, c, r0:r0 + MIX_ROWS, :] for g in range(N_GROUPS)]
            attn_ref[0, r0:r0 + MIX_ROWS, c * LANES:(c + 1) * LANES] = _mix_groups(outs, lses).astype(BF16)


def _attn_prompt(qs, kvs, base):
    nb, _, t, _ = qs[0].shape
    blk = lambda n: (n, 0, 0, 0)
    in_specs = []
    for d in DILATIONS:
        in_specs += [pl.BlockSpec((1, d, t // d, GROUP_CH), blk), pl.BlockSpec((1, d, t // d, 2 * GROUP_CH), blk)]
    operands = [x for pair in zip(qs, kvs) for x in pair]
    return pl.pallas_call(
        functools.partial(_attn_prompt_kernel, t=t),
        grid=(nb,),
        in_specs=in_specs + [_const_spec((N_GROUPS, HPG, 2 * Q_BLOCK))],
        out_specs=pl.BlockSpec((1, t, GROUP_CH), lambda n: (n, 0, 0)),
        out_shape=jax.ShapeDtypeStruct((nb, t, GROUP_CH), BF16),
        scratch_shapes=[pltpu.VMEM((N_GROUPS, HPG * Q_BLOCK, 2 * Q_BLOCK), F32),
                        pltpu.VMEM((N_GROUPS, GROUP_TILES, t, LANES), F32),
                        pltpu.VMEM((N_GROUPS, GROUP_TILES, t, LANES), F32)],
        compiler_params=_params(1),
        name="l0_attn_prompt",
    )(*operands, base)


def _ffn_kernel(x_ref, mod_ref, g_ref, wgu_ref, wd_ref, fg_ref, *rest, final, chunk, mixer_out):
    x = x_ref[0]
    if mixer_out:
        a_ref, attn_ref, wo_ref, y_ref = rest
        x = x + mod_ref[0, 2] * _mm(jnp.concatenate([a_ref[0], attn_ref[0]], axis=1), wo_ref[...])
    else:
        (y_ref,) = rest
    h = _rms_mod(x, g_ref[...], mod_ref[0, 4], mod_ref[0, 3]).astype(BF16)
    acc = jnp.zeros(x.shape, F32)
    for c0 in range(0, FFN_HIDDEN, chunk):
        c1 = min(c0 + chunk, FFN_HIDDEN)
        gate = _mm(h, wgu_ref[0, :, c0:c1])
        up = _mm(h, wgu_ref[0, :, FFN_HIDDEN + c0:FFN_HIDDEN + c1])
        acc = acc + _mm((_silu(gate) * up).astype(BF16), wd_ref[0, c0:c1, :])
    y = x + mod_ref[0, 5] * acc
    if final:
        y = y * lax.rsqrt(jnp.mean(y * y, axis=-1, keepdims=True) + EPS) * fg_ref[...]
    y_ref[0] = y


def _ffn(x, mod, g, wgu, wd, final_g, layer, final, mixer_out=None):
    nb, t, _ = x.shape
    tm = min(ROW_TILE, t)
    rm = mod.shape[2]
    row = lambda n, i: (n, i, 0)
    mod_map = (lambda n, i: (n, 0, 0, 0)) if rm == 1 else (lambda n, i: (n, 0, i, 0))
    in_specs = [pl.BlockSpec((1, tm, D_MODEL), row),
                pl.BlockSpec((1, 6, min(rm, tm), D_MODEL), mod_map),
                _const_spec((1, D_MODEL)),
                _layer_spec((D_MODEL, 2 * FFN_HIDDEN), layer), _layer_spec((FFN_HIDDEN, D_MODEL), layer),
                _const_spec((1, D_MODEL))]
    operands = [x, mod, g, wgu, wd, final_g]
    if mixer_out is not None:
        a, attn, wo = mixer_out
        in_specs += [pl.BlockSpec((1, tm, A_CH), row), pl.BlockSpec((1, tm, GROUP_CH), row),
                     _const_spec((A_CH + GROUP_CH, D_MODEL))]
        operands += [a, attn, wo]
    return pl.pallas_call(
        functools.partial(_ffn_kernel, final=final, chunk=512, mixer_out=mixer_out is not None),
        grid=(nb, t // tm),
        in_specs=in_specs,
        out_specs=pl.BlockSpec((1, tm, D_MODEL), row),
        out_shape=jax.ShapeDtypeStruct((nb, t, D_MODEL), F32),
        compiler_params=_params(2),
        name="ffn_final" if final else "l0_out_ffn",
    )(*operands)


def _spatial_tril(ws_ref, grp):
    i = lax.broadcasted_iota(jnp.int32, (D_CHUNK, D_CHUNK), 0)
    j = lax.broadcasted_iota(jnp.int32, (D_CHUNK, D_CHUNK), 1)
    return jnp.where(j <= i, ws_ref[grp], 0.0).astype(BF16)


def _in1_prompt_kernel(x_ref, mod_ref, g_ref, w_ref, cw_ref, cb_ref, lg_ref, lb_ref, ws_ref, bs_ref, wo_ref,
                       y_ref, cst_ref, ext_ref, cat_ref, *, tm):
    t = pl.program_id(1)
    last = pl.num_programs(1) - 1
    halo = C_CONV - 1
    pad = 8
    x = x_ref[0]
    h = _rms_mod(x, g_ref[...], mod_ref[0, 1], mod_ref[0, 0]).astype(BF16)
    proj = _mm(h, w_ref[...])
    bg = proj[:, :C_CH]

    @pl.when(t == 0)
    def _():
        ext_ref[0:pad, :] = jnp.zeros((pad, C_CH), F32)

    ext_ref[pad:pad + tm, :] = proj[:, C_CH:2 * C_CH] * proj[:, 2 * C_CH:3 * C_CH]
    conv = jnp.broadcast_to(cb_ref[...], (tm, C_CH))
    for k in range(C_CONV):
        conv = conv + cw_ref[k:k + 1, :] * ext_ref[pl.ds(pad - halo + k, tm), :]
    cat_ref[:, :C_CH] = (bg * conv).astype(BF16)
    tail = ext_ref[pl.ds(tm + pad - halo, halo), :]
    ext_ref[pl.ds(pad - halo, halo), :] = tail

    @pl.when(t == last)
    def _():
        cst_ref[0] = tail

    uv = jax.nn.gelu(proj[:, 3 * C_CH:], approximate=True)
    u = uv[:, :D_CH]
    v = _layernorm(uv[:, D_CH:], lg_ref[...], lb_ref[...]).astype(BF16)
    for grp in range(D_GROUPS):
        ws = _spatial_tril(ws_ref, grp)
        cols = slice(grp * D_GROUP_CH, (grp + 1) * D_GROUP_CH)
        for c in range(tm // D_CHUNK):
            rows = slice(c * D_CHUNK, (c + 1) * D_CHUNK)
            mix = _mm(ws, v[rows, cols]) + bs_ref[:, cols]
            cat_ref[rows, C_CH + grp * D_GROUP_CH:C_CH + (grp + 1) * D_GROUP_CH] = (u[rows, cols] * mix).astype(BF16)
    y_ref[0] = x + mod_ref[0, 2] * _mm(cat_ref[...], wo_ref[...])


def _in1_prompt(x, mod, g, w, cw, cb, lg, lb, ws, bs_full, wo):
    nb, t, _ = x.shape
    tm = ROW_TILE
    row = lambda n, i: (n, i, 0)
    return pl.pallas_call(
        functools.partial(_in1_prompt_kernel, tm=tm),
        grid=(nb, t // tm),
        in_specs=[pl.BlockSpec((1, tm, D_MODEL), row),
                  pl.BlockSpec((1, 6, 1, D_MODEL), lambda n, i: (n, 0, 0, 0)),
                  _const_spec((1, D_MODEL)), _const_spec((D_MODEL, O_IN)),
                  _const_spec((C_CONV, C_CH)), _const_spec((1, C_CH)),
                  _const_spec((1, D_CH)), _const_spec((1, D_CH)),
                  _const_spec((D_GROUPS, D_CHUNK, D_CHUNK)), _const_spec((D_CHUNK, D_CH)),
                  _const_spec((C_CH + D_CH, D_MODEL))],
        out_specs=[pl.BlockSpec((1, tm, D_MODEL), row),
                   pl.BlockSpec((1, C_CONV - 1, C_CH), lambda n, i: (n, 0, 0))],
        out_shape=[jax.ShapeDtypeStruct((nb, t, D_MODEL), F32),
                   jax.ShapeDtypeStruct((nb, C_CONV - 1, C_CH), F32)],
        scratch_shapes=[pltpu.VMEM((tm + 8, C_CH), F32), pltpu.VMEM((tm, C_CH + D_CH), BF16)],
        compiler_params=_params(2),
        name="l1_mixer_prompt",
    )(x, mod, g, w, cw, cb, lg, lb, ws, bs_full, wo)


def _in0_sample_kernel(x_ref, mod_ref, g_ref, w_ref, abuf_ref, cw_ref, cb_ref, lg_ref, lb_ref,
                       a_ref, ast_ref, q_ref, kv_ref, ext_ref, *, nb, steps):
    rows = nb * steps
    hist = (A_CONV - 1) * nb
    h = _rms_mod(x_ref[0], g_ref[...], mod_ref[0, 1], mod_ref[0, 0]).astype(BF16)
    proj = _mm(h, w_ref[...])
    ext_ref[0:hist, :] = abuf_ref[...]
    ext_ref[hist:hist + rows, :] = proj[:, :A_CH] * jax.nn.sigmoid(proj[:, A_CH:2 * A_CH])
    acc = jnp.broadcast_to(cb_ref[...], (rows, A_CH))
    for k in range(A_CONV):
        acc = acc + cw_ref[k:k + 1, :] * ext_ref[k * nb:k * nb + rows, :]
    a_ref[0] = _silu(_layernorm(acc, lg_ref[...], lb_ref[...])).astype(BF16)
    ast_ref[...] = ext_ref[rows:rows + hist, :]
    q_ref[...] = proj[:, Q_OFF:K_OFF]
    for g in range(N_GROUPS):
        kv_ref[:, 2 * g * GROUP_CH:(2 * g + 1) * GROUP_CH] = _group_cols(proj, K_OFF, g)
        kv_ref[:, (2 * g + 1) * GROUP_CH:(2 * g + 2) * GROUP_CH] = _group_cols(proj, V_OFF, g)


def _in0_sample(x, mod, g, w, abuf_t, cw, cb, lg, lb, nb, steps):
    rows = nb * steps
    hist = (A_CONV - 1) * nb
    return pl.pallas_call(
        functools.partial(_in0_sample_kernel, nb=nb, steps=steps),
        grid=(1,),
        in_specs=[_const_spec((1, rows, D_MODEL)), _const_spec((1, 6, rows, D_MODEL)),
                  _const_spec((1, D_MODEL)), _const_spec((D_MODEL, E_IN)), _const_spec((hist, A_CH)),
                  _const_spec((A_CONV, A_CH)), _const_spec((1, A_CH)), _const_spec((1, A_CH)),
                  _const_spec((1, A_CH))],
        out_specs=[_const_spec((1, rows, A_CH)), _const_spec((hist, A_CH)),
                   _const_spec((rows, N_GROUPS * GROUP_CH)), _const_spec((rows, 2 * N_GROUPS * GROUP_CH))],
        out_shape=[jax.ShapeDtypeStruct((1, rows, A_CH), BF16), jax.ShapeDtypeStruct((hist, A_CH), F32),
                   jax.ShapeDtypeStruct((rows, N_GROUPS * GROUP_CH), F32),
                   jax.ShapeDtypeStruct((rows, 2 * N_GROUPS * GROUP_CH), F32)],
        scratch_shapes=[pltpu.VMEM((hist + rows, A_CH), F32)],
        compiler_params=_params(1),
        name="l0_in_sample",
    )(x, mod, g, w, abuf_t, cw, cb, lg, lb)


def _attn_sample_kernel(q_ref, new_ref, c0_ref, c1_ref, c2_ref, b0_ref, b1_ref, b2_ref,
                        attn_ref, s0_ref, s1_ref, s2_ref, *, steps, pad):
    rows = HPG * steps
    row_head = lax.broadcasted_iota(jnp.int32, (rows, GROUP_CH), 0) // steps
    col_head = lax.broadcasted_iota(jnp.int32, (rows, GROUP_CH), 1) // HEAD_DIM
    step_masks = _head_masks(steps)
    outs, lses = [], []
    for g, (c_ref, b_ref, s_ref) in enumerate(zip((c0_ref, c1_ref, c2_ref), (b0_ref, b1_ref, b2_ref),
                                                  (s0_ref, s1_ref, s2_ref))):
        w = c_ref.shape[-1]
        comb = jnp.concatenate([c_ref[0], new_ref[0, 2 * g * GROUP_CH:2 * (g + 1) * GROUP_CH, :]], axis=1)
        s_ref[0] = pltpu.roll(comb, w + pad - steps, axis=1)[:, :w]
        comb_bf = comb.astype(BF16)
        q = q_ref[0][:, g * GROUP_CH:(g + 1) * GROUP_CH]
        q_bd = jnp.where(row_head == col_head, jnp.concatenate([q] * HPG, axis=0), 0.0).astype(BF16)
        s = _mm(q_bd, comb_bf[:GROUP_CH]) * (HEAD_DIM ** -0.5) + b_ref[...]
        m = jnp.max(s, axis=-1, keepdims=True)
        e = jnp.exp(s - m)
        l = jnp.sum(e, axis=-1, keepdims=True)
        p = (e * (1.0 / l)).astype(BF16)
        o_all = _mm_nt(p, comb_bf[GROUP_CH:])
        lse_all = jnp.broadcast_to(m + jnp.log(l), (rows, GROUP_CH))
        o = jnp.zeros((steps, GROUP_CH), F32)
        lse = jnp.zeros((steps, GROUP_CH), F32)
        for h in range(HPG):
            o = jnp.where(step_masks[h], o_all[h * steps:(h + 1) * steps], o)
            lse = jnp.where(step_masks[h], lse_all[h * steps:(h + 1) * steps], lse)
        outs.append(o)
        lses.append(lse)
    attn_ref[0] = _mix_groups(outs, lses)


def _attn_sample(q, new_t, caches, biases, steps, pad):
    nb = q.shape[0]
    per_n = lambda n: (n, 0, 0)
    cache_specs = [pl.BlockSpec((1, 2 * GROUP_CH, w), per_n) for w in WINDOWS]
    return pl.pallas_call(
        functools.partial(_attn_sample_kernel, steps=steps, pad=pad),
        grid=(nb,),
        in_specs=[pl.BlockSpec((1, steps, N_GROUPS * GROUP_CH), per_n),
                  pl.BlockSpec((1, 2 * N_GROUPS * GROUP_CH, pad), per_n)] + cache_specs
                 + [_const_spec((HPG * steps, w + pad)) for w in WINDOWS],
        out_specs=[pl.BlockSpec((1, steps, GROUP_CH), per_n)] + cache_specs,
        out_shape=[jax.ShapeDtypeStruct((nb, steps, GROUP_CH), F32)]
                  + [jax.ShapeDtypeStruct((nb, 2 * GROUP_CH, w), F32) for w in WINDOWS],
        compiler_params=_params(1),
        name="l0_attn_sample",
    )(q, new_t, *caches, *biases)


def _in1_sample_kernel(x_ref, mod_ref, g_ref, w_ref, cbuf_ref, cw_ref, cb_ref, lg_ref, lb_ref, coef_ref, bs_ref,
                       wo_ref, y_ref, cst_ref, dv_ref, ext_ref, *, nb, steps):
    rows = nb * steps
    hist = (C_CONV - 1) * nb
    x = x_ref[0]
    h = _rms_mod(x, g_ref[...], mod_ref[0, 1], mod_ref[0, 0]).astype(BF16)
    proj = _mm(h, w_ref[...])
    ext_ref[0:hist, :] = cbuf_ref[...]
    ext_ref[hist:hist + rows, :] = proj[:, C_CH:2 * C_CH] * proj[:, 2 * C_CH:3 * C_CH]
    conv = jnp.broadcast_to(cb_ref[...], (rows, C_CH))
    for k in range(C_CONV):
        conv = conv + cw_ref[k:k + 1, :] * ext_ref[k * nb:k * nb + rows, :]
    yc = proj[:, :C_CH] * conv
    cst_ref[...] = ext_ref[rows:rows + hist, :]
    uv = jax.nn.gelu(proj[:, 3 * C_CH:], approximate=True)
    v = _layernorm(uv[:, D_CH:], lg_ref[...], lb_ref[...])
    dv_ref[...] = v
    v_r = v.astype(BF16).astype(F32)
    coef = coef_ref[...].astype(BF16).astype(F32)
    mixes = []
    for i in range(steps):
        mix = jnp.broadcast_to(bs_ref[i:i + 1, :], (nb, D_CH))
        for j in range(i + 1):
            mix = mix + coef[i * steps + j:i * steps + j + 1, :] * v_r[j * nb:(j + 1) * nb, :]
        mixes.append(mix)
    yd = uv[:, :D_CH] * jnp.concatenate(mixes, axis=0)
    cat = jnp.concatenate([yc.astype(BF16), yd.astype(BF16)], axis=1)
    y_ref[0] = x + mod_ref[0, 2] * _mm(cat, wo_ref[...])


def _in1_sample(x, mod, g, w, cbuf_t, cw, cb, lg, lb, coef, bs_rows, wo, nb, steps):
    rows = nb * steps
    hist = (C_CONV - 1) * nb
    return pl.pallas_call(
        functools.partial(_in1_sample_kernel, nb=nb, steps=steps),
        grid=(1,),
        in_specs=[_const_spec((1, rows, D_MODEL)), _const_spec((1, 6, rows, D_MODEL)),
                  _const_spec((1, D_MODEL)), _const_spec((D_MODEL, O_IN)), _const_spec((hist, C_CH)),
                  _const_spec((C_CONV, C_CH)), _const_spec((1, C_CH)), _const_spec((1, D_CH)),
                  _const_spec((1, D_CH)), _const_spec((steps * steps, D_CH)), _const_spec((steps, D_CH)),
                  _const_spec((C_CH + D_CH, D_MODEL))],
        out_specs=[_const_spec((1, rows, D_MODEL)), _const_spec((hist, C_CH)), _const_spec((rows, D_CH))],
        out_shape=[jax.ShapeDtypeStruct((1, rows, D_MODEL), F32), jax.ShapeDtypeStruct((hist, C_CH), F32),
                   jax.ShapeDtypeStruct((rows, D_CH), F32)],
        scratch_shapes=[pltpu.VMEM((hist + rows, C_CH), F32)],
        compiler_params=_params(1),
        name="l1_mixer_sample",
    )(x, mod, g, w, cbuf_t, cw, cb, lg, lb, coef, bs_rows, wo)


def _t5_bucket(dist):
    n = np.maximum(np.asarray(dist, dtype=np.int64), 0)
    max_exact = N_BUCKETS // 2
    large = max_exact + (np.log(np.maximum(n, 1) / max_exact) / np.log(MAX_DISTANCE / max_exact)
                         * (N_BUCKETS - max_exact)).astype(np.int32)
    return np.where(n < max_exact, n, np.minimum(large, N_BUCKETS - 1)).astype(np.int32)


def _prompt_bias_base(table, grp):
    rel = Q_BLOCK - np.arange(2 * Q_BLOCK)
    band = (rel >= 0) & (rel <= SUB_WINDOW)
    tab = table[:, grp * HPG:(grp + 1) * HPG]
    bias = tab[_t5_bucket(np.clip(rel, 0, None) * DILATIONS[grp])].T
    return jnp.where(band[None], bias, NEG).astype(F32)


def _sample_bias(table, grp, steps, pad):
    w, d = WINDOWS[grp], DILATIONS[grp]
    delta = w + steps - 1 - np.arange(w + pad + steps - 1)
    valid = (delta >= 0) & (delta % d == 0) & (delta // d <= SUB_WINDOW)
    tab = table[:, grp * HPG:(grp + 1) * HPG]
    vec = jnp.where(valid[None], tab[_t5_bucket(np.clip(delta, 0, None))].T, NEG).astype(F32)
    per_step = [vec[:, steps - 1 - s:steps - 1 - s + w + pad] for s in range(steps)]
    return jnp.stack(per_step, axis=1).reshape(HPG * steps, w + pad)


def _step_major(x):
    return jnp.swapaxes(x, 0, 1).reshape((x.shape[0] * x.shape[1],) + x.shape[2:])


def _batch_major(x, nb, steps):
    return jnp.swapaxes(x.reshape((steps, nb) + x.shape[1:]), 0, 1)


def kernel(x_prompt, x_sample, c_prompt, c_sample, state_a_conv, cache_b_kv0, cache_b_kv1, cache_b_kv2, state_c_conv, rel_bias_table, ada_w, ada_b, norm_mix_g, norm_ffn_g, ffn_w_gate_up, ffn_w_down, final_norm_g, e_w_in, a_conv_w, a_conv_b, a_ln_g, a_ln_b, e_w_out, o_w_in, c_conv_w, c_conv_b, d_ln_g, d_ln_b, d_spatial_w, d_spatial_b, o_w_out):
    nb_p = x_prompt.shape[0]
    nb_s, steps, _ = x_sample.shape
    pad = LANES

    w_in0 = e_w_in[0].astype(BF16)
    w_out0 = e_w_out[0].astype(BF16)
    w_in1 = o_w_in[0].astype(BF16)
    w_out1 = o_w_out[0].astype(BF16)
    w_gu = ffn_w_gate_up.astype(BF16)
    w_dn = ffn_w_down.astype(BF16)
    fin_g = final_norm_g.reshape(1, D_MODEL)

    mod = _ada(jnp.concatenate([c_prompt, c_sample], axis=0), ada_w, ada_b)
    mod = mod.reshape(mod.shape[0], nb_p + nb_s, 6, D_MODEL)
    mod_p = [mod[l, :nb_p].reshape(nb_p, 6, 1, D_MODEL) for l in range(2)]
    mod_s = [jnp.swapaxes(jnp.tile(mod[l, nb_p:], (steps, 1, 1)), 0, 1)[None] for l in range(2)]

    (a, a_st_p, q0, q1, q2, kv0, kv1, kv2, st0, st1, st2) = _in0_prompt(
        x_prompt, mod_p[0], norm_mix_g[0:1], w_in0, a_conv_w[0], a_conv_b, a_ln_g, a_ln_b)
    base = jnp.stack([_prompt_bias_base(rel_bias_table, grp) for grp in range(N_GROUPS)])
    attn = _attn_prompt((q0, q1, q2), (kv0, kv1, kv2), base)
    xp = _ffn(x_prompt, mod_p[0], norm_ffn_g[0:1], w_gu, w_dn, fin_g, 0, False, (a, attn, w_out0))
    bs_full = jnp.repeat(d_spatial_b[0].T, D_GROUP_CH, axis=1)
    xp, c_st_p = _in1_prompt(xp, mod_p[1], norm_mix_g[1:2], w_in1, c_conv_w[0], c_conv_b, d_ln_g, d_ln_b,
                             d_spatial_w[0], bs_full, w_out1)
    y_prompt = _ffn(xp, mod_p[1], norm_ffn_g[1:2], w_gu, w_dn, fin_g, 1, True)

    xs = _step_major(x_sample)[None]
    abuf_t = _step_major(state_a_conv[0])
    a_s, a_st_s, q_s, kv_s = _in0_sample(xs, mod_s[0], norm_mix_g[0:1], w_in0, abuf_t, a_conv_w[0], a_conv_b,
                                         a_ln_g, a_ln_b, nb_s, steps)
    q_b = _batch_major(q_s, nb_s, steps)
    new_t = jnp.transpose(kv_s.reshape(steps, nb_s, -1), (1, 2, 0))
    new_t = jnp.pad(new_t, ((0, 0), (0, 0), (0, pad - steps)))
    caches = [jnp.transpose(c[0], (0, 2, 3, 4, 1)).reshape(nb_s, 2 * GROUP_CH, c.shape[2])
              for c in (cache_b_kv0, cache_b_kv1, cache_b_kv2)]
    biases = [_sample_bias(rel_bias_table, grp, steps, pad) for grp in range(N_GROUPS)]
    attn_s, s0, s1, s2 = _attn_sample(q_b, new_t, caches, biases, steps, pad)
    attn_s = _step_major(attn_s).astype(BF16)[None]
    xs = _ffn(xs, mod_s[0], norm_ffn_g[0:1], w_gu, w_dn, fin_g, 0, False, (a_s, attn_s, w_out0))
    cbuf_t = _step_major(state_c_conv[0])
    coef = jnp.repeat(jnp.transpose(d_spatial_w[0][:, :steps, :steps], (1, 2, 0)).reshape(steps * steps, D_GROUPS),
                      D_GROUP_CH, axis=1)
    xs, c_st_s, dv_s = _in1_sample(xs, mod_s[1], norm_mix_g[1:2], w_in1, cbuf_t, c_conv_w[0], c_conv_b, d_ln_g,
                                   d_ln_b, coef, bs_full[:steps], w_out1, nb_s, steps)
    y_sample = _ffn(xs, mod_s[1], norm_ffn_g[1:2], w_gu, w_dn, fin_g, 1, True)

    def kv_state_prompt(st):
        return st.reshape(1, nb_p, st.shape[1], 2, HPG, HEAD_DIM)

    def kv_state_sample(st):
        return jnp.transpose(st.reshape(nb_s, 2, HPG, HEAD_DIM, st.shape[-1]), (0, 4, 1, 2, 3))[None]

    return (y_prompt, _batch_major(y_sample[0], nb_s, steps),
            a_st_p[None], _batch_major(a_st_s, nb_s, A_CONV - 1)[None],
            kv_state_prompt(st0), kv_state_sample(s0), kv_state_prompt(st1), kv_state_sample(s1),
            kv_state_prompt(st2), kv_state_sample(s2),
            c_st_p[None], _batch_major(c_st_s, nb_s, C_CONV - 1)[None],
            _batch_major(dv_s, nb_s, steps)[None])
```

```python
import functools

import numpy as np
import jax
import jax.numpy as jnp
from jax import lax
from jax.experimental import pallas as pl
from jax.experimental.pallas import tpu as pltpu

F32 = jnp.float32
BF16 = jnp.bfloat16

D_MODEL = 1024
EPS = 1e-6
A_CH = 256
A_CONV = 31
HEAD_DIM = 64
HPG = 4
GROUP_CH = HPG * HEAD_DIM
WINDOWS = (128, 512, 2048)
DILATIONS = (1, 4, 16)
N_GROUPS = 3
SUB_WINDOW = 128
Q_BLOCK = 128
N_BUCKETS = 32
MAX_DISTANCE = 2048
C_CH = 512
C_CONV = 3
D_CH = 512
D_GROUPS = 4
D_GROUP_CH = D_CH // D_GROUPS
D_CHUNK = 128
FFN_HIDDEN = 2816
Q_OFF = 2 * A_CH
K_OFF = Q_OFF + N_GROUPS * GROUP_CH
V_OFF = K_OFF + N_GROUPS * GROUP_CH
E_IN = V_OFF + N_GROUPS * GROUP_CH
O_IN = 3 * C_CH + 2 * D_CH
LANES = 128
SUBLANES = 8
GROUP_TILES = GROUP_CH // LANES
NEG = -1e30
ROW_TILE = 512
FFN_ROW_TILE = 512
CONV_ROWS = 64
MIX_ROWS = 256
VMEM_LIMIT = 56 * 1024 * 1024


def _params(n_axes):
    return pltpu.CompilerParams(dimension_semantics=("arbitrary",) * n_axes, vmem_limit_bytes=VMEM_LIMIT)


def _const_spec(shape):
    return pl.BlockSpec(shape, lambda *_: (0,) * len(shape), pipeline_mode=pl.Buffered(1))


def _layer_spec(shape, layer):
    return pl.BlockSpec((1,) + shape, lambda *_: (layer,) + (0,) * len(shape), pipeline_mode=pl.Buffered(1))


def _rms_mod(x, g, scale, shift):
    return (x * lax.rsqrt(jnp.mean(x * x, axis=-1, keepdims=True) + EPS)) * (g * (1.0 + scale)) + shift


def _layernorm(x, g, b):
    mu = jnp.mean(x, axis=-1, keepdims=True)
    xc = x - mu
    var = jnp.mean(xc * xc, axis=-1, keepdims=True)
    return xc * lax.rsqrt(var + EPS) * g + b


def _silu(x):
    return x * jax.nn.sigmoid(x)


def _mm(a, b):
    return jnp.dot(a, b, preferred_element_type=F32)


def _mm_nt(a, b):
    return lax.dot_general(a, b, (((1,), (1,)), ((), ())), preferred_element_type=F32)


def _group_cols(proj, off, g):
    return proj[:, off + g * GROUP_CH:off + (g + 1) * GROUP_CH]


def _ada_kernel(c_ref, w_ref, b_ref, o_ref):
    cs = _silu(c_ref[...]).astype(BF16)
    o_ref[0] = _mm(cs, w_ref[0].astype(BF16)) + b_ref[0]


def _ada(c_all, ada_w, ada_b):
    depth, _, width = ada_w.shape
    nb = c_all.shape[0]
    tn = 1536
    return pl.pallas_call(
        _ada_kernel,
        grid=(depth, width // tn),
        in_specs=[_const_spec((nb, D_MODEL)),
                  pl.BlockSpec((1, D_MODEL, tn), lambda l, j: (l, 0, j)),
                  pl.BlockSpec((1, 1, tn), lambda l, j: (l, 0, j))],
        out_specs=pl.BlockSpec((1, nb, tn), lambda l, j: (l, 0, j)),
        out_shape=jax.ShapeDtypeStruct((depth, nb, width), F32),
        compiler_params=_params(2),
        name="ada_mod",
    )(c_all, ada_w, ada_b.reshape(depth, 1, width))


def _ordering_zero(x):
    bits = lax.shift_right_logical(lax.bitcast_convert_type(x, jnp.uint32), jnp.uint32(16))
    return lax.bitcast_convert_type(lax.shift_right_logical(bits, jnp.uint32(16)), F32)


def _conv_ln_silu(ext_ref, shift_ref, base, rows, wb_ref, cb_ref, lg_ref, lb_ref, taps, after):
    ch = cb_ref.shape[-1]
    zero = _ordering_zero(after[-SUBLANES:, :ch])
    acc = jnp.broadcast_to(cb_ref[...], (rows // SUBLANES, SUBLANES, ch))
    for k in range(taps):
        m = (base + k) % SUBLANES
        src = ext_ref if m == 0 else shift_ref.at[m - 1]
        x = src[pl.ds(base + k - m, rows), :].reshape(rows // SUBLANES, SUBLANES, ch)
        acc = acc + (wb_ref[k] + zero)[None] * x
    return _silu(_layernorm(acc.reshape(rows, ch), lg_ref[...], lb_ref[...]))


def _in0_prompt_kernel(x_ref, mod_ref, g_ref, w_ref, cw_ref, cb_ref, lg_ref, lb_ref,
                       a_ref, ast_ref, q0_ref, q1_ref, q2_ref, kv0_ref, kv1_ref, kv2_ref,
                       st0_ref, st1_ref, st2_ref, ext_ref, shift_ref, perm_ref, wb_ref, *, tm):
    t = pl.program_id(1)
    last = pl.num_programs(1) - 1
    halo = A_CONV - 1
    pad = 32

    @pl.when(t == 0)
    def _():
        ext_ref[0:pad, :] = jnp.zeros((pad, A_CH), F32)
        for k in range(A_CONV):
            wb_ref[k] = jnp.broadcast_to(cw_ref[k:k + 1, :], (SUBLANES, A_CH))

    h = _rms_mod(x_ref[0], g_ref[...], mod_ref[0, 1], mod_ref[0, 0]).astype(BF16)
    a_in = _mm(h, w_ref[:, :Q_OFF])
    ext_ref[pad:pad + tm, :] = a_in[:, :A_CH] * jax.nn.sigmoid(a_in[:, A_CH:])
    for m in range(1, SUBLANES):
        shift_ref[m - 1] = ext_ref[pl.ds(m, tm + pad - SUBLANES), :]

    def conv_chunk(c, after):
        r0 = c * CONV_ROWS
        y = _conv_ln_silu(ext_ref, shift_ref, pad - halo + r0, CONV_ROWS, wb_ref, cb_ref, lg_ref, lb_ref, A_CONV,
                          after)
        a_ref[0, r0:r0 + CONV_ROWS, :] = y.astype(BF16)

    def project(g, off, dst, col0, slot0):
        d = DILATIONS[g]
        val = _mm(h, w_ref[:, off + g * GROUP_CH:off + (g + 1) * GROUP_CH])
        if off != Q_OFF:
            st_refs[g][0, :, col0:col0 + GROUP_CH] = val[tm - min(WINDOWS[g], tm):]
        if d == 1:
            dst[0, 0, :, col0:col0 + GROUP_CH] = val.astype(BF16)
            return val
        for c in range(GROUP_TILES):
            tile = perm_ref.at[slot0 + c]
            tile[...] = val[:, c * LANES:(c + 1) * LANES]
            for r in range(d):
                dst[0, r, :, col0 + c * LANES:col0 + (c + 1) * LANES] = (
                    tile[pl.ds(r, tm // d, stride=d), :].astype(BF16))
        return val

    q_refs, kv_refs, st_refs = (q0_ref, q1_ref, q2_ref), (kv0_ref, kv1_ref, kv2_ref), (st0_ref, st1_ref, st2_ref)
    tasks = [(g, off, (q_refs if off == Q_OFF else kv_refs)[g], GROUP_CH if off == V_OFF else 0)
             for g in range(N_GROUPS) for off in (Q_OFF, K_OFF, V_OFF)]
    n_chunks = tm // CONV_ROWS
    for i, (g, off, dst, col0) in enumerate(tasks):
        val = project(g, off, dst, col0, i * GROUP_TILES)
        if i < n_chunks:
            conv_chunk(i, val)
    for c in range(len(tasks), n_chunks):
        conv_chunk(c, val)
    ext_ref[pl.ds(pad - halo, halo), :] = ext_ref[pl.ds(tm + pad - halo, halo), :]

    @pl.when(t == last)
    def _():
        ast_ref[0] = ext_ref[pl.ds(pad - halo, halo), :]


def _in0_prompt(x, mod, g, w, cw, cb, lg, lb):
    nb, t, _ = x.shape
    tm = ROW_TILE
    assert t % tm == 0 and tm >= WINDOWS[1] and WINDOWS[2] == t
    row = lambda n, i: (n, i, 0)
    per_n = lambda n, i: (n, 0, 0)
    res = lambda n, i: (n, 0, i, 0)
    out_shape = [jax.ShapeDtypeStruct((nb, t, A_CH), BF16), jax.ShapeDtypeStruct((nb, A_CONV - 1, A_CH), F32)]
    out_specs = [pl.BlockSpec((1, tm, A_CH), row), pl.BlockSpec((1, A_CONV - 1, A_CH), per_n)]
    for width in (GROUP_CH, 2 * GROUP_CH):
        for d in DILATIONS:
            out_shape.append(jax.ShapeDtypeStruct((nb, d, t // d, width), BF16))
            out_specs.append(pl.BlockSpec((1, d, tm // d, width), res))
    out_shape += [jax.ShapeDtypeStruct((nb, WINDOWS[0], 2 * GROUP_CH), F32),
                  jax.ShapeDtypeStruct((nb, WINDOWS[1], 2 * GROUP_CH), F32),
                  jax.ShapeDtypeStruct((nb, t, 2 * GROUP_CH), F32)]
    out_specs += [pl.BlockSpec((1, WINDOWS[0], 2 * GROUP_CH), per_n),
                  pl.BlockSpec((1, WINDOWS[1], 2 * GROUP_CH), per_n),
                  pl.BlockSpec((1, tm, 2 * GROUP_CH), row)]
    return pl.pallas_call(
        functools.partial(_in0_prompt_kernel, tm=tm),
        grid=(nb, t // tm),
        in_specs=[pl.BlockSpec((1, tm, D_MODEL), row),
                  pl.BlockSpec((1, 6, 1, D_MODEL), lambda n, i: (n, 0, 0, 0)),
                  _const_spec((1, D_MODEL)), _const_spec((D_MODEL, E_IN)),
                  _const_spec((A_CONV, A_CH)), _const_spec((1, A_CH)),
                  _const_spec((1, A_CH)), _const_spec((1, A_CH))],
        out_specs=out_specs,
        out_shape=out_shape,
        scratch_shapes=[pltpu.VMEM((tm + 32, A_CH), F32), pltpu.VMEM((SUBLANES - 1, tm + 32 - SUBLANES, A_CH), F32),
                        pltpu.VMEM((3 * N_GROUPS * GROUP_TILES, tm, LANES), F32),
                        pltpu.VMEM((A_CONV, SUBLANES, A_CH), F32)],
        compiler_params=_params(2),
        name="l0_in_prompt",
    )(x, mod, g, w, cw, cb, lg, lb)


def _head_masks(rows):
    col_head = lax.broadcasted_iota(jnp.int32, (rows, GROUP_CH), 1) // HEAD_DIM
    return [col_head == h for h in range(HPG)]


def _mix_groups(outs, lses):
    m = jnp.maximum(jnp.maximum(lses[0], lses[1]), lses[2])
    es = [jnp.exp(l - m) for l in lses]
    inv = 1.0 / (es[0] + es[1] + es[2])
    return (es[0] * inv) * outs[0] + (es[1] * inv) * outs[1] + (es[2] * inv) * outs[2]


def _attn_prompt_kernel(q0_ref, kv0_ref, q1_ref, kv1_ref, q2_ref, kv2_ref, base_ref, attn_ref,
                        bias_ref, o_ref, lse_ref, *, t):
    masks = _head_masks(Q_BLOCK)
    masks_bf = [jnp.where(m, HEAD_DIM ** -0.5, 0.0).astype(BF16) for m in masks]
    key_idx = lax.broadcasted_iota(jnp.int32, (HPG * Q_BLOCK, 2 * Q_BLOCK), 1)

    @pl.when(pl.program_id(0) == 0)
    def _():
        for g in range(N_GROUPS):
            for h in range(HPG):
                row = jnp.broadcast_to(base_ref[g, h:h + 1, :], (Q_BLOCK, 2 * Q_BLOCK))
                bias_ref[g, h * Q_BLOCK:(h + 1) * Q_BLOCK, :] = pltpu.roll(row, 0, 1, stride=1, stride_axis=0)

    for g, (d, q_ref, kv_ref) in enumerate(zip(DILATIONS, (q0_ref, q1_ref, q2_ref), (kv0_ref, kv1_ref, kv2_ref))):
        nblk = t // d // Q_BLOCK
        single = nblk == 1

        def block(j, carry, g=g, d=d, q_ref=q_ref, kv_ref=kv_ref, nblk=nblk, single=single):
            r = j // nblk
            i = j % nblk
            rows = pl.ds(pl.multiple_of(i * Q_BLOCK, Q_BLOCK), Q_BLOCK)
            q = q_ref[0, r, rows, :]
            qs = jnp.concatenate([q * mb for mb in masks_bf], axis=0)
            if single:
                kk = kv_ref[0, r, rows, :]
                bias = bias_ref[g, :, Q_BLOCK:]
            else:
                prev = pl.ds(pl.multiple_of(jnp.maximum(i - 1, 0) * Q_BLOCK, Q_BLOCK), Q_BLOCK)
                kk = jnp.concatenate([kv_ref[0, r, prev, :], kv_ref[0, r, rows, :]], axis=0)
                bias = bias_ref[g] + jnp.where(key_idx < jnp.where(i == 0, Q_BLOCK, 0), NEG, 0.0)
            s = _mm_nt(qs, kk[:, :GROUP_CH]) + bias
            m = jnp.max(s, axis=-1, keepdims=True)
            e = jnp.exp(s - m)
            l = jnp.sum(e, axis=-1, keepdims=True)
            o_all = _mm((e * (1.0 / l)).astype(BF16), kk[:, GROUP_CH:])
            lse_all = jnp.broadcast_to(m + jnp.log(l), (HPG * Q_BLOCK, GROUP_CH))
            o = o_all[:Q_BLOCK]
            lse = lse_all[:Q_BLOCK]
            for h in range(1, HPG):
                o = jnp.where(masks[h], o_all[h * Q_BLOCK:(h + 1) * Q_BLOCK], o)
                lse = jnp.where(masks[h], lse_all[h * Q_BLOCK:(h + 1) * Q_BLOCK], lse)
            tokens = rows if d == 1 else pl.ds(r + i * (Q_BLOCK * d), Q_BLOCK, stride=d)
            for c in range(GROUP_TILES):
                o_ref[g, c, tokens, :] = o[:, c * LANES:(c + 1) * LANES]
                lse_ref[g, c, tokens, :] = lse[:, c * LANES:(c + 1) * LANES]
            return carry

        lax.fori_loop(0, d * nblk, block, 0, unroll=4)

    for r0 in range(0, t, MIX_ROWS):
        for c in range(GROUP_TILES):
            outs = [o_ref[g, c, r0:r0 + MIX_ROWS, :] for g in range(N_GROUPS)]
            lses = [lse_ref[g, c, r0:r0 + MIX_ROWS, :] for g in range(N_GROUPS)]
            attn_ref[0, r0:r0 + MIX_ROWS, c * LANES:(c + 1) * LANES] = _mix_groups(outs, lses).astype(BF16)


def _attn_prompt(qs, kvs, base):
    nb, _, t, _ = qs[0].shape
    blk = lambda n: (n, 0, 0, 0)
    in_specs = []
    for d in DILATIONS:
        in_specs += [pl.BlockSpec((1, d, t // d, GROUP_CH), blk), pl.BlockSpec((1, d, t // d, 2 * GROUP_CH), blk)]
    operands = [x for pair in zip(qs, kvs) for x in pair]
    return pl.pallas_call(
        functools.partial(_attn_prompt_kernel, t=t),
        grid=(nb,),
        in_specs=in_specs + [_const_spec((N_GROUPS, HPG, 2 * Q_BLOCK))],
        out_specs=pl.BlockSpec((1, t, GROUP_CH), lambda n: (n, 0, 0)),
        out_shape=jax.ShapeDtypeStruct((nb, t, GROUP_CH), BF16),
        scratch_shapes=[pltpu.VMEM((N_GROUPS, HPG * Q_BLOCK, 2 * Q_BLOCK), F32),
                        pltpu.VMEM((N_GROUPS, GROUP_TILES, t, LANES), F32),
                        pltpu.VMEM((N_GROUPS, GROUP_TILES, t, LANES), F32)],
        compiler_params=_params(1),
        name="l0_attn_prompt",
    )(*operands, base)


def _ffn_kernel(x_ref, mod_ref, g_ref, wgu_ref, wd_ref, fg_ref, *rest, final, chunk, mixer_out):
    x = x_ref[0]
    if mixer_out:
        a_ref, attn_ref, wo_ref, y_ref = rest
        x = x + mod_ref[0, 2] * _mm(jnp.concatenate([a_ref[0], attn_ref[0]], axis=1), wo_ref[...])
    else:
        (y_ref,) = rest
    h = _rms_mod(x, g_ref[...], mod_ref[0, 4], mod_ref[0, 3]).astype(BF16)
    acc = jnp.zeros(x.shape, F32)
    for c0 in range(0, FFN_HIDDEN, chunk):
        c1 = min(c0 + chunk, FFN_HIDDEN)
        gate = _mm(h, wgu_ref[0, :, c0:c1])
        up = _mm(h, wgu_ref[0, :, FFN_HIDDEN + c0:FFN_HIDDEN + c1])
        acc = acc + _mm((_silu(gate) * up).astype(BF16), wd_ref[0, c0:c1, :])
    y = x + mod_ref[0, 5] * acc
    if final:
        y = y * lax.rsqrt(jnp.mean(y * y, axis=-1, keepdims=True) + EPS) * fg_ref[...]
    y_ref[0] = y


def _ffn(x, mod, g, wgu, wd, final_g, layer, final, mixer_out=None):
    nb, t, _ = x.shape
    tm = min(FFN_ROW_TILE, t)
    rm = mod.shape[2]
    row = lambda n, i: (n, i, 0)
    mod_map = (lambda n, i: (n, 0, 0, 0)) if rm == 1 else (lambda n, i: (n, 0, i, 0))
    in_specs = [pl.BlockSpec((1, tm, D_MODEL), row),
                pl.BlockSpec((1, 6, min(rm, tm), D_MODEL), mod_map),
                _const_spec((1, D_MODEL)),
                _layer_spec((D_MODEL, 2 * FFN_HIDDEN), layer), _layer_spec((FFN_HIDDEN, D_MODEL), layer),
                _const_spec((1, D_MODEL))]
    operands = [x, mod, g, wgu, wd, final_g]
    if mixer_out is not None:
        a, attn, wo = mixer_out
        in_specs += [pl.BlockSpec((1, tm, A_CH), row), pl.BlockSpec((1, tm, GROUP_CH), row),
                     _const_spec((A_CH + GROUP_CH, D_MODEL))]
        operands += [a, attn, wo]
    return pl.pallas_call(
        functools.partial(_ffn_kernel, final=final, chunk=512, mixer_out=mixer_out is not None),
        grid=(nb, t // tm),
        in_specs=in_specs,
        out_specs=pl.BlockSpec((1, tm, D_MODEL), row),
        out_shape=jax.ShapeDtypeStruct((nb, t, D_MODEL), F32),
        compiler_params=_params(2),
        name="ffn_final" if final else "l0_out_ffn",
    )(*operands)


def _spatial_tril(ws_ref, grp):
    i = lax.broadcasted_iota(jnp.int32, (D_CHUNK, D_CHUNK), 0)
    j = lax.broadcasted_iota(jnp.int32, (D_CHUNK, D_CHUNK), 1)
    return jnp.where(j <= i, ws_ref[grp], 0.0).astype(BF16)


def _in1_prompt_kernel(x_ref, mod_ref, g_ref, w_ref, cw_ref, cb_ref, lg_ref, lb_ref, ws_ref, bs_ref, wo_ref,
                       y_ref, cst_ref, ext_ref, cat_ref, *, tm):
    t = pl.program_id(1)
    last = pl.num_programs(1) - 1
    halo = C_CONV - 1
    pad = 8

    @pl.when(t == 0)
    def _():
        ext_ref[0:pad, :] = jnp.zeros((pad, C_CH), F32)

    x = x_ref[0]
    h = _rms_mod(x, g_ref[...], mod_ref[0, 1], mod_ref[0, 0]).astype(BF16)
    bg, cg, xi, u_in, v_in = (_mm(h, w_ref[:, c0:c0 + C_CH]) for c0 in range(0, O_IN, C_CH))
    ext_ref[pad:pad + tm, :] = cg * xi
    conv = jnp.broadcast_to(cb_ref[...], (tm, C_CH))
    for k in range(C_CONV):
        conv = conv + cw_ref[k:k + 1, :] * ext_ref[pl.ds(pad - halo + k, tm), :]
    cat_ref[:, :C_CH] = (bg * conv).astype(BF16)
    ext_ref[pl.ds(pad - halo, halo), :] = ext_ref[pl.ds(tm + pad - halo, halo), :]
    u = jax.nn.gelu(u_in, approximate=True)
    v = _layernorm(jax.nn.gelu(v_in, approximate=True), lg_ref[...], lb_ref[...]).astype(BF16)
    for grp in range(D_GROUPS):
        ws = _spatial_tril(ws_ref, grp)
        cols = slice(grp * D_GROUP_CH, (grp + 1) * D_GROUP_CH)
        for c in range(tm // D_CHUNK):
            rows = slice(c * D_CHUNK, (c + 1) * D_CHUNK)
            mix = _mm(ws, v[rows, cols]) + bs_ref[:, cols]
            cat_ref[rows, C_CH + grp * D_GROUP_CH:C_CH + (grp + 1) * D_GROUP_CH] = (u[rows, cols] * mix).astype(BF16)
    y_ref[0] = x + mod_ref[0, 2] * _mm(cat_ref[...], wo_ref[...])

    @pl.when(t == last)
    def _():
        cst_ref[0] = ext_ref[pl.ds(pad - halo, halo), :]


def _in1_prompt(x, mod, g, w, cw, cb, lg, lb, ws, bs_full, wo):
    nb, t, _ = x.shape
    tm = ROW_TILE
    row = lambda n, i: (n, i, 0)
    return pl.pallas_call(
        functools.partial(_in1_prompt_kernel, tm=tm),
        grid=(nb, t // tm),
        in_specs=[pl.BlockSpec((1, tm, D_MODEL), row),
                  pl.BlockSpec((1, 6, 1, D_MODEL), lambda n, i: (n, 0, 0, 0)),
                  _const_spec((1, D_MODEL)), _const_spec((D_MODEL, O_IN)),
                  _const_spec((C_CONV, C_CH)), _const_spec((1, C_CH)),
                  _const_spec((1, D_CH)), _const_spec((1, D_CH)),
                  _const_spec((D_GROUPS, D_CHUNK, D_CHUNK)), _const_spec((D_CHUNK, D_CH)),
                  _const_spec((C_CH + D_CH, D_MODEL))],
        out_specs=[pl.BlockSpec((1, tm, D_MODEL), row),
                   pl.BlockSpec((1, C_CONV - 1, C_CH), lambda n, i: (n, 0, 0))],
        out_shape=[jax.ShapeDtypeStruct((nb, t, D_MODEL), F32),
                   jax.ShapeDtypeStruct((nb, C_CONV - 1, C_CH), F32)],
        scratch_shapes=[pltpu.VMEM((tm + 8, C_CH), F32), pltpu.VMEM((tm, C_CH + D_CH), BF16)],
        compiler_params=_params(2),
        name="l1_mixer_prompt",
    )(x, mod, g, w, cw, cb, lg, lb, ws, bs_full, wo)


def _in0_sample_kernel(x_ref, mod_ref, g_ref, w_ref, abuf_ref, cw_ref, cb_ref, lg_ref, lb_ref,
                       a_ref, ast_ref, q_ref, kv_ref, ext_ref, *, nb, steps):
    rows = nb * steps
    hist = (A_CONV - 1) * nb
    h = _rms_mod(x_ref[0], g_ref[...], mod_ref[0, 1], mod_ref[0, 0]).astype(BF16)
    proj = _mm(h, w_ref[...])
    ext_ref[0:hist, :] = abuf_ref[...]
    ext_ref[hist:hist + rows, :] = proj[:, :A_CH] * jax.nn.sigmoid(proj[:, A_CH:2 * A_CH])
    acc = jnp.broadcast_to(cb_ref[...], (rows, A_CH))
    for k in range(A_CONV):
        acc = acc + cw_ref[k:k + 1, :] * ext_ref[k * nb:k * nb + rows, :]
    a_ref[0] = _silu(_layernorm(acc, lg_ref[...], lb_ref[...])).astype(BF16)
    ast_ref[...] = ext_ref[rows:rows + hist, :]
    q_ref[...] = proj[:, Q_OFF:K_OFF]
    for g in range(N_GROUPS):
        kv_ref[:, 2 * g * GROUP_CH:(2 * g + 1) * GROUP_CH] = _group_cols(proj, K_OFF, g)
        kv_ref[:, (2 * g + 1) * GROUP_CH:(2 * g + 2) * GROUP_CH] = _group_cols(proj, V_OFF, g)


def _in0_sample(x, mod, g, w, abuf_t, cw, cb, lg, lb, nb, steps):
    rows = nb * steps
    hist = (A_CONV - 1) * nb
    return pl.pallas_call(
        functools.partial(_in0_sample_kernel, nb=nb, steps=steps),
        grid=(1,),
        in_specs=[_const_spec((1, rows, D_MODEL)), _const_spec((1, 6, rows, D_MODEL)),
                  _const_spec((1, D_MODEL)), _const_spec((D_MODEL, E_IN)), _const_spec((hist, A_CH)),
                  _const_spec((A_CONV, A_CH)), _const_spec((1, A_CH)), _const_spec((1, A_CH)),
                  _const_spec((1, A_CH))],
        out_specs=[_const_spec((1, rows, A_CH)), _const_spec((hist, A_CH)),
                   _const_spec((rows, N_GROUPS * GROUP_CH)), _const_spec((rows, 2 * N_GROUPS * GROUP_CH))],
        out_shape=[jax.ShapeDtypeStruct((1, rows, A_CH), BF16), jax.ShapeDtypeStruct((hist, A_CH), F32),
                   jax.ShapeDtypeStruct((rows, N_GROUPS * GROUP_CH), F32),
                   jax.ShapeDtypeStruct((rows, 2 * N_GROUPS * GROUP_CH), F32)],
        scratch_shapes=[pltpu.VMEM((hist + rows, A_CH), F32)],
        compiler_params=_params(1),
        name="l0_in_sample",
    )(x, mod, g, w, abuf_t, cw, cb, lg, lb)


def _attn_sample_kernel(q_ref, new_ref, c0_ref, c1_ref, c2_ref, b0_ref, b1_ref, b2_ref,
                        attn_ref, s0_ref, s1_ref, s2_ref, *, steps, pad):
    rows = HPG * steps
    row_head = lax.broadcasted_iota(jnp.int32, (rows, GROUP_CH), 0) // steps
    col_head = lax.broadcasted_iota(jnp.int32, (rows, GROUP_CH), 1) // HEAD_DIM
    step_masks = _head_masks(steps)
    outs, lses = [], []
    for g, (c_ref, b_ref, s_ref) in enumerate(zip((c0_ref, c1_ref, c2_ref), (b0_ref, b1_ref, b2_ref),
                                                  (s0_ref, s1_ref, s2_ref))):
        w = c_ref.shape[-1]
        comb = jnp.concatenate([c_ref[0], new_ref[0, 2 * g * GROUP_CH:2 * (g + 1) * GROUP_CH, :]], axis=1)
        s_ref[0] = pltpu.roll(comb, w + pad - steps, axis=1)[:, :w]
        comb_bf = comb.astype(BF16)
        q = q_ref[0][:, g * GROUP_CH:(g + 1) * GROUP_CH]
        q_bd = jnp.where(row_head == col_head, jnp.concatenate([q] * HPG, axis=0), 0.0).astype(BF16)
        s = _mm(q_bd, comb_bf[:GROUP_CH]) * (HEAD_DIM ** -0.5) + b_ref[...]
        m = jnp.max(s, axis=-1, keepdims=True)
        e = jnp.exp(s - m)
        l = jnp.sum(e, axis=-1, keepdims=True)
        p = (e * (1.0 / l)).astype(BF16)
        o_all = _mm_nt(p, comb_bf[GROUP_CH:])
        lse_all = jnp.broadcast_to(m + jnp.log(l), (rows, GROUP_CH))
        o = jnp.zeros((steps, GROUP_CH), F32)
        lse = jnp.zeros((steps, GROUP_CH), F32)
        for h in range(HPG):
            o = jnp.where(step_masks[h], o_all[h * steps:(h + 1) * steps], o)
            lse = jnp.where(step_masks[h], lse_all[h * steps:(h + 1) * steps], lse)
        outs.append(o)
        lses.append(lse)
    attn_ref[0] = _mix_groups(outs, lses)


def _attn_sample(q, new_t, caches, biases, steps, pad):
    nb = q.shape[0]
    per_n = lambda n: (n, 0, 0)
    cache_specs = [pl.BlockSpec((1, 2 * GROUP_CH, w), per_n) for w in WINDOWS]
    return pl.pallas_call(
        functools.partial(_attn_sample_kernel, steps=steps, pad=pad),
        grid=(nb,),
        in_specs=[pl.BlockSpec((1, steps, N_GROUPS * GROUP_CH), per_n),
                  pl.BlockSpec((1, 2 * N_GROUPS * GROUP_CH, pad), per_n)] + cache_specs
                 + [_const_spec((HPG * steps, w + pad)) for w in WINDOWS],
        out_specs=[pl.BlockSpec((1, steps, GROUP_CH), per_n)] + cache_specs,
        out_shape=[jax.ShapeDtypeStruct((nb, steps, GROUP_CH), F32)]
                  + [jax.ShapeDtypeStruct((nb, 2 * GROUP_CH, w), F32) for w in WINDOWS],
        compiler_params=_params(1),
        name="l0_attn_sample",
    )(q, new_t, *caches, *biases)


def _in1_sample_kernel(x_ref, mod_ref, g_ref, w_ref, cbuf_ref, cw_ref, cb_ref, lg_ref, lb_ref, coef_ref, bs_ref,
                       wo_ref, y_ref, cst_ref, dv_ref, ext_ref, *, nb, steps):
    rows = nb * steps
    hist = (C_CONV - 1) * nb
    x = x_ref[0]
    h = _rms_mod(x, g_ref[...], mod_ref[0, 1], mod_ref[0, 0]).astype(BF16)
    proj = _mm(h, w_ref[...])
    ext_ref[0:hist, :] = cbuf_ref[...]
    ext_ref[hist:hist + rows, :] = proj[:, C_CH:2 * C_CH] * proj[:, 2 * C_CH:3 * C_CH]
    conv = jnp.broadcast_to(cb_ref[...], (rows, C_CH))
    for k in range(C_CONV):
        conv = conv + cw_ref[k:k + 1, :] * ext_ref[k * nb:k * nb + rows, :]
    yc = proj[:, :C_CH] * conv
    cst_ref[...] = ext_ref[rows:rows + hist, :]
    uv = jax.nn.gelu(proj[:, 3 * C_CH:], approximate=True)
    v = _layernorm(uv[:, D_CH:], lg_ref[...], lb_ref[...])
    dv_ref[...] = v
    v_r = v.astype(BF16).astype(F32)
    coef = coef_ref[...].astype(BF16).astype(F32)
    mixes = []
    for i in range(steps):
        mix = jnp.broadcast_to(bs_ref[i:i + 1, :], (nb, D_CH))
        for j in range(i + 1):
            mix = mix + coef[i * steps + j:i * steps + j + 1, :] * v_r[j * nb:(j + 1) * nb, :]
        mixes.append(mix)
    yd = uv[:, :D_CH] * jnp.concatenate(mixes, axis=0)
    cat = jnp.concatenate([yc.astype(BF16), yd.astype(BF16)], axis=1)
    y_ref[0] = x + mod_ref[0, 2] * _mm(cat, wo_ref[...])


def _in1_sample(x, mod, g, w, cbuf_t, cw, cb, lg, lb, coef, bs_rows, wo, nb, steps):
    rows = nb * steps
    hist = (C_CONV - 1) * nb
    return pl.pallas_call(
        functools.partial(_in1_sample_kernel, nb=nb, steps=steps),
        grid=(1,),
        in_specs=[_const_spec((1, rows, D_MODEL)), _const_spec((1, 6, rows, D_MODEL)),
                  _const_spec((1, D_MODEL)), _const_spec((D_MODEL, O_IN)), _const_spec((hist, C_CH)),
                  _const_spec((C_CONV, C_CH)), _const_spec((1, C_CH)), _const_spec((1, D_CH)),
                  _const_spec((1, D_CH)), _const_spec((steps * steps, D_CH)), _const_spec((steps, D_CH)),
                  _const_spec((C_CH + D_CH, D_MODEL))],
        out_specs=[_const_spec((1, rows, D_MODEL)), _const_spec((hist, C_CH)), _const_spec((rows, D_CH))],
        out_shape=[jax.ShapeDtypeStruct((1, rows, D_MODEL), F32), jax.ShapeDtypeStruct((hist, C_CH), F32),
                   jax.ShapeDtypeStruct((rows, D_CH), F32)],
        scratch_shapes=[pltpu.VMEM((hist + rows, C_CH), F32)],
        compiler_params=_params(1),
        name="l1_mixer_sample",
    )(x, mod, g, w, cbuf_t, cw, cb, lg, lb, coef, bs_rows, wo)


def _t5_bucket(dist):
    n = np.maximum(np.asarray(dist, dtype=np.int64), 0)
    max_exact = N_BUCKETS // 2
    large = max_exact + (np.log(np.maximum(n, 1) / max_exact) / np.log(MAX_DISTANCE / max_exact)
                         * (N_BUCKETS - max_exact)).astype(np.int32)
    return np.where(n < max_exact, n, np.minimum(large, N_BUCKETS - 1)).astype(np.int32)


def _prompt_bias_base(table, grp):
    rel = Q_BLOCK - np.arange(2 * Q_BLOCK)
    band = (rel >= 0) & (rel <= SUB_WINDOW)
    tab = table[:, grp * HPG:(grp + 1) * HPG]
    bias = tab[_t5_bucket(np.clip(rel, 0, None) * DILATIONS[grp])].T
    return jnp.where(band[None], bias, NEG).astype(F32)


def _sample_bias(table, grp, steps, pad):
    w, d = WINDOWS[grp], DILATIONS[grp]
    delta = w + steps - 1 - np.arange(w + pad + steps - 1)
    valid = (delta >= 0) & (delta % d == 0) & (delta // d <= SUB_WINDOW)
    tab = table[:, grp * HPG:(grp + 1) * HPG]
    vec = jnp.where(valid[None], tab[_t5_bucket(np.clip(delta, 0, None))].T, NEG).astype(F32)
    per_step = [vec[:, steps - 1 - s:steps - 1 - s + w + pad] for s in range(steps)]
    return jnp.stack(per_step, axis=1).reshape(HPG * steps, w + pad)


def _step_major(x):
    return jnp.swapaxes(x, 0, 1).reshape((x.shape[0] * x.shape[1],) + x.shape[2:])


def _batch_major(x, nb, steps):
    return jnp.swapaxes(x.reshape((steps, nb) + x.shape[1:]), 0, 1)


def kernel(x_prompt, x_sample, c_prompt, c_sample, state_a_conv, cache_b_kv0, cache_b_kv1, cache_b_kv2, state_c_conv, rel_bias_table, ada_w, ada_b, norm_mix_g, norm_ffn_g, ffn_w_gate_up, ffn_w_down, final_norm_g, e_w_in, a_conv_w, a_conv_b, a_ln_g, a_ln_b, e_w_out, o_w_in, c_conv_w, c_conv_b, d_ln_g, d_ln_b, d_spatial_w, d_spatial_b, o_w_out):
    nb_p = x_prompt.shape[0]
    nb_s, steps, _ = x_sample.shape
    pad = LANES

    w_in0 = e_w_in[0].astype(BF16)
    w_out0 = e_w_out[0].astype(BF16)
    w_in1 = o_w_in[0].astype(BF16)
    w_out1 = o_w_out[0].astype(BF16)
    w_gu = ffn_w_gate_up.astype(BF16)
    w_dn = ffn_w_down.astype(BF16)
    fin_g = final_norm_g.reshape(1, D_MODEL)

    mod = _ada(jnp.concatenate([c_prompt, c_sample], axis=0), ada_w, ada_b)
    mod = mod.reshape(mod.shape[0], nb_p + nb_s, 6, D_MODEL)
    mod_p = [mod[l, :nb_p].reshape(nb_p, 6, 1, D_MODEL) for l in range(2)]
    mod_s = [jnp.swapaxes(jnp.tile(mod[l, nb_p:], (steps, 1, 1)), 0, 1)[None] for l in range(2)]

    (a, a_st_p, q0, q1, q2, kv0, kv1, kv2, st0, st1, st2) = _in0_prompt(
        x_prompt, mod_p[0], norm_mix_g[0:1], w_in0, a_conv_w[0], a_conv_b, a_ln_g, a_ln_b)
    base = jnp.stack([_prompt_bias_base(rel_bias_table, grp) for grp in range(N_GROUPS)])
    attn = _attn_prompt((q0, q1, q2), (kv0, kv1, kv2), base)
    xp = _ffn(x_prompt, mod_p[0], norm_ffn_g[0:1], w_gu, w_dn, fin_g, 0, False, (a, attn, w_out0))
    bs_full = jnp.repeat(d_spatial_b[0].T, D_GROUP_CH, axis=1)
    xp, c_st_p = _in1_prompt(xp, mod_p[1], norm_mix_g[1:2], w_in1, c_conv_w[0], c_conv_b, d_ln_g, d_ln_b,
                             d_spatial_w[0], bs_full, w_out1)
    y_prompt = _ffn(xp, mod_p[1], norm_ffn_g[1:2], w_gu, w_dn, fin_g, 1, True)

    xs = _step_major(x_sample)[None]
    abuf_t = _step_major(state_a_conv[0])
    a_s, a_st_s, q_s, kv_s = _in0_sample(xs, mod_s[0], norm_mix_g[0:1], w_in0, abuf_t, a_conv_w[0], a_conv_b,
                                         a_ln_g, a_ln_b, nb_s, steps)
    q_b = _batch_major(q_s, nb_s, steps)
    new_t = jnp.transpose(kv_s.reshape(steps, nb_s, -1), (1, 2, 0))
    new_t = jnp.pad(new_t, ((0, 0), (0, 0), (0, pad - steps)))
    caches = [jnp.transpose(c[0], (0, 2, 3, 4, 1)).reshape(nb_s, 2 * GROUP_CH, c.shape[2])
              for c in (cache_b_kv0, cache_b_kv1, cache_b_kv2)]
    biases = [_sample_bias(rel_bias_table, grp, steps, pad) for grp in range(N_GROUPS)]
    attn_s, s0, s1, s2 = _attn_sample(q_b, new_t, caches, biases, steps, pad)
    attn_s = _step_major(attn_s).astype(BF16)[None]
    xs = _ffn(xs, mod_s[0], norm_ffn_g[0:1], w_gu, w_dn, fin_g, 0, False, (a_s, attn_s, w_out0))
    cbuf_t = _step_major(state_c_conv[0])
    coef = jnp.repeat(jnp.transpose(d_spatial_w[0][:, :steps, :steps], (1, 2, 0)).reshape(steps * steps, D_GROUPS),
                      D_GROUP_CH, axis=1)
    xs, c_st_s, dv_s = _in1_sample(xs, mod_s[1], norm_mix_g[1:2], w_in1, cbuf_t, c_conv_w[0], c_conv_b, d_ln_g,
                                   d_ln_b, coef, bs_full[:steps], w_out1, nb_s, steps)
    y_sample = _ffn(xs, mod_s[1], norm_ffn_g[1:2], w_gu, w_dn, fin_g, 1, True)

    def kv_state_prompt(st):
        return st.reshape(1, nb_p, st.shape[1], 2, HPG, HEAD_DIM)

    def kv_state_sample(st):
        return jnp.transpose(st.reshape(nb_s, 2, HPG, HEAD_DIM, st.shape[-1]), (0, 4, 1, 2, 3))[None]

    return (y_prompt, _batch_major(y_sample[0], nb_s, steps),
            a_st_p[None], _batch_major(a_st_s, nb_s, A_CONV - 1)[None],
            kv_state_prompt(st0), kv_state_sample(s0), kv_state_prompt(st1), kv_state_sample(s1),
            kv_state_prompt(st2), kv_state_sample(s2),
            c_st_p[None], _batch_major(c_st_s, nb_s, C_CONV - 1)[None],
            _batch_major(dv_s, nb_s, steps)[None])
```

```python
import functools

import numpy as np
import jax
import jax.numpy as jnp
from jax import lax
from jax.experimental import pallas as pl
from jax.experimental.pallas import tpu as pltpu

F32 = jnp.float32
BF16 = jnp.bfloat16

D_MODEL = 1024
EPS = 1e-6
A_CH = 256
A_CONV = 31
HEAD_DIM = 64
HPG = 4
GROUP_CH = HPG * HEAD_DIM
WINDOWS = (128, 512, 2048)
DILATIONS = (1, 4, 16)
N_GROUPS = 3
SUB_WINDOW = 128
Q_BLOCK = 128
N_BUCKETS = 32
MAX_DISTANCE = 2048
C_CH = 512
C_CONV = 3
D_CH = 512
D_GROUPS = 4
D_GROUP_CH = D_CH // D_GROUPS
D_CHUNK = 128
FFN_HIDDEN = 2816
Q_OFF = 2 * A_CH
K_OFF = Q_OFF + N_GROUPS * GROUP_CH
V_OFF = K_OFF + N_GROUPS * GROUP_CH
E_IN = V_OFF + N_GROUPS * GROUP_CH
O_IN = 3 * C_CH + 2 * D_CH
LANES = 128
SUBLANES = 8
GROUP_TILES = GROUP_CH // LANES
NEG = -1e30
ROW_TILE = 512
FFN_ROW_TILE = 512
CONV_ROWS = 64
MIX_ROWS = 256
VMEM_LIMIT = 56 * 1024 * 1024


def _params(n_axes):
    return pltpu.CompilerParams(dimension_semantics=("arbitrary",) * n_axes, vmem_limit_bytes=VMEM_LIMIT)


def _const_spec(shape):
    return pl.BlockSpec(shape, lambda *_: (0,) * len(shape), pipeline_mode=pl.Buffered(1))


def _layer_spec(shape, layer):
    return pl.BlockSpec((1,) + shape, lambda *_: (layer,) + (0,) * len(shape), pipeline_mode=pl.Buffered(1))


def _rms_mod(x, g, scale, shift):
    return (x * lax.rsqrt(jnp.mean(x * x, axis=-1, keepdims=True) + EPS)) * (g * (1.0 + scale)) + shift


def _layernorm(x, g, b):
    mu = jnp.mean(x, axis=-1, keepdims=True)
    xc = x - mu
    var = jnp.mean(xc * xc, axis=-1, keepdims=True)
    return xc * lax.rsqrt(var + EPS) * g + b


def _silu(x):
    return x * jax.nn.sigmoid(x)


def _mm(a, b):
    return jnp.dot(a, b, preferred_element_type=F32)


def _mm_nt(a, b):
    return lax.dot_general(a, b, (((1,), (1,)), ((), ())), preferred_element_type=F32)


def _group_cols(proj, off, g):
    return proj[:, off + g * GROUP_CH:off + (g + 1) * GROUP_CH]


def _ada_kernel(c_ref, w_ref, b_ref, o_ref):
    cs = _silu(c_ref[...]).astype(BF16)
    o_ref[0] = _mm(cs, w_ref[0].astype(BF16)) + b_ref[0]


def _ada(c_all, ada_w, ada_b):
    depth, _, width = ada_w.shape
    nb = c_all.shape[0]
    tn = 1536
    return pl.pallas_call(
        _ada_kernel,
        grid=(depth, width // tn),
        in_specs=[_const_spec((nb, D_MODEL)),
                  pl.BlockSpec((1, D_MODEL, tn), lambda l, j: (l, 0, j)),
                  pl.BlockSpec((1, 1, tn), lambda l, j: (l, 0, j))],
        out_specs=pl.BlockSpec((1, nb, tn), lambda l, j: (l, 0, j)),
        out_shape=jax.ShapeDtypeStruct((depth, nb, width), F32),
        compiler_params=_params(2),
        name="ada_mod",
    )(c_all, ada_w, ada_b.reshape(depth, 1, width))


def _ordering_zero(x):
    bits = lax.shift_right_logical(lax.bitcast_convert_type(x, jnp.uint32), jnp.uint32(16))
    return lax.bitcast_convert_type(lax.shift_right_logical(bits, jnp.uint32(16)), F32)


def _conv_ln_silu(ext_ref, shift_ref, base, rows, wb_ref, cb_ref, lg_ref, lb_ref, taps, after):
    ch = cb_ref.shape[-1]
    zero = _ordering_zero(after[-SUBLANES:, :ch])
    acc = jnp.broadcast_to(cb_ref[...], (rows // SUBLANES, SUBLANES, ch))
    for k in range(taps):
        m = (base + k) % SUBLANES
        src = ext_ref if m == 0 else shift_ref.at[m - 1]
        x = src[pl.ds(base + k - m, rows), :].reshape(rows // SUBLANES, SUBLANES, ch)
        acc = acc + (wb_ref[k] + zero)[None] * x
    return _silu(_layernorm(acc.reshape(rows, ch), lg_ref[...], lb_ref[...]))


def _in0_prompt_kernel(x_ref, mod_ref, g_ref, w_ref, cw_ref, cb_ref, lg_ref, lb_ref,
                       a_ref, ast_ref, q0_ref, q1_ref, q2_ref, kv0_ref, kv1_ref, kv2_ref,
                       st0_ref, st1_ref, st2_ref, ext_ref, shift_ref, perm_ref, wb_ref, *, tm):
    t = pl.program_id(1)
    last = pl.num_programs(1) - 1
    halo = A_CONV - 1
    pad = 32

    @pl.when(t == 0)
    def _():
        ext_ref[0:pad, :] = jnp.zeros((pad, A_CH), F32)
        for k in range(A_CONV):
            wb_ref[k] = jnp.broadcast_to(cw_ref[k:k + 1, :], (SUBLANES, A_CH))

    h = _rms_mod(x_ref[0], g_ref[...], mod_ref[0, 1], mod_ref[0, 0]).astype(BF16)
    a_in = _mm(h, w_ref[:, :Q_OFF])
    ext_ref[pad:pad + tm, :] = a_in[:, :A_CH] * jax.nn.sigmoid(a_in[:, A_CH:])
    for m in range(1, SUBLANES):
        shift_ref[m - 1] = ext_ref[pl.ds(m, tm + pad - SUBLANES), :]

    def conv_chunk(c, after):
        r0 = c * CONV_ROWS
        y = _conv_ln_silu(ext_ref, shift_ref, pad - halo + r0, CONV_ROWS, wb_ref, cb_ref, lg_ref, lb_ref, A_CONV,
                          after)
        a_ref[0, r0:r0 + CONV_ROWS, :] = y.astype(BF16)

    def project(g, off, dst, col0, slot0):
        d = DILATIONS[g]
        val = _mm(h, w_ref[:, off + g * GROUP_CH:off + (g + 1) * GROUP_CH])
        if off != Q_OFF:
            st_refs[g][0, :, col0:col0 + GROUP_CH] = val[tm - min(WINDOWS[g], tm):]
        if d == 1:
            dst[0, 0, :, col0:col0 + GROUP_CH] = val.astype(BF16)
            return val
        for c in range(GROUP_TILES):
            tile = perm_ref.at[slot0 + c]
            tile[...] = val[:, c * LANES:(c + 1) * LANES]
            for r in range(d):
                dst[0, r, :, col0 + c * LANES:col0 + (c + 1) * LANES] = (
                    tile[pl.ds(r, tm // d, stride=d), :].astype(BF16))
        return val

    q_refs, kv_refs, st_refs = (q0_ref, q1_ref, q2_ref), (kv0_ref, kv1_ref, kv2_ref), (st0_ref, st1_ref, st2_ref)
    tasks = [(g, off, (q_refs if off == Q_OFF else kv_refs)[g], GROUP_CH if off == V_OFF else 0)
             for g in range(N_GROUPS) for off in (Q_OFF, K_OFF, V_OFF)]
    n_chunks = tm // CONV_ROWS
    for i, (g, off, dst, col0) in enumerate(tasks):
        val = project(g, off, dst, col0, i * GROUP_TILES)
        if i < n_chunks:
            conv_chunk(i, val)
    for c in range(len(tasks), n_chunks):
        conv_chunk(c, val)
    ext_ref[pl.ds(pad - halo, halo), :] = ext_ref[pl.ds(tm + pad - halo, halo), :]

    @pl.when(t == last)
    def _():
        ast_ref[0] = ext_ref[pl.ds(pad - halo, halo), :]


def _in0_prompt(x, mod, g, w, cw, cb, lg, lb):
    nb, t, _ = x.shape
    tm = ROW_TILE
    assert t % tm == 0 and tm >= WINDOWS[1] and WINDOWS[2] == t
    row = lambda n, i: (n, i, 0)
    per_n = lambda n, i: (n, 0, 0)
    res = lambda n, i: (n, 0, i, 0)
    out_shape = [jax.ShapeDtypeStruct((nb, t, A_CH), BF16), jax.ShapeDtypeStruct((nb, A_CONV - 1, A_CH), F32)]
    out_specs = [pl.BlockSpec((1, tm, A_CH), row), pl.BlockSpec((1, A_CONV - 1, A_CH), per_n)]
    for width in (GROUP_CH, 2 * GROUP_CH):
        for d in DILATIONS:
            out_shape.append(jax.ShapeDtypeStruct((nb, d, t // d, width), BF16))
            out_specs.append(pl.BlockSpec((1, d, tm // d, width), res))
    out_shape += [jax.ShapeDtypeStruct((nb, WINDOWS[0], 2 * GROUP_CH), F32),
                  jax.ShapeDtypeStruct((nb, WINDOWS[1], 2 * GROUP_CH), F32),
                  jax.ShapeDtypeStruct((nb, t, 2 * GROUP_CH), F32)]
    out_specs += [pl.BlockSpec((1, WINDOWS[0], 2 * GROUP_CH), per_n),
                  pl.BlockSpec((1, WINDOWS[1], 2 * GROUP_CH), per_n),
                  pl.BlockSpec((1, tm, 2 * GROUP_CH), row)]
    return pl.pallas_call(
        functools.partial(_in0_prompt_kernel, tm=tm),
        grid=(nb, t // tm),
        in_specs=[pl.BlockSpec((1, tm, D_MODEL), row),
                  pl.BlockSpec((1, 6, 1, D_MODEL), lambda n, i: (n, 0, 0, 0)),
                  _const_spec((1, D_MODEL)), _const_spec((D_MODEL, E_IN)),
                  _const_spec((A_CONV, A_CH)), _const_spec((1, A_CH)),
                  _const_spec((1, A_CH)), _const_spec((1, A_CH))],
        out_specs=out_specs,
        out_shape=out_shape,
        scratch_shapes=[pltpu.VMEM((tm + 32, A_CH), F32), pltpu.VMEM((SUBLANES - 1, tm + 32 - SUBLANES, A_CH), F32),
                        pltpu.VMEM((3 * N_GROUPS * GROUP_TILES, tm, LANES), F32),
                        pltpu.VMEM((A_CONV, SUBLANES, A_CH), F32)],
        compiler_params=_params(2),
        name="l0_in_prompt",
    )(x, mod, g, w, cw, cb, lg, lb)


def _head_masks(rows):
    col_head = lax.broadcasted_iota(jnp.int32, (rows, GROUP_CH), 1) // HEAD_DIM
    return [col_head == h for h in range(HPG)]


def _mix_groups(outs, lses):
    m = jnp.maximum(jnp.maximum(lses[0], lses[1]), lses[2])
    es = [jnp.exp(l - m) for l in lses]
    inv = 1.0 / (es[0] + es[1] + es[2])
    return (es[0] * inv) * outs[0] + (es[1] * inv) * outs[1] + (es[2] * inv) * outs[2]


def _attn_prompt_kernel(q0_ref, kv0_ref, q1_ref, kv1_ref, q2_ref, kv2_ref, base_ref, attn_ref,
                        bias_ref, o_ref, lse_ref, *, t):
    masks = _head_masks(Q_BLOCK)
    masks_bf = [jnp.where(m, HEAD_DIM ** -0.5, 0.0).astype(BF16) for m in masks]
    key_idx = lax.broadcasted_iota(jnp.int32, (HPG * Q_BLOCK, 2 * Q_BLOCK), 1)

    @pl.when(pl.program_id(0) == 0)
    def _():
        for g in range(N_GROUPS):
            for h in range(HPG):
                row = jnp.broadcast_to(base_ref[g, h:h + 1, :], (Q_BLOCK, 2 * Q_BLOCK))
                bias_ref[g, h * Q_BLOCK:(h + 1) * Q_BLOCK, :] = pltpu.roll(row, 0, 1, stride=1, stride_axis=0)

    for g, (d, q_ref, kv_ref) in enumerate(zip(DILATIONS, (q0_ref, q1_ref, q2_ref), (kv0_ref, kv1_ref, kv2_ref))):
        nblk = t // d // Q_BLOCK
        single = nblk == 1

        def block(j, carry, g=g, d=d, q_ref=q_ref, kv_ref=kv_ref, nblk=nblk, single=single):
            r = j // nblk
            i = j % nblk
            rows = pl.ds(pl.multiple_of(i * Q_BLOCK, Q_BLOCK), Q_BLOCK)
            q = q_ref[0, r, rows, :]
            qs = jnp.concatenate([q * mb for mb in masks_bf], axis=0)
            if single:
                kk = kv_ref[0, r, rows, :]
                bias = bias_ref[g, :, Q_BLOCK:]
            else:
                prev = pl.ds(pl.multiple_of(jnp.maximum(i - 1, 0) * Q_BLOCK, Q_BLOCK), Q_BLOCK)
                kk = jnp.concatenate([kv_ref[0, r, prev, :], kv_ref[0, r, rows, :]], axis=0)
                bias = bias_ref[g] + jnp.where(key_idx < jnp.where(i == 0, Q_BLOCK, 0), NEG, 0.0)
            s = _mm_nt(qs, kk[:, :GROUP_CH]) + bias
            m = jnp.max(s, axis=-1, keepdims=True)
            e = jnp.exp(s - m)
            l = jnp.sum(e, axis=-1, keepdims=True)
            o_all = _mm((e * (1.0 / l)).astype(BF16), kk[:, GROUP_CH:])
            lse_all = jnp.broadcast_to(m + jnp.log(l), (HPG * Q_BLOCK, GROUP_CH))
            o = o_all[:Q_BLOCK]
            lse = lse_all[:Q_BLOCK]
            for h in range(1, HPG):
                o = jnp.where(masks[h], o_all[h * Q_BLOCK:(h + 1) * Q_BLOCK], o)
                lse = jnp.where(masks[h], lse_all[h * Q_BLOCK:(h + 1) * Q_BLOCK], lse)
            tokens = rows if d == 1 else pl.ds(r + i * (Q_BLOCK * d), Q_BLOCK, stride=d)
            for c in range(GROUP_TILES):
                o_ref[g, c, tokens, :] = o[:, c * LANES:(c + 1) * LANES]
                lse_ref[g, c, tokens, :] = lse[:, c * LANES:(c + 1) * LANES]
            return carry

        lax.fori_loop(0, d * nblk, block, 0, unroll=16)

    for r0 in range(0, t, MIX_ROWS):
        for c in range(GROUP_TILES):
            outs = [o_ref[g, c, r0:r0 + MIX_ROWS, :] for g in range(N_GROUPS)]
            lses = [lse_ref[g, c, r0:r0 + MIX_ROWS, :] for g in range(N_GROUPS)]
            attn_ref[0, r0:r0 + MIX_ROWS, c * LANES:(c + 1) * LANES] = _mix_groups(outs, lses).astype(BF16)


def _attn_prompt(qs, kvs, base):
    nb, _, t, _ = qs[0].shape
    blk = lambda n: (n, 0, 0, 0)
    in_specs = []
    for d in DILATIONS:
        in_specs += [pl.BlockSpec((1, d, t // d, GROUP_CH), blk), pl.BlockSpec((1, d, t // d, 2 * GROUP_CH), blk)]
    operands = [x for pair in zip(qs, kvs) for x in pair]
    return pl.pallas_call(
        functools.partial(_attn_prompt_kernel, t=t),
        grid=(nb,),
        in_specs=in_specs + [_const_spec((N_GROUPS, HPG, 2 * Q_BLOCK))],
        out_specs=pl.BlockSpec((1, t, GROUP_CH), lambda n: (n, 0, 0)),
        out_shape=jax.ShapeDtypeStruct((nb, t, GROUP_CH), BF16),
        scratch_shapes=[pltpu.VMEM((N_GROUPS, HPG * Q_BLOCK, 2 * Q_BLOCK), F32),
                        pltpu.VMEM((N_GROUPS, GROUP_TILES, t, LANES), F32),
                        pltpu.VMEM((N_GROUPS, GROUP_TILES, t, LANES), F32)],
        compiler_params=_params(1),
        name="l0_attn_prompt",
    )(*operands, base)


def _ffn_kernel(x_ref, mod_ref, g_ref, wgu_ref, wd_ref, fg_ref, *rest, final, chunk, mixer_out):
    x = x_ref[0]
    if mixer_out:
        a_ref, attn_ref, wo_ref, y_ref = rest
        x = x + mod_ref[0, 2] * _mm(jnp.concatenate([a_ref[0], attn_ref[0]], axis=1), wo_ref[...])
    else:
        (y_ref,) = rest
    h = _rms_mod(x, g_ref[...], mod_ref[0, 4], mod_ref[0, 3]).astype(BF16)
    acc = jnp.zeros(x.shape, F32)
    for c0 in range(0, FFN_HIDDEN, chunk):
        c1 = min(c0 + chunk, FFN_HIDDEN)
        gate = _mm(h, wgu_ref[0, :, c0:c1])
        up = _mm(h, wgu_ref[0, :, FFN_HIDDEN + c0:FFN_HIDDEN + c1])
        acc = acc + _mm((_silu(gate) * up).astype(BF16), wd_ref[0, c0:c1, :])
    y = x + mod_ref[0, 5] * acc
    if final:
        y = y * lax.rsqrt(jnp.mean(y * y, axis=-1, keepdims=True) + EPS) * fg_ref[...]
    y_ref[0] = y


def _ffn(x, mod, g, wgu, wd, final_g, layer, final, mixer_out=None):
    nb, t, _ = x.shape
    tm = min(FFN_ROW_TILE, t)
    rm = mod.shape[2]
    row = lambda n, i: (n, i, 0)
    mod_map = (lambda n, i: (n, 0, 0, 0)) if rm == 1 else (lambda n, i: (n, 0, i, 0))
    in_specs = [pl.BlockSpec((1, tm, D_MODEL), row),
                pl.BlockSpec((1, 6, min(rm, tm), D_MODEL), mod_map),
                _const_spec((1, D_MODEL)),
                _layer_spec((D_MODEL, 2 * FFN_HIDDEN), layer), _layer_spec((FFN_HIDDEN, D_MODEL), layer),
                _const_spec((1, D_MODEL))]
    operands = [x, mod, g, wgu, wd, final_g]
    if mixer_out is not None:
        a, attn, wo = mixer_out
        in_specs += [pl.BlockSpec((1, tm, A_CH), row), pl.BlockSpec((1, tm, GROUP_CH), row),
                     _const_spec((A_CH + GROUP_CH, D_MODEL))]
        operands += [a, attn, wo]
    return pl.pallas_call(
        functools.partial(_ffn_kernel, final=final, chunk=512, mixer_out=mixer_out is not None),
        grid=(nb, t // tm),
        in_specs=in_specs,
        out_specs=pl.BlockSpec((1, tm, D_MODEL), row),
        out_shape=jax.ShapeDtypeStruct((nb, t, D_MODEL), F32),
        compiler_params=_params(2),
        name="ffn_final" if final else "l0_out_ffn",
    )(*operands)


def _spatial_tril(ws_ref, grp):
    i = lax.broadcasted_iota(jnp.int32, (D_CHUNK, D_CHUNK), 0)
    j = lax.broadcasted_iota(jnp.int32, (D_CHUNK, D_CHUNK), 1)
    return jnp.where(j <= i, ws_ref[grp], 0.0).astype(BF16)


def _in1_prompt_kernel(x_ref, mod_ref, g_ref, w_ref, cw_ref, cb_ref, lg_ref, lb_ref, ws_ref, bs_ref, wo_ref,
                       y_ref, cst_ref, ext_ref, cat_ref, *, tm):
    t = pl.program_id(1)
    last = pl.num_programs(1) - 1
    halo = C_CONV - 1
    pad = 8

    @pl.when(t == 0)
    def _():
        ext_ref[0:pad, :] = jnp.zeros((pad, C_CH), F32)

    x = x_ref[0]
    h = _rms_mod(x, g_ref[...], mod_ref[0, 1], mod_ref[0, 0]).astype(BF16)
    bg, cg, xi, u_in, v_in = (_mm(h, w_ref[:, c0:c0 + C_CH]) for c0 in range(0, O_IN, C_CH))
    ext_ref[pad:pad + tm, :] = cg * xi
    conv = jnp.broadcast_to(cb_ref[...], (tm, C_CH))
    for k in range(C_CONV):
        conv = conv + cw_ref[k:k + 1, :] * ext_ref[pl.ds(pad - halo + k, tm), :]
    cat_ref[:, :C_CH] = (bg * conv).astype(BF16)
    ext_ref[pl.ds(pad - halo, halo), :] = ext_ref[pl.ds(tm + pad - halo, halo), :]
    u = jax.nn.gelu(u_in, approximate=True)
    v = _layernorm(jax.nn.gelu(v_in, approximate=True), lg_ref[...], lb_ref[...]).astype(BF16)
    for grp in range(D_GROUPS):
        ws = _spatial_tril(ws_ref, grp)
        cols = slice(grp * D_GROUP_CH, (grp + 1) * D_GROUP_CH)
        for c in range(tm // D_CHUNK):
            rows = slice(c * D_CHUNK, (c + 1) * D_CHUNK)
            mix = _mm(ws, v[rows, cols]) + bs_ref[:, cols]
            cat_ref[rows, C_CH + grp * D_GROUP_CH:C_CH + (grp + 1) * D_GROUP_CH] = (u[rows, cols] * mix).astype(BF16)
    y_ref[0] = x + mod_ref[0, 2] * _mm(cat_ref[...], wo_ref[...])

    @pl.when(t == last)
    def _():
        cst_ref[0] = ext_ref[pl.ds(pad - halo, halo), :]


def _in1_prompt(x, mod, g, w, cw, cb, lg, lb, ws, bs_full, wo):
    nb, t, _ = x.shape
    tm = ROW_TILE
    row = lambda n, i: (n, i, 0)
    return pl.pallas_call(
        functools.partial(_in1_prompt_kernel, tm=tm),
        grid=(nb, t // tm),
        in_specs=[pl.BlockSpec((1, tm, D_MODEL), row),
                  pl.BlockSpec((1, 6, 1, D_MODEL), lambda n, i: (n, 0, 0, 0)),
                  _const_spec((1, D_MODEL)), _const_spec((D_MODEL, O_IN)),
                  _const_spec((C_CONV, C_CH)), _const_spec((1, C_CH)),
                  _const_spec((1, D_CH)), _const_spec((1, D_CH)),
                  _const_spec((D_GROUPS, D_CHUNK, D_CHUNK)), _const_spec((D_CHUNK, D_CH)),
                  _const_spec((C_CH + D_CH, D_MODEL))],
        out_specs=[pl.BlockSpec((1, tm, D_MODEL), row),
                   pl.BlockSpec((1, C_CONV - 1, C_CH), lambda n, i: (n, 0, 0))],
        out_shape=[jax.ShapeDtypeStruct((nb, t, D_MODEL), F32),
                   jax.ShapeDtypeStruct((nb, C_CONV - 1, C_CH), F32)],
        scratch_shapes=[pltpu.VMEM((tm + 8, C_CH), F32), pltpu.VMEM((tm, C_CH + D_CH), BF16)],
        compiler_params=_params(2),
        name="l1_mixer_prompt",
    )(x, mod, g, w, cw, cb, lg, lb, ws, bs_full, wo)


def _in0_sample_kernel(x_ref, mod_ref, g_ref, w_ref, abuf_ref, cw_ref, cb_ref, lg_ref, lb_ref,
                       a_ref, ast_ref, q_ref, kv_ref, ext_ref, *, nb, steps):
    rows = nb * steps
    hist = (A_CONV - 1) * nb
    h = _rms_mod(x_ref[0], g_ref[...], mod_ref[0, 1], mod_ref[0, 0]).astype(BF16)
    proj = _mm(h, w_ref[...])
    ext_ref[0:hist, :] = abuf_ref[...]
    ext_ref[hist:hist + rows, :] = proj[:, :A_CH] * jax.nn.sigmoid(proj[:, A_CH:2 * A_CH])
    acc = jnp.broadcast_to(cb_ref[...], (rows, A_CH))
    for k in range(A_CONV):
        acc = acc + cw_ref[k:k + 1, :] * ext_ref[k * nb:k * nb + rows, :]
    a_ref[0] = _silu(_layernorm(acc, lg_ref[...], lb_ref[...])).astype(BF16)
    ast_ref[...] = ext_ref[rows:rows + hist, :]
    q_ref[...] = proj[:, Q_OFF:K_OFF]
    for g in range(N_GROUPS):
        kv_ref[:, 2 * g * GROUP_CH:(2 * g + 1) * GROUP_CH] = _group_cols(proj, K_OFF, g)
        kv_ref[:, (2 * g + 1) * GROUP_CH:(2 * g + 2) * GROUP_CH] = _group_cols(proj, V_OFF, g)


def _in0_sample(x, mod, g, w, abuf_t, cw, cb, lg, lb, nb, steps):
    rows = nb * steps
    hist = (A_CONV - 1) * nb
    return pl.pallas_call(
        functools.partial(_in0_sample_kernel, nb=nb, steps=steps),
        grid=(1,),
        in_specs=[_const_spec((1, rows, D_MODEL)), _const_spec((1, 6, rows, D_MODEL)),
                  _const_spec((1, D_MODEL)), _const_spec((D_MODEL, E_IN)), _const_spec((hist, A_CH)),
                  _const_spec((A_CONV, A_CH)), _const_spec((1, A_CH)), _const_spec((1, A_CH)),
                  _const_spec((1, A_CH))],
        out_specs=[_const_spec((1, rows, A_CH)), _const_spec((hist, A_CH)),
                   _const_spec((rows, N_GROUPS * GROUP_CH)), _const_spec((rows, 2 * N_GROUPS * GROUP_CH))],
        out_shape=[jax.ShapeDtypeStruct((1, rows, A_CH), BF16), jax.ShapeDtypeStruct((hist, A_CH), F32),
                   jax.ShapeDtypeStruct((rows, N_GROUPS * GROUP_CH), F32),
                   jax.ShapeDtypeStruct((rows, 2 * N_GROUPS * GROUP_CH), F32)],
        scratch_shapes=[pltpu.VMEM((hist + rows, A_CH), F32)],
        compiler_params=_params(1),
        name="l0_in_sample",
    )(x, mod, g, w, abuf_t, cw, cb, lg, lb)


def _attn_sample_kernel(q_ref, new_ref, c0_ref, c1_ref, c2_ref, b0_ref, b1_ref, b2_ref,
                        attn_ref, s0_ref, s1_ref, s2_ref, *, steps, pad):
    rows = HPG * steps
    row_head = lax.broadcasted_iota(jnp.int32, (rows, GROUP_CH), 0) // steps
    col_head = lax.broadcasted_iota(jnp.int32, (rows, GROUP_CH), 1) // HEAD_DIM
    step_masks = _head_masks(steps)
    outs, lses = [], []
    for g, (c_ref, b_ref, s_ref) in enumerate(zip((c0_ref, c1_ref, c2_ref), (b0_ref, b1_ref, b2_ref),
                                                  (s0_ref, s1_ref, s2_ref))):
        w = c_ref.shape[-1]
        comb = jnp.concatenate([c_ref[0], new_ref[0, 2 * g * GROUP_CH:2 * (g + 1) * GROUP_CH, :]], axis=1)
        s_ref[0] = pltpu.roll(comb, w + pad - steps, axis=1)[:, :w]
        comb_bf = comb.astype(BF16)
        q = q_ref[0][:, g * GROUP_CH:(g + 1) * GROUP_CH]
        q_bd = jnp.where(row_head == col_head, jnp.concatenate([q] * HPG, axis=0), 0.0).astype(BF16)
        s = _mm(q_bd, comb_bf[:GROUP_CH]) * (HEAD_DIM ** -0.5) + b_ref[...]
        m = jnp.max(s, axis=-1, keepdims=True)
        e = jnp.exp(s - m)
        l = jnp.sum(e, axis=-1, keepdims=True)
        p = (e * (1.0 / l)).astype(BF16)
        o_all = _mm_nt(p, comb_bf[GROUP_CH:])
        lse_all = jnp.broadcast_to(m + jnp.log(l), (rows, GROUP_CH))
        o = jnp.zeros((steps, GROUP_CH), F32)
        lse = jnp.zeros((steps, GROUP_CH), F32)
        for h in range(HPG):
            o = jnp.where(step_masks[h], o_all[h * steps:(h + 1) * steps], o)
            lse = jnp.where(step_masks[h], lse_all[h * steps:(h + 1) * steps], lse)
        outs.append(o)
        lses.append(lse)
    attn_ref[0] = _mix_groups(outs, lses)


def _attn_sample(q, new_t, caches, biases, steps, pad):
    nb = q.shape[0]
    per_n = lambda n: (n, 0, 0)
    cache_specs = [pl.BlockSpec((1, 2 * GROUP_CH, w), per_n) for w in WINDOWS]
    return pl.pallas_call(
        functools.partial(_attn_sample_kernel, steps=steps, pad=pad),
        grid=(nb,),
        in_specs=[pl.BlockSpec((1, steps, N_GROUPS * GROUP_CH), per_n),
                  pl.BlockSpec((1, 2 * N_GROUPS * GROUP_CH, pad), per_n)] + cache_specs
                 + [_const_spec((HPG * steps, w + pad)) for w in WINDOWS],
        out_specs=[pl.BlockSpec((1, steps, GROUP_CH), per_n)] + cache_specs,
        out_shape=[jax.ShapeDtypeStruct((nb, steps, GROUP_CH), F32)]
                  + [jax.ShapeDtypeStruct((nb, 2 * GROUP_CH, w), F32) for w in WINDOWS],
        compiler_params=_params(1),
        name="l0_attn_sample",
    )(q, new_t, *caches, *biases)


def _in1_sample_kernel(x_ref, mod_ref, g_ref, w_ref, cbuf_ref, cw_ref, cb_ref, lg_ref, lb_ref, coef_ref, bs_ref,
                       wo_ref, y_ref, cst_ref, dv_ref, ext_ref, *, nb, steps):
    rows = nb * steps
    hist = (C_CONV - 1) * nb
    x = x_ref[0]
    h = _rms_mod(x, g_ref[...], mod_ref[0, 1], mod_ref[0, 0]).astype(BF16)
    proj = _mm(h, w_ref[...])
    ext_ref[0:hist, :] = cbuf_ref[...]
    ext_ref[hist:hist + rows, :] = proj[:, C_CH:2 * C_CH] * proj[:, 2 * C_CH:3 * C_CH]
    conv = jnp.broadcast_to(cb_ref[...], (rows, C_CH))
    for k in range(C_CONV):
        conv = conv + cw_ref[k:k + 1, :] * ext_ref[k * nb:k * nb + rows, :]
    yc = proj[:, :C_CH] * conv
    cst_ref[...] = ext_ref[rows:rows + hist, :]
    uv = jax.nn.gelu(proj[:, 3 * C_CH:], approximate=True)
    v = _layernorm(uv[:, D_CH:], lg_ref[...], lb_ref[...])
    dv_ref[...] = v
    v_r = v.astype(BF16).astype(F32)
    coef = coef_ref[...].astype(BF16).astype(F32)
    mixes = []
    for i in range(steps):
        mix = jnp.broadcast_to(bs_ref[i:i + 1, :], (nb, D_CH))
        for j in range(i + 1):
            mix = mix + coef[i * steps + j:i * steps + j + 1, :] * v_r[j * nb:(j + 1) * nb, :]
        mixes.append(mix)
    yd = uv[:, :D_CH] * jnp.concatenate(mixes, axis=0)
    cat = jnp.concatenate([yc.astype(BF16), yd.astype(BF16)], axis=1)
    y_ref[0] = x + mod_ref[0, 2] * _mm(cat, wo_ref[...])


def _in1_sample(x, mod, g, w, cbuf_t, cw, cb, lg, lb, coef, bs_rows, wo, nb, steps):
    rows = nb * steps
    hist = (C_CONV - 1) * nb
    return pl.pallas_call(
        functools.partial(_in1_sample_kernel, nb=nb, steps=steps),
        grid=(1,),
        in_specs=[_const_spec((1, rows, D_MODEL)), _const_spec((1, 6, rows, D_MODEL)),
                  _const_spec((1, D_MODEL)), _const_spec((D_MODEL, O_IN)), _const_spec((hist, C_CH)),
                  _const_spec((C_CONV, C_CH)), _const_spec((1, C_CH)), _const_spec((1, D_CH)),
                  _const_spec((1, D_CH)), _const_spec((steps * steps, D_CH)), _const_spec((steps, D_CH)),
                  _const_spec((C_CH + D_CH, D_MODEL))],
        out_specs=[_const_spec((1, rows, D_MODEL)), _const_spec((hist, C_CH)), _const_spec((rows, D_CH))],
        out_shape=[jax.ShapeDtypeStruct((1, rows, D_MODEL), F32), jax.ShapeDtypeStruct((hist, C_CH), F32),
                   jax.ShapeDtypeStruct((rows, D_CH), F32)],
        scratch_shapes=[pltpu.VMEM((hist + rows, C_CH), F32)],
        compiler_params=_params(1),
        name="l1_mixer_sample",
    )(x, mod, g, w, cbuf_t, cw, cb, lg, lb, coef, bs_rows, wo)


def _t5_bucket(dist):
    n = np.maximum(np.asarray(dist, dtype=np.int64), 0)
    max_exact = N_BUCKETS // 2
    large = max_exact + (np.log(np.maximum(n, 1) / max_exact) / np.log(MAX_DISTANCE / max_exact)
                         * (N_BUCKETS - max_exact)).astype(np.int32)
    return np.where(n < max_exact, n, np.minimum(large, N_BUCKETS - 1)).astype(np.int32)


def _prompt_bias_base(table, grp):
    rel = Q_BLOCK - np.arange(2 * Q_BLOCK)
    band = (rel >= 0) & (rel <= SUB_WINDOW)
    tab = table[:, grp * HPG:(grp + 1) * HPG]
    bias = tab[_t5_bucket(np.clip(rel, 0, None) * DILATIONS[grp])].T
    return jnp.where(band[None], bias, NEG).astype(F32)


def _sample_bias(table, grp, steps, pad):
    w, d = WINDOWS[grp], DILATIONS[grp]
    delta = w + steps - 1 - np.arange(w + pad + steps - 1)
    valid = (delta >= 0) & (delta % d == 0) & (delta // d <= SUB_WINDOW)
    tab = table[:, grp * HPG:(grp + 1) * HPG]
    vec = jnp.where(valid[None], tab[_t5_bucket(np.clip(delta, 0, None))].T, NEG).astype(F32)
    per_step = [vec[:, steps - 1 - s:steps - 1 - s + w + pad] for s in range(steps)]
    return jnp.stack(per_step, axis=1).reshape(HPG * steps, w + pad)


def _step_major(x):
    return jnp.swapaxes(x, 0, 1).reshape((x.shape[0] * x.shape[1],) + x.shape[2:])


def _batch_major(x, nb, steps):
    return jnp.swapaxes(x.reshape((steps, nb) + x.shape[1:]), 0, 1)


def kernel(x_prompt, x_sample, c_prompt, c_sample, state_a_conv, cache_b_kv0, cache_b_kv1, cache_b_kv2, state_c_conv, rel_bias_table, ada_w, ada_b, norm_mix_g, norm_ffn_g, ffn_w_gate_up, ffn_w_down, final_norm_g, e_w_in, a_conv_w, a_conv_b, a_ln_g, a_ln_b, e_w_out, o_w_in, c_conv_w, c_conv_b, d_ln_g, d_ln_b, d_spatial_w, d_spatial_b, o_w_out):
    nb_p = x_prompt.shape[0]
    nb_s, steps, _ = x_sample.shape
    pad = LANES

    w_in0 = e_w_in[0].astype(BF16)
    w_out0 = e_w_out[0].astype(BF16)
    w_in1 = o_w_in[0].astype(BF16)
    w_out1 = o_w_out[0].astype(BF16)
    w_gu = ffn_w_gate_up.astype(BF16)
    w_dn = ffn_w_down.astype(BF16)
    fin_g = final_norm_g.reshape(1, D_MODEL)

    mod = _ada(jnp.concatenate([c_prompt, c_sample], axis=0), ada_w, ada_b)
    mod = mod.reshape(mod.shape[0], nb_p + nb_s, 6, D_MODEL)
    mod_p = [mod[l, :nb_p].reshape(nb_p, 6, 1, D_MODEL) for l in range(2)]
    mod_s = [jnp.swapaxes(jnp.tile(mod[l, nb_p:], (steps, 1, 1)), 0, 1)[None] for l in range(2)]

    (a, a_st_p, q0, q1, q2, kv0, kv1, kv2, st0, st1, st2) = _in0_prompt(
        x_prompt, mod_p[0], norm_mix_g[0:1], w_in0, a_conv_w[0], a_conv_b, a_ln_g, a_ln_b)
    base = jnp.stack([_prompt_bias_base(rel_bias_table, grp) for grp in range(N_GROUPS)])
    attn = _attn_prompt((q0, q1, q2), (kv0, kv1, kv2), base)
    xp = _ffn(x_prompt, mod_p[0], norm_ffn_g[0:1], w_gu, w_dn, fin_g, 0, False, (a, attn, w_out0))
    bs_full = jnp.repeat(d_spatial_b[0].T, D_GROUP_CH, axis=1)
    xp, c_st_p = _in1_prompt(xp, mod_p[1], norm_mix_g[1:2], w_in1, c_conv_w[0], c_conv_b, d_ln_g, d_ln_b,
                             d_spatial_w[0], bs_full, w_out1)
    y_prompt = _ffn(xp, mod_p[1], norm_ffn_g[1:2], w_gu, w_dn, fin_g, 1, True)

    xs = _step_major(x_sample)[None]
    abuf_t = _step_major(state_a_conv[0])
    a_s, a_st_s, q_s, kv_s = _in0_sample(xs, mod_s[0], norm_mix_g[0:1], w_in0, abuf_t, a_conv_w[0], a_conv_b,
                                         a_ln_g, a_ln_b, nb_s, steps)
    q_b = _batch_major(q_s, nb_s, steps)
    new_t = jnp.transpose(kv_s.reshape(steps, nb_s, -1), (1, 2, 0))
    new_t = jnp.pad(new_t, ((0, 0), (0, 0), (0, pad - steps)))
    caches = [jnp.transpose(c[0], (0, 2, 3, 4, 1)).reshape(nb_s, 2 * GROUP_CH, c.shape[2])
              for c in (cache_b_kv0, cache_b_kv1, cache_b_kv2)]
    biases = [_sample_bias(rel_bias_table, grp, steps, pad) for grp in range(N_GROUPS)]
    attn_s, s0, s1, s2 = _attn_sample(q_b, new_t, caches, biases, steps, pad)
    attn_s = _step_major(attn_s).astype(BF16)[None]
    xs = _ffn(xs, mod_s[0], norm_ffn_g[0:1], w_gu, w_dn, fin_g, 0, False, (a_s, attn_s, w_out0))
    cbuf_t = _step_major(state_c_conv[0])
    coef = jnp.repeat(jnp.transpose(d_spatial_w[0][:, :steps, :steps], (1, 2, 0)).reshape(steps * steps, D_GROUPS),
                      D_GROUP_CH, axis=1)
    xs, c_st_s, dv_s = _in1_sample(xs, mod_s[1], norm_mix_g[1:2], w_in1, cbuf_t, c_conv_w[0], c_conv_b, d_ln_g,
                                   d_ln_b, coef, bs_full[:steps], w_out1, nb_s, steps)
    y_sample = _ffn(xs, mod_s[1], norm_ffn_g[1:2], w_gu, w_dn, fin_g, 1, True)

    def kv_state_prompt(st):
        return st.reshape(1, nb_p, st.shape[1], 2, HPG, HEAD_DIM)

    def kv_state_sample(st):
        return jnp.transpose(st.reshape(nb_s, 2, HPG, HEAD_DIM, st.shape[-1]), (0, 4, 1, 2, 3))[None]

    return (y_prompt, _batch_major(y_sample[0], nb_s, steps),
            a_st_p[None], _batch_major(a_st_s, nb_s, A_CONV - 1)[None],
            kv_state_prompt(st0), kv_state_sample(s0), kv_state_prompt(st1), kv_state_sample(s1),
            kv_state_prompt(st2), kv_state_sample(s2),
            c_st_p[None], _batch_major(c_st_s, nb_s, C_CONV - 1)[None],
            _batch_major(dv_s, nb_s, steps)[None])
```

```python
import functools

import numpy as np
import jax
import jax.numpy as jnp
from jax import lax
from jax.experimental import pallas as pl
from jax.experimental.pallas import tpu as pltpu

F32 = jnp.float32
BF16 = jnp.bfloat16

D_MODEL = 1024
EPS = 1e-6
A_CH = 256
A_CONV = 31
HEAD_DIM = 64
HPG = 4
GROUP_CH = HPG * HEAD_DIM
WINDOWS = (128, 512, 2048)
DILATIONS = (1, 4, 16)
N_GROUPS = 3
SUB_WINDOW = 128
Q_BLOCK = 128
N_BUCKETS = 32
MAX_DISTANCE = 2048
C_CH = 512
C_CONV = 3
D_CH = 512
D_GROUPS = 4
D_GROUP_CH = D_CH // D_GROUPS
D_CHUNK = 128
FFN_HIDDEN = 2816
Q_OFF = 2 * A_CH
K_OFF = Q_OFF + N_GROUPS * GROUP_CH
V_OFF = K_OFF + N_GROUPS * GROUP_CH
E_IN = V_OFF + N_GROUPS * GROUP_CH
O_IN = 3 * C_CH + 2 * D_CH
LANES = 128
SUBLANES = 8
GROUP_TILES = GROUP_CH // LANES
NEG = -1e30
ROW_TILE = 512
FFN_ROW_TILE = 512
CONV_ROWS = 64
MIX_ROWS = 256
SHIFT_ROWS = 64
VMEM_LIMIT = 56 * 1024 * 1024


def _params(n_axes):
    return pltpu.CompilerParams(dimension_semantics=("arbitrary",) * n_axes, vmem_limit_bytes=VMEM_LIMIT)


def _const_spec(shape):
    return pl.BlockSpec(shape, lambda *_: (0,) * len(shape), pipeline_mode=pl.Buffered(1))


def _layer_spec(shape, layer):
    return pl.BlockSpec((1,) + shape, lambda *_: (layer,) + (0,) * len(shape), pipeline_mode=pl.Buffered(1))


def _rms_mod(x, g, scale, shift):
    return (x * lax.rsqrt(jnp.mean(x * x, axis=-1, keepdims=True) + EPS)) * (g * (1.0 + scale)) + shift


def _layernorm(x, g, b):
    mu = jnp.mean(x, axis=-1, keepdims=True)
    xc = x - mu
    var = jnp.mean(xc * xc, axis=-1, keepdims=True)
    return xc * lax.rsqrt(var + EPS) * g + b


def _silu(x):
    return x * jax.nn.sigmoid(x)


def _mm(a, b):
    return jnp.dot(a, b, preferred_element_type=F32)


def _mm_nt(a, b):
    return lax.dot_general(a, b, (((1,), (1,)), ((), ())), preferred_element_type=F32)


def _group_cols(proj, off, g):
    return proj[:, off + g * GROUP_CH:off + (g + 1) * GROUP_CH]


def _ada_kernel(c_ref, w_ref, b_ref, o_ref):
    cs = _silu(c_ref[...]).astype(BF16)
    o_ref[0] = _mm(cs, w_ref[0].astype(BF16)) + b_ref[0]


def _ada(c_all, ada_w, ada_b):
    depth, _, width = ada_w.shape
    nb = c_all.shape[0]
    tn = 1536
    return pl.pallas_call(
        _ada_kernel,
        grid=(depth, width // tn),
        in_specs=[_const_spec((nb, D_MODEL)),
                  pl.BlockSpec((1, D_MODEL, tn), lambda l, j: (l, 0, j)),
                  pl.BlockSpec((1, 1, tn), lambda l, j: (l, 0, j))],
        out_specs=pl.BlockSpec((1, nb, tn), lambda l, j: (l, 0, j)),
        out_shape=jax.ShapeDtypeStruct((depth, nb, width), F32),
        compiler_params=_params(2),
        name="ada_mod",
    )(c_all, ada_w, ada_b.reshape(depth, 1, width))


def _ordering_zero(x):
    bits = lax.shift_right_logical(lax.bitcast_convert_type(x, jnp.uint32), jnp.uint32(16))
    return lax.bitcast_convert_type(lax.shift_right_logical(bits, jnp.uint32(16)), F32)


def _conv_ln_silu(ext_ref, shift_ref, base, rows, wb_ref, cb_ref, lg_ref, lb_ref, taps, after):
    ch = cb_ref.shape[-1]
    zero = _ordering_zero(after[-SUBLANES:, :ch])
    acc = jnp.broadcast_to(cb_ref[...], (rows // SUBLANES, SUBLANES, ch))
    for k in range(taps):
        m = (base + k) % SUBLANES
        src = ext_ref if m == 0 else shift_ref.at[m - 1]
        x = src[pl.ds(base + k - m, rows), :].reshape(rows // SUBLANES, SUBLANES, ch)
        acc = acc + (wb_ref[k] + zero)[None] * x
    return _silu(_layernorm(acc.reshape(rows, ch), lg_ref[...], lb_ref[...]))


def _in0_prompt_kernel(x_ref, mod_ref, g_ref, w_ref, cw_ref, cb_ref, lg_ref, lb_ref,
                       a_ref, ast_ref, q0_ref, q1_ref, q2_ref, kv0_ref, kv1_ref, kv2_ref,
                       st0_ref, st1_ref, st2_ref, ext_ref, shift_ref, perm_ref, wb_ref, *, tm):
    t = pl.program_id(1)
    last = pl.num_programs(1) - 1
    halo = A_CONV - 1
    pad = 32

    @pl.when(t == 0)
    def _():
        ext_ref[0:pad, :] = jnp.zeros((pad, A_CH), F32)
        for k in range(A_CONV):
            wb_ref[k] = jnp.broadcast_to(cw_ref[k:k + 1, :], (SUBLANES, A_CH))

    h = _rms_mod(x_ref[0], g_ref[...], mod_ref[0, 1], mod_ref[0, 0]).astype(BF16)
    a_in = _mm(h, w_ref[:, :Q_OFF])
    ext_ref[pad:pad + tm, :] = a_in[:, :A_CH] * jax.nn.sigmoid(a_in[:, A_CH:])
    for m in range(1, SUBLANES):
        shift_ref[m - 1] = ext_ref[pl.ds(m, tm + pad - SUBLANES), :]

    def conv_chunk(c, after):
        r0 = c * CONV_ROWS
        y = _conv_ln_silu(ext_ref, shift_ref, pad - halo + r0, CONV_ROWS, wb_ref, cb_ref, lg_ref, lb_ref, A_CONV,
                          after)
        a_ref[0, r0:r0 + CONV_ROWS, :] = y.astype(BF16)

    def project(g, off, dst, col0, slot0):
        d = DILATIONS[g]
        val = _mm(h, w_ref[:, off + g * GROUP_CH:off + (g + 1) * GROUP_CH])
        if off != Q_OFF:
            st_refs[g][0, :, col0:col0 + GROUP_CH] = val[tm - min(WINDOWS[g], tm):]
        if d == 1:
            dst[0, 0, :, col0:col0 + GROUP_CH] = val.astype(BF16)
            return val
        for c in range(GROUP_TILES):
            tile = perm_ref.at[slot0 + c]
            tile[...] = val[:, c * LANES:(c + 1) * LANES]
            for r in range(d):
                dst[0, r, :, col0 + c * LANES:col0 + (c + 1) * LANES] = (
                    tile[pl.ds(r, tm // d, stride=d), :].astype(BF16))
        return val

    q_refs, kv_refs, st_refs = (q0_ref, q1_ref, q2_ref), (kv0_ref, kv1_ref, kv2_ref), (st0_ref, st1_ref, st2_ref)
    tasks = [(g, off, (q_refs if off == Q_OFF else kv_refs)[g], GROUP_CH if off == V_OFF else 0)
             for g in range(N_GROUPS) for off in (Q_OFF, K_OFF, V_OFF)]
    n_chunks = tm // CONV_ROWS
    for i, (g, off, dst, col0) in enumerate(tasks):
        val = project(g, off, dst, col0, i * GROUP_TILES)
        if i < n_chunks:
            conv_chunk(i, val)
    for c in range(len(tasks), n_chunks):
        conv_chunk(c, val)
    ext_ref[pl.ds(pad - halo, halo), :] = ext_ref[pl.ds(tm + pad - halo, halo), :]

    @pl.when(t == last)
    def _():
        ast_ref[0] = ext_ref[pl.ds(pad - halo, halo), :]


def _in0_prompt(x, mod, g, w, cw, cb, lg, lb):
    nb, t, _ = x.shape
    tm = ROW_TILE
    assert t % tm == 0 and tm >= WINDOWS[1] and WINDOWS[2] == t
    row = lambda n, i: (n, i, 0)
    per_n = lambda n, i: (n, 0, 0)
    res = lambda n, i: (n, 0, i, 0)
    out_shape = [jax.ShapeDtypeStruct((nb, t, A_CH), BF16), jax.ShapeDtypeStruct((nb, A_CONV - 1, A_CH), F32)]
    out_specs = [pl.BlockSpec((1, tm, A_CH), row), pl.BlockSpec((1, A_CONV - 1, A_CH), per_n)]
    for width in (GROUP_CH, 2 * GROUP_CH):
        for d in DILATIONS:
            out_shape.append(jax.ShapeDtypeStruct((nb, d, t // d, width), BF16))
            out_specs.append(pl.BlockSpec((1, d, tm // d, width), res))
    out_shape += [jax.ShapeDtypeStruct((nb, WINDOWS[0], 2 * GROUP_CH), F32),
                  jax.ShapeDtypeStruct((nb, WINDOWS[1], 2 * GROUP_CH), F32),
                  jax.ShapeDtypeStruct((nb, t, 2 * GROUP_CH), F32)]
    out_specs += [pl.BlockSpec((1, WINDOWS[0], 2 * GROUP_CH), per_n),
                  pl.BlockSpec((1, WINDOWS[1], 2 * GROUP_CH), per_n),
                  pl.BlockSpec((1, tm, 2 * GROUP_CH), row)]
    return pl.pallas_call(
        functools.partial(_in0_prompt_kernel, tm=tm),
        grid=(nb, t // tm),
        in_specs=[pl.BlockSpec((1, tm, D_MODEL), row),
                  pl.BlockSpec((1, 6, 1, D_MODEL), lambda n, i: (n, 0, 0, 0)),
                  _const_spec((1, D_MODEL)), _const_spec((D_MODEL, E_IN)),
                  _const_spec((A_CONV, A_CH)), _const_spec((1, A_CH)),
                  _const_spec((1, A_CH)), _const_spec((1, A_CH))],
        out_specs=out_specs,
        out_shape=out_shape,
        scratch_shapes=[pltpu.VMEM((tm + 32, A_CH), F32), pltpu.VMEM((SUBLANES - 1, tm + 32 - SUBLANES, A_CH), F32),
                        pltpu.VMEM((3 * N_GROUPS * GROUP_TILES, tm, LANES), F32),
                        pltpu.VMEM((A_CONV, SUBLANES, A_CH), F32)],
        compiler_params=_params(2),
        name="l0_in_prompt",
    )(x, mod, g, w, cw, cb, lg, lb)


def _head_masks(rows):
    col_head = lax.broadcasted_iota(jnp.int32, (rows, GROUP_CH), 1) // HEAD_DIM
    return [col_head == h for h in range(HPG)]


def _mix_groups(outs, lses):
    m = jnp.maximum(jnp.maximum(lses[0], lses[1]), lses[2])
    es = [jnp.exp(l - m) for l in lses]
    inv = 1.0 / (es[0] + es[1] + es[2])
    return (es[0] * inv) * outs[0] + (es[1] * inv) * outs[1] + (es[2] * inv) * outs[2]


def _attn_prompt_kernel(q0_ref, kv0_ref, q1_ref, kv1_ref, q2_ref, kv2_ref, base_ref, attn_ref,
                        bias_ref, o_ref, lse_ref, *, t):
    masks = _head_masks(Q_BLOCK)
    masks_bf = [jnp.where(m, HEAD_DIM ** -0.5, 0.0).astype(BF16) for m in masks]
    key_idx = lax.broadcasted_iota(jnp.int32, (HPG * Q_BLOCK, 2 * Q_BLOCK), 1)

    @pl.when(pl.program_id(0) == 0)
    def _():
        for g in range(N_GROUPS):
            for h in range(HPG):
                row = jnp.broadcast_to(base_ref[g, h:h + 1, :], (Q_BLOCK, 2 * Q_BLOCK))
                bias_ref[g, h * Q_BLOCK:(h + 1) * Q_BLOCK, :] = pltpu.roll(row, 0, 1, stride=1, stride_axis=0)

    for g, (d, q_ref, kv_ref) in enumerate(zip(DILATIONS, (q0_ref, q1_ref, q2_ref), (kv0_ref, kv1_ref, kv2_ref))):
        nblk = t // d // Q_BLOCK
        single = nblk == 1

        def block(j, carry, g=g, d=d, q_ref=q_ref, kv_ref=kv_ref, nblk=nblk, single=single):
            r = j // nblk
            i = j % nblk
            rows = pl.ds(pl.multiple_of(i * Q_BLOCK, Q_BLOCK), Q_BLOCK)
            q = q_ref[0, r, rows, :]
            qs = jnp.concatenate([q * mb for mb in masks_bf], axis=0)
            if single:
                kk = kv_ref[0, r, rows, :]
                bias = bias_ref[g, :, Q_BLOCK:]
            else:
                prev = pl.ds(pl.multiple_of(jnp.maximum(i - 1, 0) * Q_BLOCK, Q_BLOCK), Q_BLOCK)
                kk = jnp.concatenate([kv_ref[0, r, prev, :], kv_ref[0, r, rows, :]], axis=0)
                bias = bias_ref[g] + jnp.where(key_idx < jnp.where(i == 0, Q_BLOCK, 0), NEG, 0.0)
            s = _mm_nt(qs, kk[:, :GROUP_CH]) + bias
            m = jnp.max(s, axis=-1, keepdims=True)
            e = jnp.exp(s - m)
            l = jnp.sum(e, axis=-1, keepdims=True)
            o_all = _mm((e * (1.0 / l)).astype(BF16), kk[:, GROUP_CH:])
            lse_all = jnp.broadcast_to(m + jnp.log(l), (HPG * Q_BLOCK, GROUP_CH))
            o = o_all[:Q_BLOCK]
            lse = lse_all[:Q_BLOCK]
            for h in range(1, HPG):
                o = jnp.where(masks[h], o_all[h * Q_BLOCK:(h + 1) * Q_BLOCK], o)
                lse = jnp.where(masks[h], lse_all[h * Q_BLOCK:(h + 1) * Q_BLOCK], lse)
            tokens = rows if d == 1 else pl.ds(r + i * (Q_BLOCK * d), Q_BLOCK, stride=d)
            for c in range(GROUP_TILES):
                o_ref[g, c, tokens, :] = o[:, c * LANES:(c + 1) * LANES]
                lse_ref[g, c, tokens, :] = lse[:, c * LANES:(c + 1) * LANES]
            return carry

        lax.fori_loop(0, d * nblk, block, 0, unroll=16)

    for r0 in range(0, t, MIX_ROWS):
        for c in range(GROUP_TILES):
            outs = [o_ref[g, c, r0:r0 + MIX_ROWS, :] for g in range(N_GROUPS)]
            lses = [lse_ref[g, c, r0:r0 + MIX_ROWS, :] for g in range(N_GROUPS)]
            attn_ref[0, r0:r0 + MIX_ROWS, c * LANES:(c + 1) * LANES] = _mix_groups(outs, lses).astype(BF16)


def _attn_prompt(qs, kvs, base):
    nb, _, t, _ = qs[0].shape
    blk = lambda n: (n, 0, 0, 0)
    in_specs = []
    for d in DILATIONS:
        in_specs += [pl.BlockSpec((1, d, t // d, GROUP_CH), blk), pl.BlockSpec((1, d, t // d, 2 * GROUP_CH), blk)]
    operands = [x for pair in zip(qs, kvs) for x in pair]
    return pl.pallas_call(
        functools.partial(_attn_prompt_kernel, t=t),
        grid=(nb,),
        in_specs=in_specs + [_const_spec((N_GROUPS, HPG, 2 * Q_BLOCK))],
        out_specs=pl.BlockSpec((1, t, GROUP_CH), lambda n: (n, 0, 0)),
        out_shape=jax.ShapeDtypeStruct((nb, t, GROUP_CH), BF16),
        scratch_shapes=[pltpu.VMEM((N_GROUPS, HPG * Q_BLOCK, 2 * Q_BLOCK), F32),
                        pltpu.VMEM((N_GROUPS, GROUP_TILES, t, LANES), F32),
                        pltpu.VMEM((N_GROUPS, GROUP_TILES, t, LANES), F32)],
        compiler_params=_params(1),
        name="l0_attn_prompt",
    )(*operands, base)


def _ffn_kernel(x_ref, mod_ref, g_ref, wgu_ref, wd_ref, fg_ref, *rest, final, chunk, mixer_out, shift_steps):
    rest = list(rest)
    state_ref = rest.pop() if shift_steps else None
    y_ref = rest.pop()
    x = x_ref[0]
    if shift_steps:
        newt_ref = rest.pop()
        cache_ref = rest.pop()
        step = pl.program_id(0) * pl.num_programs(1) + pl.program_id(1)
        for r0 in range(0, 2 * GROUP_CH, SHIFT_ROWS):
            new = _new_steps(newt_ref, r0, r0 + SHIFT_ROWS, step, shift_steps)
            state_ref[0, r0:r0 + SHIFT_ROWS, :] = _shift_window(cache_ref[0, r0:r0 + SHIFT_ROWS, :], new,
                                                                shift_steps)[1]
    if mixer_out:
        a_ref, attn_ref, wo_ref = rest
        x = x + mod_ref[0, 2] * _mm(jnp.concatenate([a_ref[0], attn_ref[0]], axis=1), wo_ref[...])
    h = _rms_mod(x, g_ref[...], mod_ref[0, 4], mod_ref[0, 3]).astype(BF16)
    acc = jnp.zeros(x.shape, F32)
    for c0 in range(0, FFN_HIDDEN, chunk):
        c1 = min(c0 + chunk, FFN_HIDDEN)
        gate = _mm(h, wgu_ref[0, :, c0:c1])
        up = _mm(h, wgu_ref[0, :, FFN_HIDDEN + c0:FFN_HIDDEN + c1])
        acc = acc + _mm((_silu(gate) * up).astype(BF16), wd_ref[0, c0:c1, :])
    y = x + mod_ref[0, 5] * acc
    if final:
        y = y * lax.rsqrt(jnp.mean(y * y, axis=-1, keepdims=True) + EPS) * fg_ref[...]
    y_ref[0] = y


def _ffn(x, mod, g, wgu, wd, final_g, layer, final, mixer_out=None, shift=None):
    nb, t, _ = x.shape
    tm = min(FFN_ROW_TILE, t)
    rm = mod.shape[2]
    row = lambda n, i: (n, i, 0)
    mod_map = (lambda n, i: (n, 0, 0, 0)) if rm == 1 else (lambda n, i: (n, 0, i, 0))
    in_specs = [pl.BlockSpec((1, tm, D_MODEL), row),
                pl.BlockSpec((1, 6, min(rm, tm), D_MODEL), mod_map),
                _const_spec((1, D_MODEL)),
                _layer_spec((D_MODEL, 2 * FFN_HIDDEN), layer), _layer_spec((FFN_HIDDEN, D_MODEL), layer),
                _const_spec((1, D_MODEL))]
    operands = [x, mod, g, wgu, wd, final_g]
    if mixer_out is not None:
        a, attn, wo = mixer_out
        in_specs += [pl.BlockSpec((1, tm, A_CH), row), pl.BlockSpec((1, tm, GROUP_CH), row),
                     _const_spec((A_CH + GROUP_CH, D_MODEL))]
        operands += [a, attn, wo]
    out_specs = [pl.BlockSpec((1, tm, D_MODEL), row)]
    out_shape = [jax.ShapeDtypeStruct((nb, t, D_MODEL), F32)]
    shift_steps = 0
    if shift is not None:
        cache, new_t, grp, shift_steps = shift
        n_tiles = t // tm
        assert cache.shape[0] == nb * n_tiles
        per_step = lambda n, i: (n * n_tiles + i, 0, 0)
        in_specs += [pl.BlockSpec((1,) + cache.shape[1:], per_step),
                     pl.BlockSpec((2 * GROUP_CH, LANES), lambda n, i: (grp, 0), pipeline_mode=pl.Buffered(1))]
        operands += [cache, new_t]
        out_specs.append(pl.BlockSpec((1,) + cache.shape[1:], per_step))
        out_shape.append(jax.ShapeDtypeStruct(cache.shape, F32))
    return pl.pallas_call(
        functools.partial(_ffn_kernel, final=final, chunk=512, mixer_out=mixer_out is not None,
                          shift_steps=shift_steps),
        grid=(nb, t // tm),
        in_specs=in_specs,
        out_specs=out_specs,
        out_shape=out_shape,
        compiler_params=_params(2),
        name="ffn_final" if final else "l0_out_ffn",
    )(*operands)


def _spatial_tril(ws_ref, grp):
    i = lax.broadcasted_iota(jnp.int32, (D_CHUNK, D_CHUNK), 0)
    j = lax.broadcasted_iota(jnp.int32, (D_CHUNK, D_CHUNK), 1)
    return jnp.where(j <= i, ws_ref[grp], 0.0).astype(BF16)


def _in1_prompt_kernel(x_ref, mod_ref, g_ref, w_ref, cw_ref, cb_ref, lg_ref, lb_ref, ws_ref, bs_ref, wo_ref,
                       y_ref, cst_ref, ext_ref, cat_ref, *, tm):
    t = pl.program_id(1)
    last = pl.num_programs(1) - 1
    halo = C_CONV - 1
    pad = 8

    @pl.when(t == 0)
    def _():
        ext_ref[0:pad, :] = jnp.zeros((pad, C_CH), F32)

    x = x_ref[0]
    h = _rms_mod(x, g_ref[...], mod_ref[0, 1], mod_ref[0, 0]).astype(BF16)
    bg, cg, xi, u_in, v_in = (_mm(h, w_ref[:, c0:c0 + C_CH]) for c0 in range(0, O_IN, C_CH))
    ext_ref[pad:pad + tm, :] = cg * xi
    conv = jnp.broadcast_to(cb_ref[...], (tm, C_CH))
    for k in range(C_CONV):
        conv = conv + cw_ref[k:k + 1, :] * ext_ref[pl.ds(pad - halo + k, tm), :]
    cat_ref[:, :C_CH] = (bg * conv).astype(BF16)
    ext_ref[pl.ds(pad - halo, halo), :] = ext_ref[pl.ds(tm + pad - halo, halo), :]
    u = jax.nn.gelu(u_in, approximate=True)
    v = _layernorm(jax.nn.gelu(v_in, approximate=True), lg_ref[...], lb_ref[...]).astype(BF16)
    for grp in range(D_GROUPS):
        ws = _spatial_tril(ws_ref, grp)
        cols = slice(grp * D_GROUP_CH, (grp + 1) * D_GROUP_CH)
        for c in range(tm // D_CHUNK):
            rows = slice(c * D_CHUNK, (c + 1) * D_CHUNK)
            mix = _mm(ws, v[rows, cols]) + bs_ref[:, cols]
            cat_ref[rows, C_CH + grp * D_GROUP_CH:C_CH + (grp + 1) * D_GROUP_CH] = (u[rows, cols] * mix).astype(BF16)
    y_ref[0] = x + mod_ref[0, 2] * _mm(cat_ref[...], wo_ref[...])

    @pl.when(t == last)
    def _():
        cst_ref[0] = ext_ref[pl.ds(pad - halo, halo), :]


def _in1_prompt(x, mod, g, w, cw, cb, lg, lb, ws, bs_full, wo):
    nb, t, _ = x.shape
    tm = ROW_TILE
    row = lambda n, i: (n, i, 0)
    return pl.pallas_call(
        functools.partial(_in1_prompt_kernel, tm=tm),
        grid=(nb, t // tm),
        in_specs=[pl.BlockSpec((1, tm, D_MODEL), row),
                  pl.BlockSpec((1, 6, 1, D_MODEL), lambda n, i: (n, 0, 0, 0)),
                  _const_spec((1, D_MODEL)), _const_spec((D_MODEL, O_IN)),
                  _const_spec((C_CONV, C_CH)), _const_spec((1, C_CH)),
                  _const_spec((1, D_CH)), _const_spec((1, D_CH)),
                  _const_spec((D_GROUPS, D_CHUNK, D_CHUNK)), _const_spec((D_CHUNK, D_CH)),
                  _const_spec((C_CH + D_CH, D_MODEL))],
        out_specs=[pl.BlockSpec((1, tm, D_MODEL), row),
                   pl.BlockSpec((1, C_CONV - 1, C_CH), lambda n, i: (n, 0, 0))],
        out_shape=[jax.ShapeDtypeStruct((nb, t, D_MODEL), F32),
                   jax.ShapeDtypeStruct((nb, C_CONV - 1, C_CH), F32)],
        scratch_shapes=[pltpu.VMEM((tm + 8, C_CH), F32), pltpu.VMEM((tm, C_CH + D_CH), BF16)],
        compiler_params=_params(2),
        name="l1_mixer_prompt",
    )(x, mod, g, w, cw, cb, lg, lb, ws, bs_full, wo)


def _in0_sample_kernel(x_ref, mod_ref, g_ref, w_ref, abuf_ref, cw_ref, cb_ref, lg_ref, lb_ref,
                       a_ref, ast_ref, q_ref, newt_ref, ext_ref, perm_ref, *, nb, steps):
    rows = nb * steps
    hist = (A_CONV - 1) * nb
    h = _rms_mod(x_ref[0], g_ref[...], mod_ref[0, 1], mod_ref[0, 0]).astype(BF16)
    proj = _mm(h, w_ref[...])
    ext_ref[0:hist, :] = abuf_ref[...]
    ext_ref[hist:hist + rows, :] = proj[:, :A_CH] * jax.nn.sigmoid(proj[:, A_CH:2 * A_CH])
    acc = jnp.broadcast_to(cb_ref[...], (rows, A_CH))
    for k in range(A_CONV):
        acc = acc + cw_ref[k:k + 1, :] * ext_ref[k * nb:k * nb + rows, :]
    a_ref[0] = _silu(_layernorm(acc, lg_ref[...], lb_ref[...])).astype(BF16)
    ast_ref[...] = ext_ref[rows:rows + hist, :]
    q_ref[...] = proj[:, Q_OFF:K_OFF]
    tile = 0
    for g in range(N_GROUPS):
        for off in (K_OFF, V_OFF):
            for c in range(GROUP_TILES):
                col = off + g * GROUP_CH + c * LANES
                for s in range(steps):
                    perm_ref[tile, pl.ds(s, nb, stride=steps), :] = proj[s * nb:(s + 1) * nb, col:col + LANES]
                newt_ref[tile * LANES:(tile + 1) * LANES, :] = perm_ref[tile].T
                tile += 1


def _in0_sample(x, mod, g, w, abuf_t, cw, cb, lg, lb, nb, steps):
    rows = nb * steps
    assert rows == LANES
    hist = (A_CONV - 1) * nb
    return pl.pallas_call(
        functools.partial(_in0_sample_kernel, nb=nb, steps=steps),
        grid=(1,),
        in_specs=[_const_spec((1, rows, D_MODEL)), _const_spec((1, 6, rows, D_MODEL)),
                  _const_spec((1, D_MODEL)), _const_spec((D_MODEL, E_IN)), _const_spec((hist, A_CH)),
                  _const_spec((A_CONV, A_CH)), _const_spec((1, A_CH)), _const_spec((1, A_CH)),
                  _const_spec((1, A_CH))],
        out_specs=[_const_spec((1, rows, A_CH)), _const_spec((hist, A_CH)),
                   _const_spec((rows, N_GROUPS * GROUP_CH)), _const_spec((2 * N_GROUPS * GROUP_CH, LANES))],
        out_shape=[jax.ShapeDtypeStruct((1, rows, A_CH), BF16), jax.ShapeDtypeStruct((hist, A_CH), F32),
                   jax.ShapeDtypeStruct((rows, N_GROUPS * GROUP_CH), F32),
                   jax.ShapeDtypeStruct((2 * N_GROUPS * GROUP_CH, LANES), F32)],
        scratch_shapes=[pltpu.VMEM((hist + rows, A_CH), F32),
                        pltpu.VMEM((2 * N_GROUPS * GROUP_TILES, LANES, LANES), F32)],
        compiler_params=_params(1),
        name="l0_in_sample",
    )(x, mod, g, w, abuf_t, cw, cb, lg, lb)


def _new_steps(newt_ref, r0, r1, n, steps):
    return pltpu.roll(newt_ref[r0:r1, :], lax.rem(LANES - n * steps, LANES), axis=1)


def _shift_window(cache_rows, new_rows, steps):
    w = cache_rows.shape[-1]
    comb = jnp.concatenate([cache_rows, new_rows], axis=1)
    return comb, pltpu.roll(comb, w + LANES - steps, axis=1)[:, :w]


def _attn_sample_kernel(q_ref, newt_ref, c0_ref, c1_ref, c2_ref, b0_ref, b1_ref, b2_ref,
                        attn_ref, s0_ref, s1_ref, *, steps):
    n = pl.program_id(0)
    rows = HPG * steps
    row_head = lax.broadcasted_iota(jnp.int32, (rows, GROUP_CH), 0) // steps
    col_head = lax.broadcasted_iota(jnp.int32, (rows, GROUP_CH), 1) // HEAD_DIM
    step_masks = _head_masks(steps)
    outs, lses = [], []
    for g, (c_ref, b_ref, s_ref) in enumerate(zip((c0_ref, c1_ref, c2_ref), (b0_ref, b1_ref, b2_ref),
                                                  (s0_ref, s1_ref, None))):
        new = _new_steps(newt_ref, 2 * g * GROUP_CH, 2 * (g + 1) * GROUP_CH, n, steps)
        comb, shifted = _shift_window(c_ref[0], new, steps)
        if s_ref is not None:
            s_ref[0] = shifted
        comb_bf = comb.astype(BF16)
        q = q_ref[0][:, g * GROUP_CH:(g + 1) * GROUP_CH]
        q_bd = jnp.where(row_head == col_head, jnp.concatenate([q] * HPG, axis=0), 0.0).astype(BF16)
        s = _mm(q_bd, comb_bf[:GROUP_CH]) * (HEAD_DIM ** -0.5) + b_ref[...]
        m = jnp.max(s, axis=-1, keepdims=True)
        e = jnp.exp(s - m)
        l = jnp.sum(e, axis=-1, keepdims=True)
        p = (e * (1.0 / l)).astype(BF16)
        o_all = _mm_nt(p, comb_bf[GROUP_CH:])
        lse_all = jnp.broadcast_to(m + jnp.log(l), (rows, GROUP_CH))
        o = jnp.zeros((steps, GROUP_CH), F32)
        lse = jnp.zeros((steps, GROUP_CH), F32)
        for h in range(HPG):
            o = jnp.where(step_masks[h], o_all[h * steps:(h + 1) * steps], o)
            lse = jnp.where(step_masks[h], lse_all[h * steps:(h + 1) * steps], lse)
        outs.append(o)
        lses.append(lse)
    attn_ref[0] = _mix_groups(outs, lses)


def _attn_sample(q, new_t, caches, biases, steps):
    nb = q.shape[0]
    per_n = lambda n: (n, 0, 0)
    cache_specs = [pl.BlockSpec((1, 2 * GROUP_CH, w), per_n) for w in WINDOWS]
    return pl.pallas_call(
        functools.partial(_attn_sample_kernel, steps=steps),
        grid=(nb,),
        in_specs=[pl.BlockSpec((1, steps, N_GROUPS * GROUP_CH), per_n),
                  _const_spec((2 * N_GROUPS * GROUP_CH, LANES))] + cache_specs
                 + [_const_spec((HPG * steps, w + LANES)) for w in WINDOWS],
        out_specs=[pl.BlockSpec((1, steps, GROUP_CH), per_n)] + cache_specs[:2],
        out_shape=[jax.ShapeDtypeStruct((nb, steps, GROUP_CH), F32)]
                  + [jax.ShapeDtypeStruct((nb, 2 * GROUP_CH, w), F32) for w in WINDOWS[:2]],
        compiler_params=_params(1),
        name="l0_attn_sample",
    )(q, new_t, *caches, *biases)


def _in1_sample_kernel(x_ref, mod_ref, g_ref, w_ref, cbuf_ref, cw_ref, cb_ref, lg_ref, lb_ref, coef_ref, bs_ref,
                       wo_ref, y_ref, cst_ref, dv_ref, ext_ref, *, nb, steps):
    rows = nb * steps
    hist = (C_CONV - 1) * nb
    x = x_ref[0]
    h = _rms_mod(x, g_ref[...], mod_ref[0, 1], mod_ref[0, 0]).astype(BF16)
    proj = _mm(h, w_ref[...])
    ext_ref[0:hist, :] = cbuf_ref[...]
    ext_ref[hist:hist + rows, :] = proj[:, C_CH:2 * C_CH] * proj[:, 2 * C_CH:3 * C_CH]
    conv = jnp.broadcast_to(cb_ref[...], (rows, C_CH))
    for k in range(C_CONV):
        conv = conv + cw_ref[k:k + 1, :] * ext_ref[k * nb:k * nb + rows, :]
    yc = proj[:, :C_CH] * conv
    cst_ref[...] = ext_ref[rows:rows + hist, :]
    uv = jax.nn.gelu(proj[:, 3 * C_CH:], approximate=True)
    v = _layernorm(uv[:, D_CH:], lg_ref[...], lb_ref[...])
    dv_ref[...] = v
    v_r = v.astype(BF16).astype(F32)
    coef = coef_ref[...].astype(BF16).astype(F32)
    mixes = []
    for i in range(steps):
        mix = jnp.broadcast_to(bs_ref[i:i + 1, :], (nb, D_CH))
        for j in range(i + 1):
            mix = mix + coef[i * steps + j:i * steps + j + 1, :] * v_r[j * nb:(j + 1) * nb, :]
        mixes.append(mix)
    yd = uv[:, :D_CH] * jnp.concatenate(mixes, axis=0)
    cat = jnp.concatenate([yc.astype(BF16), yd.astype(BF16)], axis=1)
    y_ref[0] = x + mod_ref[0, 2] * _mm(cat, wo_ref[...])


def _in1_sample(x, mod, g, w, cbuf_t, cw, cb, lg, lb, coef, bs_rows, wo, nb, steps):
    rows = nb * steps
    hist = (C_CONV - 1) * nb
    return pl.pallas_call(
        functools.partial(_in1_sample_kernel, nb=nb, steps=steps),
        grid=(1,),
        in_specs=[_const_spec((1, rows, D_MODEL)), _const_spec((1, 6, rows, D_MODEL)),
                  _const_spec((1, D_MODEL)), _const_spec((D_MODEL, O_IN)), _const_spec((hist, C_CH)),
                  _const_spec((C_CONV, C_CH)), _const_spec((1, C_CH)), _const_spec((1, D_CH)),
                  _const_spec((1, D_CH)), _const_spec((steps * steps, D_CH)), _const_spec((steps, D_CH)),
                  _const_spec((C_CH + D_CH, D_MODEL))],
        out_specs=[_const_spec((1, rows, D_MODEL)), _const_spec((hist, C_CH)), _const_spec((rows, D_CH))],
        out_shape=[jax.ShapeDtypeStruct((1, rows, D_MODEL), F32), jax.ShapeDtypeStruct((hist, C_CH), F32),
                   jax.ShapeDtypeStruct((rows, D_CH), F32)],
        scratch_shapes=[pltpu.VMEM((hist + rows, C_CH), F32)],
        compiler_params=_params(1),
        name="l1_mixer_sample",
    )(x, mod, g, w, cbuf_t, cw, cb, lg, lb, coef, bs_rows, wo)


def _t5_bucket(dist):
    n = np.maximum(np.asarray(dist, dtype=np.int64), 0)
    max_exact = N_BUCKETS // 2
    large = max_exact + (np.log(np.maximum(n, 1) / max_exact) / np.log(MAX_DISTANCE / max_exact)
                         * (N_BUCKETS - max_exact)).astype(np.int32)
    return np.where(n < max_exact, n, np.minimum(large, N_BUCKETS - 1)).astype(np.int32)


def _prompt_bias_base(table, grp):
    rel = Q_BLOCK - np.arange(2 * Q_BLOCK)
    band = (rel >= 0) & (rel <= SUB_WINDOW)
    tab = table[:, grp * HPG:(grp + 1) * HPG]
    bias = tab[_t5_bucket(np.clip(rel, 0, None) * DILATIONS[grp])].T
    return jnp.where(band[None], bias, NEG).astype(F32)


def _sample_bias(table, grp, steps, pad):
    w, d = WINDOWS[grp], DILATIONS[grp]
    delta = w + steps - 1 - np.arange(w + pad + steps - 1)
    valid = (delta >= 0) & (delta % d == 0) & (delta // d <= SUB_WINDOW)
    tab = table[:, grp * HPG:(grp + 1) * HPG]
    vec = jnp.where(valid[None], tab[_t5_bucket(np.clip(delta, 0, None))].T, NEG).astype(F32)
    per_step = [vec[:, steps - 1 - s:steps - 1 - s + w + pad] for s in range(steps)]
    return jnp.stack(per_step, axis=1).reshape(HPG * steps, w + pad)


def _step_major(x):
    return jnp.swapaxes(x, 0, 1).reshape((x.shape[0] * x.shape[1],) + x.shape[2:])


def _batch_major(x, nb, steps):
    return jnp.swapaxes(x.reshape((steps, nb) + x.shape[1:]), 0, 1)


def kernel(x_prompt, x_sample, c_prompt, c_sample, state_a_conv, cache_b_kv0, cache_b_kv1, cache_b_kv2, state_c_conv, rel_bias_table, ada_w, ada_b, norm_mix_g, norm_ffn_g, ffn_w_gate_up, ffn_w_down, final_norm_g, e_w_in, a_conv_w, a_conv_b, a_ln_g, a_ln_b, e_w_out, o_w_in, c_conv_w, c_conv_b, d_ln_g, d_ln_b, d_spatial_w, d_spatial_b, o_w_out):
    nb_p = x_prompt.shape[0]
    nb_s, steps, _ = x_sample.shape
    pad = LANES

    w_in0 = e_w_in[0].astype(BF16)
    w_out0 = e_w_out[0].astype(BF16)
    w_in1 = o_w_in[0].astype(BF16)
    w_out1 = o_w_out[0].astype(BF16)
    w_gu = ffn_w_gate_up.astype(BF16)
    w_dn = ffn_w_down.astype(BF16)
    fin_g = final_norm_g.reshape(1, D_MODEL)

    mod = _ada(jnp.concatenate([c_prompt, c_sample], axis=0), ada_w, ada_b)
    mod = mod.reshape(mod.shape[0], nb_p + nb_s, 6, D_MODEL)
    mod_p = [mod[l, :nb_p].reshape(nb_p, 6, 1, D_MODEL) for l in range(2)]
    mod_s = [jnp.swapaxes(jnp.tile(mod[l, nb_p:], (steps, 1, 1)), 0, 1)[None] for l in range(2)]

    (a, a_st_p, q0, q1, q2, kv0, kv1, kv2, st0, st1, st2) = _in0_prompt(
        x_prompt, mod_p[0], norm_mix_g[0:1], w_in0, a_conv_w[0], a_conv_b, a_ln_g, a_ln_b)
    base = jnp.stack([_prompt_bias_base(rel_bias_table, grp) for grp in range(N_GROUPS)])
    attn = _attn_prompt((q0, q1, q2), (kv0, kv1, kv2), base)
    xs = _step_major(x_sample)[None]
    abuf_t = _step_major(state_a_conv[0])
    a_s, a_st_s, q_s, new_t = _in0_sample(xs, mod_s[0], norm_mix_g[0:1], w_in0, abuf_t, a_conv_w[0], a_conv_b,
                                          a_ln_g, a_ln_b, nb_s, steps)
    caches = [jnp.transpose(c[0], (0, 2, 3, 4, 1)).reshape(nb_s, 2 * GROUP_CH, c.shape[2])
              for c in (cache_b_kv0, cache_b_kv1, cache_b_kv2)]
    xp, s2 = _ffn(x_prompt, mod_p[0], norm_ffn_g[0:1], w_gu, w_dn, fin_g, 0, False, (a, attn, w_out0),
                  (caches[2], new_t, 2, steps))
    bs_full = jnp.repeat(d_spatial_b[0].T, D_GROUP_CH, axis=1)
    xp, c_st_p = _in1_prompt(xp, mod_p[1], norm_mix_g[1:2], w_in1, c_conv_w[0], c_conv_b, d_ln_g, d_ln_b,
                             d_spatial_w[0], bs_full, w_out1)
    (y_prompt,) = _ffn(xp, mod_p[1], norm_ffn_g[1:2], w_gu, w_dn, fin_g, 1, True)

    q_b = _batch_major(q_s, nb_s, steps)
    biases = [_sample_bias(rel_bias_table, grp, steps, pad) for grp in range(N_GROUPS)]
    attn_s, s0, s1 = _attn_sample(q_b, new_t, caches, biases, steps)
    attn_s = _step_major(attn_s).astype(BF16)[None]
    (xs,) = _ffn(xs, mod_s[0], norm_ffn_g[0:1], w_gu, w_dn, fin_g, 0, False, (a_s, attn_s, w_out0))
    cbuf_t = _step_major(state_c_conv[0])
    coef = jnp.repeat(jnp.transpose(d_spatial_w[0][:, :steps, :steps], (1, 2, 0)).reshape(steps * steps, D_GROUPS),
                      D_GROUP_CH, axis=1)
    xs, c_st_s, dv_s = _in1_sample(xs, mod_s[1], norm_mix_g[1:2], w_in1, cbuf_t, c_conv_w[0], c_conv_b, d_ln_g,
                                   d_ln_b, coef, bs_full[:steps], w_out1, nb_s, steps)
    (y_sample,) = _ffn(xs, mod_s[1], norm_ffn_g[1:2], w_gu, w_dn, fin_g, 1, True)

    def kv_state_prompt(st):
        return st.reshape(1, nb_p, st.shape[1], 2, HPG, HEAD_DIM)

    def kv_state_sample(st):
        return jnp.transpose(st.reshape(nb_s, 2, HPG, HEAD_DIM, st.shape[-1]), (0, 4, 1, 2, 3))[None]

    return (y_prompt, _batch_major(y_sample[0], nb_s, steps),
            a_st_p[None], _batch_major(a_st_s, nb_s, A_CONV - 1)[None],
            kv_state_prompt(st0), kv_state_sample(s0), kv_state_prompt(st1), kv_state_sample(s1),
            kv_state_prompt(st2), kv_state_sample(s2),
            c_st_p[None], _batch_major(c_st_s, nb_s, C_CONV - 1)[None],
            _batch_major(dv_s, nb_s, steps)[None])
```

```python
import functools

import numpy as np
import jax
import jax.numpy as jnp
from jax import lax
from jax.experimental import pallas as pl
from jax.experimental.pallas import tpu as pltpu

F32 = jnp.float32
BF16 = jnp.bfloat16

D_MODEL = 1024
EPS = 1e-6
A_CH = 256
A_CONV = 31
HEAD_DIM = 64
HPG = 4
GROUP_CH = HPG * HEAD_DIM
WINDOWS = (128, 512, 2048)
DILATIONS = (1, 4, 16)
N_GROUPS = 3
SUB_WINDOW = 128
Q_BLOCK = 128
N_BUCKETS = 32
MAX_DISTANCE = 2048
C_CH = 512
C_CONV = 3
D_CH = 512
D_GROUPS = 4
D_GROUP_CH = D_CH // D_GROUPS
D_CHUNK = 128
FFN_HIDDEN = 2816
Q_OFF = 2 * A_CH
K_OFF = Q_OFF + N_GROUPS * GROUP_CH
V_OFF = K_OFF + N_GROUPS * GROUP_CH
E_IN = V_OFF + N_GROUPS * GROUP_CH
O_IN = 3 * C_CH + 2 * D_CH
LANES = 128
SUBLANES = 8
GROUP_TILES = GROUP_CH // LANES
NEG = -1e30
ROW_TILE = 512
FFN_ROW_TILE = 512
CONV_ROWS = 64
MIX_ROWS = 256
SHIFT_ROWS = 64
VMEM_LIMIT = 56 * 1024 * 1024


def _params(n_axes):
    return pltpu.CompilerParams(dimension_semantics=("arbitrary",) * n_axes, vmem_limit_bytes=VMEM_LIMIT)


def _const_spec(shape):
    return pl.BlockSpec(shape, lambda *_: (0,) * len(shape), pipeline_mode=pl.Buffered(1))


def _layer_spec(shape, layer):
    return pl.BlockSpec((1,) + shape, lambda *_: (layer,) + (0,) * len(shape), pipeline_mode=pl.Buffered(1))


def _rms_mod(x, g, scale, shift):
    return (x * lax.rsqrt(jnp.mean(x * x, axis=-1, keepdims=True) + EPS)) * (g * (1.0 + scale)) + shift


def _layernorm(x, g, b):
    mu = jnp.mean(x, axis=-1, keepdims=True)
    xc = x - mu
    var = jnp.mean(xc * xc, axis=-1, keepdims=True)
    return xc * lax.rsqrt(var + EPS) * g + b


def _silu(x):
    return x * jax.nn.sigmoid(x)


def _mm(a, b):
    return jnp.dot(a, b, preferred_element_type=F32)


def _mm_nt(a, b):
    return lax.dot_general(a, b, (((1,), (1,)), ((), ())), preferred_element_type=F32)


def _group_cols(proj, off, g):
    return proj[:, off + g * GROUP_CH:off + (g + 1) * GROUP_CH]


def _ada_kernel(c_ref, w_ref, b_ref, o_ref):
    cs = _silu(c_ref[...]).astype(BF16)
    o_ref[0] = _mm(cs, w_ref[0].astype(BF16)) + b_ref[0]


def _ada(c_all, ada_w, ada_b):
    depth, _, width = ada_w.shape
    nb = c_all.shape[0]
    tn = 1536
    return pl.pallas_call(
        _ada_kernel,
        grid=(depth, width // tn),
        in_specs=[_const_spec((nb, D_MODEL)),
                  pl.BlockSpec((1, D_MODEL, tn), lambda l, j: (l, 0, j)),
                  pl.BlockSpec((1, 1, tn), lambda l, j: (l, 0, j))],
        out_specs=pl.BlockSpec((1, nb, tn), lambda l, j: (l, 0, j)),
        out_shape=jax.ShapeDtypeStruct((depth, nb, width), F32),
        compiler_params=_params(2),
        name="ada_mod",
    )(c_all, ada_w, ada_b.reshape(depth, 1, width))


def _ordering_zero(x):
    bits = lax.shift_right_logical(lax.bitcast_convert_type(x, jnp.uint32), jnp.uint32(16))
    return lax.bitcast_convert_type(lax.shift_right_logical(bits, jnp.uint32(16)), F32)


def _conv_ln_silu(ext_ref, shift_ref, base, rows, wb_ref, cb_ref, lg_ref, lb_ref, taps, after):
    ch = cb_ref.shape[-1]
    zero = _ordering_zero(after[-SUBLANES:, :ch])
    acc = jnp.broadcast_to(cb_ref[...], (rows // SUBLANES, SUBLANES, ch))
    for k in range(taps):
        m = (base + k) % SUBLANES
        src = ext_ref if m == 0 else shift_ref.at[m - 1]
        x = src[pl.ds(base + k - m, rows), :].reshape(rows // SUBLANES, SUBLANES, ch)
        acc = acc + (wb_ref[k] + zero)[None] * x
    return _silu(_layernorm(acc.reshape(rows, ch), lg_ref[...], lb_ref[...]))


def _in0_prompt_kernel(x_ref, mod_ref, g_ref, w_ref, cw_ref, cb_ref, lg_ref, lb_ref,
                       a_ref, ast_ref, q0_ref, q1_ref, q2_ref, kv0_ref, kv1_ref, kv2_ref,
                       st0_ref, st1_ref, st2_ref, ext_ref, shift_ref, perm_ref, wb_ref, *, tm):
    t = pl.program_id(1)
    last = pl.num_programs(1) - 1
    halo = A_CONV - 1
    pad = 32

    @pl.when(t == 0)
    def _():
        ext_ref[0:pad, :] = jnp.zeros((pad, A_CH), F32)
        for k in range(A_CONV):
            wb_ref[k] = jnp.broadcast_to(cw_ref[k:k + 1, :], (SUBLANES, A_CH))

    h = _rms_mod(x_ref[0], g_ref[...], mod_ref[0, 1], mod_ref[0, 0]).astype(BF16)
    a_in = _mm(h, w_ref[:, :Q_OFF])
    ext_ref[pad:pad + tm, :] = a_in[:, :A_CH] * jax.nn.sigmoid(a_in[:, A_CH:])
    for m in range(1, SUBLANES):
        shift_ref[m - 1] = ext_ref[pl.ds(m, tm + pad - SUBLANES), :]

    def conv_chunk(c, after):
        r0 = c * CONV_ROWS
        y = _conv_ln_silu(ext_ref, shift_ref, pad - halo + r0, CONV_ROWS, wb_ref, cb_ref, lg_ref, lb_ref, A_CONV,
                          after)
        a_ref[0, r0:r0 + CONV_ROWS, :] = y.astype(BF16)

    def project(g, off, dst, col0, slot0):
        d = DILATIONS[g]
        val = _mm(h, w_ref[:, off + g * GROUP_CH:off + (g + 1) * GROUP_CH])
        if off != Q_OFF:
            st_refs[g][0, :, col0:col0 + GROUP_CH] = val[tm - min(WINDOWS[g], tm):]
        if d == 1:
            dst[0, 0, :, col0:col0 + GROUP_CH] = val.astype(BF16)
            return val
        for c in range(GROUP_TILES):
            tile = perm_ref.at[slot0 + c]
            tile[...] = val[:, c * LANES:(c + 1) * LANES]
            for r in range(d):
                dst[0, r, :, col0 + c * LANES:col0 + (c + 1) * LANES] = (
                    tile[pl.ds(r, tm // d, stride=d), :].astype(BF16))
        return val

    q_refs, kv_refs, st_refs = (q0_ref, q1_ref, q2_ref), (kv0_ref, kv1_ref, kv2_ref), (st0_ref, st1_ref, st2_ref)
    tasks = [(g, off, (q_refs if off == Q_OFF else kv_refs)[g], GROUP_CH if off == V_OFF else 0)
             for g in range(N_GROUPS) for off in (Q_OFF, K_OFF, V_OFF)]
    n_chunks = tm // CONV_ROWS
    for i, (g, off, dst, col0) in enumerate(tasks):
        val = project(g, off, dst, col0, i * GROUP_TILES)
        if i < n_chunks:
            conv_chunk(i, val)
    for c in range(len(tasks), n_chunks):
        conv_chunk(c, val)
    ext_ref[pl.ds(pad - halo, halo), :] = ext_ref[pl.ds(tm + pad - halo, halo), :]

    @pl.when(t == last)
    def _():
        ast_ref[0] = ext_ref[pl.ds(pad - halo, halo), :]


def _in0_prompt(x, mod, g, w, cw, cb, lg, lb):
    nb, t, _ = x.shape
    tm = ROW_TILE
    assert t % tm == 0 and tm >= WINDOWS[1] and WINDOWS[2] == t
    row = lambda n, i: (n, i, 0)
    per_n = lambda n, i: (n, 0, 0)
    res = lambda n, i: (n, 0, i, 0)
    out_shape = [jax.ShapeDtypeStruct((nb, t, A_CH), BF16), jax.ShapeDtypeStruct((nb, A_CONV - 1, A_CH), F32)]
    out_specs = [pl.BlockSpec((1, tm, A_CH), row), pl.BlockSpec((1, A_CONV - 1, A_CH), per_n)]
    for width in (GROUP_CH, 2 * GROUP_CH):
        for d in DILATIONS:
            out_shape.append(jax.ShapeDtypeStruct((nb, d, t // d, width), BF16))
            out_specs.append(pl.BlockSpec((1, d, tm // d, width), res))
    out_shape += [jax.ShapeDtypeStruct((nb, WINDOWS[0], 2 * GROUP_CH), F32),
                  jax.ShapeDtypeStruct((nb, WINDOWS[1], 2 * GROUP_CH), F32),
                  jax.ShapeDtypeStruct((nb, t, 2 * GROUP_CH), F32)]
    out_specs += [pl.BlockSpec((1, WINDOWS[0], 2 * GROUP_CH), per_n),
                  pl.BlockSpec((1, WINDOWS[1], 2 * GROUP_CH), per_n),
                  pl.BlockSpec((1, tm, 2 * GROUP_CH), row)]
    return pl.pallas_call(
        functools.partial(_in0_prompt_kernel, tm=tm),
        grid=(nb, t // tm),
        in_specs=[pl.BlockSpec((1, tm, D_MODEL), row),
                  pl.BlockSpec((1, 6, 1, D_MODEL), lambda n, i: (n, 0, 0, 0)),
                  _const_spec((1, D_MODEL)), _const_spec((D_MODEL, E_IN)),
                  _const_spec((A_CONV, A_CH)), _const_spec((1, A_CH)),
                  _const_spec((1, A_CH)), _const_spec((1, A_CH))],
        out_specs=out_specs,
        out_shape=out_shape,
        scratch_shapes=[pltpu.VMEM((tm + 32, A_CH), F32), pltpu.VMEM((SUBLANES - 1, tm + 32 - SUBLANES, A_CH), F32),
                        pltpu.VMEM((3 * N_GROUPS * GROUP_TILES, tm, LANES), F32),
                        pltpu.VMEM((A_CONV, SUBLANES, A_CH), F32)],
        compiler_params=_params(2),
        name="l0_in_prompt",
    )(x, mod, g, w, cw, cb, lg, lb)


def _head_masks(rows):
    col_head = lax.broadcasted_iota(jnp.int32, (rows, GROUP_CH), 1) // HEAD_DIM
    return [col_head == h for h in range(HPG)]


def _mix_groups(outs, lses):
    m = jnp.maximum(jnp.maximum(lses[0], lses[1]), lses[2])
    es = [jnp.exp(l - m) for l in lses]
    inv = 1.0 / (es[0] + es[1] + es[2])
    return (es[0] * inv) * outs[0] + (es[1] * inv) * outs[1] + (es[2] * inv) * outs[2]


def _attn_prompt_kernel(q0_ref, kv0_ref, q1_ref, kv1_ref, q2_ref, kv2_ref, base_ref, attn_ref,
                        bias_ref, o_ref, lse_ref, *, t):
    masks = _head_masks(Q_BLOCK)
    masks_bf = [jnp.where(m, HEAD_DIM ** -0.5, 0.0).astype(BF16) for m in masks]
    key_idx = lax.broadcasted_iota(jnp.int32, (HPG * Q_BLOCK, 2 * Q_BLOCK), 1)

    @pl.when(pl.program_id(0) == 0)
    def _():
        for g in range(N_GROUPS):
            for h in range(HPG):
                row = jnp.broadcast_to(base_ref[g, h:h + 1, :], (Q_BLOCK, 2 * Q_BLOCK))
                bias_ref[g, h * Q_BLOCK:(h + 1) * Q_BLOCK, :] = pltpu.roll(row, 0, 1, stride=1, stride_axis=0)

    for g, (d, q_ref, kv_ref) in enumerate(zip(DILATIONS, (q0_ref, q1_ref, q2_ref), (kv0_ref, kv1_ref, kv2_ref))):
        nblk = t // d // Q_BLOCK
        single = nblk == 1

        def block(j, carry, g=g, d=d, q_ref=q_ref, kv_ref=kv_ref, nblk=nblk, single=single):
            r = j // nblk
            i = j % nblk
            rows = pl.ds(pl.multiple_of(i * Q_BLOCK, Q_BLOCK), Q_BLOCK)
            q = q_ref[0, r, rows, :]
            qs = jnp.concatenate([q * mb for mb in masks_bf], axis=0)
            if single:
                kk = kv_ref[0, r, rows, :]
                bias = bias_ref[g, :, Q_BLOCK:]
            else:
                prev = pl.ds(pl.multiple_of(jnp.maximum(i - 1, 0) * Q_BLOCK, Q_BLOCK), Q_BLOCK)
                kk = jnp.concatenate([kv_ref[0, r, prev, :], kv_ref[0, r, rows, :]], axis=0)
                bias = bias_ref[g] + jnp.where(key_idx < jnp.where(i == 0, Q_BLOCK, 0), NEG, 0.0)
            s = _mm_nt(qs, kk[:, :GROUP_CH]) + bias
            m = jnp.max(s, axis=-1, keepdims=True)
            e = jnp.exp(s - m)
            l = jnp.sum(e, axis=-1, keepdims=True)
            o_all = _mm((e * (1.0 / l)).astype(BF16), kk[:, GROUP_CH:])
            lse_all = jnp.broadcast_to(m + jnp.log(l), (HPG * Q_BLOCK, GROUP_CH))
            o = o_all[:Q_BLOCK]
            lse = lse_all[:Q_BLOCK]
            for h in range(1, HPG):
                o = jnp.where(masks[h], o_all[h * Q_BLOCK:(h + 1) * Q_BLOCK], o)
                lse = jnp.where(masks[h], lse_all[h * Q_BLOCK:(h + 1) * Q_BLOCK], lse)
            tokens = rows if d == 1 else pl.ds(r + i * (Q_BLOCK * d), Q_BLOCK, stride=d)
            for c in range(GROUP_TILES):
                o_ref[g, c, tokens, :] = o[:, c * LANES:(c + 1) * LANES]
                lse_ref[g, c, tokens, :] = lse[:, c * LANES:(c + 1) * LANES]
            return carry

        lax.fori_loop(0, d * nblk, block, 0, unroll=16)

    for r0 in range(0, t, MIX_ROWS):
        for c in range(GROUP_TILES):
            outs = [o_ref[g, c, r0:r0 + MIX_ROWS, :] for g in range(N_GROUPS)]
            lses = [lse_ref[g, c, r0:r0 + MIX_ROWS, :] for g in range(N_GROUPS)]
            attn_ref[0, r0:r0 + MIX_ROWS, c * LANES:(c + 1) * LANES] = _mix_groups(outs, lses).astype(BF16)


def _attn_prompt(qs, kvs, base):
    nb, _, t, _ = qs[0].shape
    blk = lambda n: (n, 0, 0, 0)
    in_specs = []
    for d in DILATIONS:
        in_specs += [pl.BlockSpec((1, d, t // d, GROUP_CH), blk), pl.BlockSpec((1, d, t // d, 2 * GROUP_CH), blk)]
    operands = [x for pair in zip(qs, kvs) for x in pair]
    return pl.pallas_call(
        functools.partial(_attn_prompt_kernel, t=t),
        grid=(nb,),
        in_specs=in_specs + [_const_spec((N_GROUPS, HPG, 2 * Q_BLOCK))],
        out_specs=pl.BlockSpec((1, t, GROUP_CH), lambda n: (n, 0, 0)),
        out_shape=jax.ShapeDtypeStruct((nb, t, GROUP_CH), BF16),
        scratch_shapes=[pltpu.VMEM((N_GROUPS, HPG * Q_BLOCK, 2 * Q_BLOCK), F32),
                        pltpu.VMEM((N_GROUPS, GROUP_TILES, t, LANES), F32),
                        pltpu.VMEM((N_GROUPS, GROUP_TILES, t, LANES), F32)],
        compiler_params=_params(1),
        name="l0_attn_prompt",
    )(*operands, base)


def _ffn_kernel(x_ref, mod_ref, g_ref, wgu_ref, wd_ref, fg_ref, *rest, final, chunk, mixer_out, n_shift,
                shift_steps):
    rest = list(rest)
    state_refs = [rest.pop() for _ in range(n_shift)][::-1]
    y_ref = rest.pop()
    shift_in = [rest.pop() for _ in range(2 * n_shift)][::-1]
    x = x_ref[0]
    step = pl.program_id(0) * pl.num_programs(1) + pl.program_id(1)
    for k, state_ref in enumerate(state_refs):
        cache_ref, newt_ref = shift_in[2 * k], shift_in[2 * k + 1]
        for r0 in range(0, 2 * GROUP_CH, SHIFT_ROWS):
            new = _new_steps(newt_ref, r0, r0 + SHIFT_ROWS, step, shift_steps)
            state_ref[0, r0:r0 + SHIFT_ROWS, :] = _shift_window(cache_ref[0, r0:r0 + SHIFT_ROWS, :], new,
                                                                shift_steps)[1]
    if mixer_out:
        a_ref, attn_ref, wo_ref = rest
        x = x + mod_ref[0, 2] * _mm(jnp.concatenate([a_ref[0], attn_ref[0]], axis=1), wo_ref[...])
    h = _rms_mod(x, g_ref[...], mod_ref[0, 4], mod_ref[0, 3]).astype(BF16)
    acc = jnp.zeros(x.shape, F32)
    for c0 in range(0, FFN_HIDDEN, chunk):
        c1 = min(c0 + chunk, FFN_HIDDEN)
        gate = _mm(h, wgu_ref[0, :, c0:c1])
        up = _mm(h, wgu_ref[0, :, FFN_HIDDEN + c0:FFN_HIDDEN + c1])
        acc = acc + _mm((_silu(gate) * up).astype(BF16), wd_ref[0, c0:c1, :])
    y = x + mod_ref[0, 5] * acc
    if final:
        y = y * lax.rsqrt(jnp.mean(y * y, axis=-1, keepdims=True) + EPS) * fg_ref[...]
    y_ref[0] = y


def _ffn(x, mod, g, wgu, wd, final_g, layer, final, mixer_out=None, shift=None):
    nb, t, _ = x.shape
    tm = min(FFN_ROW_TILE, t)
    rm = mod.shape[2]
    row = lambda n, i: (n, i, 0)
    mod_map = (lambda n, i: (n, 0, 0, 0)) if rm == 1 else (lambda n, i: (n, 0, i, 0))
    in_specs = [pl.BlockSpec((1, tm, D_MODEL), row),
                pl.BlockSpec((1, 6, min(rm, tm), D_MODEL), mod_map),
                _const_spec((1, D_MODEL)),
                _layer_spec((D_MODEL, 2 * FFN_HIDDEN), layer), _layer_spec((FFN_HIDDEN, D_MODEL), layer),
                _const_spec((1, D_MODEL))]
    operands = [x, mod, g, wgu, wd, final_g]
    if mixer_out is not None:
        a, attn, wo = mixer_out
        in_specs += [pl.BlockSpec((1, tm, A_CH), row), pl.BlockSpec((1, tm, GROUP_CH), row),
                     _const_spec((A_CH + GROUP_CH, D_MODEL))]
        operands += [a, attn, wo]
    out_specs = [pl.BlockSpec((1, tm, D_MODEL), row)]
    out_shape = [jax.ShapeDtypeStruct((nb, t, D_MODEL), F32)]
    shift_groups, shift_steps = (), 0
    if shift is not None:
        shift_groups, new_t, shift_steps = shift
        n_tiles = t // tm
        per_step = lambda n, i: (n * n_tiles + i, 0, 0)
        for cache, grp in shift_groups:
            assert cache.shape[0] == nb * n_tiles
            in_specs += [pl.BlockSpec((1,) + cache.shape[1:], per_step),
                         pl.BlockSpec((2 * GROUP_CH, LANES), lambda n, i, grp=grp: (grp, 0),
                                      pipeline_mode=pl.Buffered(1))]
            operands += [cache, new_t]
            out_specs.append(pl.BlockSpec((1,) + cache.shape[1:], per_step))
            out_shape.append(jax.ShapeDtypeStruct(cache.shape, F32))
    return pl.pallas_call(
        functools.partial(_ffn_kernel, final=final, chunk=512, mixer_out=mixer_out is not None,
                          n_shift=len(shift_groups), shift_steps=shift_steps),
        grid=(nb, t // tm),
        in_specs=in_specs,
        out_specs=out_specs,
        out_shape=out_shape,
        compiler_params=_params(2),
        name="ffn_final" if final else "l0_out_ffn",
    )(*operands)


def _spatial_tril(ws_ref, grp):
    i = lax.broadcasted_iota(jnp.int32, (D_CHUNK, D_CHUNK), 0)
    j = lax.broadcasted_iota(jnp.int32, (D_CHUNK, D_CHUNK), 1)
    return jnp.where(j <= i, ws_ref[grp], 0.0).astype(BF16)


def _in1_prompt_kernel(x_ref, mod_ref, g_ref, w_ref, cw_ref, cb_ref, lg_ref, lb_ref, ws_ref, bs_ref, wo_ref,
                       y_ref, cst_ref, ext_ref, cat_ref, *, tm):
    t = pl.program_id(1)
    last = pl.num_programs(1) - 1
    halo = C_CONV - 1
    pad = 8

    @pl.when(t == 0)
    def _():
        ext_ref[0:pad, :] = jnp.zeros((pad, C_CH), F32)

    x = x_ref[0]
    h = _rms_mod(x, g_ref[...], mod_ref[0, 1], mod_ref[0, 0]).astype(BF16)
    bg, cg, xi, u_in, v_in = (_mm(h, w_ref[:, c0:c0 + C_CH]) for c0 in range(0, O_IN, C_CH))
    ext_ref[pad:pad + tm, :] = cg * xi
    conv = jnp.broadcast_to(cb_ref[...], (tm, C_CH))
    for k in range(C_CONV):
        conv = conv + cw_ref[k:k + 1, :] * ext_ref[pl.ds(pad - halo + k, tm), :]
    cat_ref[:, :C_CH] = (bg * conv).astype(BF16)
    ext_ref[pl.ds(pad - halo, halo), :] = ext_ref[pl.ds(tm + pad - halo, halo), :]
    u = jax.nn.gelu(u_in, approximate=True)
    v = _layernorm(jax.nn.gelu(v_in, approximate=True), lg_ref[...], lb_ref[...]).astype(BF16)
    for grp in range(D_GROUPS):
        ws = _spatial_tril(ws_ref, grp)
        cols = slice(grp * D_GROUP_CH, (grp + 1) * D_GROUP_CH)
        for c in range(tm // D_CHUNK):
            rows = slice(c * D_CHUNK, (c + 1) * D_CHUNK)
            mix = _mm(ws, v[rows, cols]) + bs_ref[:, cols]
            cat_ref[rows, C_CH + grp * D_GROUP_CH:C_CH + (grp + 1) * D_GROUP_CH] = (u[rows, cols] * mix).astype(BF16)
    y_ref[0] = x + mod_ref[0, 2] * _mm(cat_ref[...], wo_ref[...])

    @pl.when(t == last)
    def _():
        cst_ref[0] = ext_ref[pl.ds(pad - halo, halo), :]


def _in1_prompt(x, mod, g, w, cw, cb, lg, lb, ws, bs_full, wo):
    nb, t, _ = x.shape
    tm = ROW_TILE
    row = lambda n, i: (n, i, 0)
    return pl.pallas_call(
        functools.partial(_in1_prompt_kernel, tm=tm),
        grid=(nb, t // tm),
        in_specs=[pl.BlockSpec((1, tm, D_MODEL), row),
                  pl.BlockSpec((1, 6, 1, D_MODEL), lambda n, i: (n, 0, 0, 0)),
                  _const_spec((1, D_MODEL)), _const_spec((D_MODEL, O_IN)),
                  _const_spec((C_CONV, C_CH)), _const_spec((1, C_CH)),
                  _const_spec((1, D_CH)), _const_spec((1, D_CH)),
                  _const_spec((D_GROUPS, D_CHUNK, D_CHUNK)), _const_spec((D_CHUNK, D_CH)),
                  _const_spec((C_CH + D_CH, D_MODEL))],
        out_specs=[pl.BlockSpec((1, tm, D_MODEL), row),
                   pl.BlockSpec((1, C_CONV - 1, C_CH), lambda n, i: (n, 0, 0))],
        out_shape=[jax.ShapeDtypeStruct((nb, t, D_MODEL), F32),
                   jax.ShapeDtypeStruct((nb, C_CONV - 1, C_CH), F32)],
        scratch_shapes=[pltpu.VMEM((tm + 8, C_CH), F32), pltpu.VMEM((tm, C_CH + D_CH), BF16)],
        compiler_params=_params(2),
        name="l1_mixer_prompt",
    )(x, mod, g, w, cw, cb, lg, lb, ws, bs_full, wo)


def _in0_sample_kernel(x_ref, mod_ref, g_ref, w_ref, abuf_ref, cw_ref, cb_ref, lg_ref, lb_ref,
                       a_ref, ast_ref, q_ref, newt_ref, ext_ref, perm_ref, *, nb, steps):
    rows = nb * steps
    hist = (A_CONV - 1) * nb
    h = _rms_mod(x_ref[0], g_ref[...], mod_ref[0, 1], mod_ref[0, 0]).astype(BF16)
    proj = _mm(h, w_ref[...])
    ext_ref[0:hist, :] = abuf_ref[...]
    ext_ref[hist:hist + rows, :] = proj[:, :A_CH] * jax.nn.sigmoid(proj[:, A_CH:2 * A_CH])
    acc = jnp.broadcast_to(cb_ref[...], (rows, A_CH))
    for k in range(A_CONV):
        acc = acc + cw_ref[k:k + 1, :] * ext_ref[k * nb:k * nb + rows, :]
    a_ref[0] = _silu(_layernorm(acc, lg_ref[...], lb_ref[...])).astype(BF16)
    ast_ref[...] = ext_ref[rows:rows + hist, :]
    q_ref[...] = proj[:, Q_OFF:K_OFF]
    tile = 0
    for g in range(N_GROUPS):
        for off in (K_OFF, V_OFF):
            for c in range(GROUP_TILES):
                col = off + g * GROUP_CH + c * LANES
                for s in range(steps):
                    perm_ref[tile, pl.ds(s, nb, stride=steps), :] = proj[s * nb:(s + 1) * nb, col:col + LANES]
                newt_ref[tile * LANES:(tile + 1) * LANES, :] = perm_ref[tile].T
                tile += 1


def _in0_sample(x, mod, g, w, abuf_t, cw, cb, lg, lb, nb, steps):
    rows = nb * steps
    assert rows == LANES
    hist = (A_CONV - 1) * nb
    return pl.pallas_call(
        functools.partial(_in0_sample_kernel, nb=nb, steps=steps),
        grid=(1,),
        in_specs=[_const_spec((1, rows, D_MODEL)), _const_spec((1, 6, rows, D_MODEL)),
                  _const_spec((1, D_MODEL)), _const_spec((D_MODEL, E_IN)), _const_spec((hist, A_CH)),
                  _const_spec((A_CONV, A_CH)), _const_spec((1, A_CH)), _const_spec((1, A_CH)),
                  _const_spec((1, A_CH))],
        out_specs=[_const_spec((1, rows, A_CH)), _const_spec((hist, A_CH)),
                   _const_spec((rows, N_GROUPS * GROUP_CH)), _const_spec((2 * N_GROUPS * GROUP_CH, LANES))],
        out_shape=[jax.ShapeDtypeStruct((1, rows, A_CH), BF16), jax.ShapeDtypeStruct((hist, A_CH), F32),
                   jax.ShapeDtypeStruct((rows, N_GROUPS * GROUP_CH), F32),
                   jax.ShapeDtypeStruct((2 * N_GROUPS * GROUP_CH, LANES), F32)],
        scratch_shapes=[pltpu.VMEM((hist + rows, A_CH), F32),
                        pltpu.VMEM((2 * N_GROUPS * GROUP_TILES, LANES, LANES), F32)],
        compiler_params=_params(1),
        name="l0_in_sample",
    )(x, mod, g, w, abuf_t, cw, cb, lg, lb)


def _new_steps(newt_ref, r0, r1, n, steps):
    return pltpu.roll(newt_ref[r0:r1, :], lax.rem(LANES - n * steps, LANES), axis=1)


def _shift_window(cache_rows, new_rows, steps):
    w = cache_rows.shape[-1]
    comb = jnp.concatenate([cache_rows, new_rows], axis=1)
    return comb, pltpu.roll(comb, w + LANES - steps, axis=1)[:, :w]


def _attn_sample_kernel(q_ref, newt_ref, c0_ref, c1_ref, c2_ref, b0_ref, b1_ref, b2_ref,
                        attn_ref, *, steps):
    n = pl.program_id(0)
    rows = HPG * steps
    row_head = lax.broadcasted_iota(jnp.int32, (rows, GROUP_CH), 0) // steps
    col_head = lax.broadcasted_iota(jnp.int32, (rows, GROUP_CH), 1) // HEAD_DIM
    step_masks = _head_masks(steps)
    outs, lses = [], []
    for g, (c_ref, b_ref) in enumerate(zip((c0_ref, c1_ref, c2_ref), (b0_ref, b1_ref, b2_ref))):
        new = _new_steps(newt_ref, 2 * g * GROUP_CH, 2 * (g + 1) * GROUP_CH, n, steps)
        comb_bf = jnp.concatenate([c_ref[0], new], axis=1).astype(BF16)
        q = q_ref[0][:, g * GROUP_CH:(g + 1) * GROUP_CH]
        q_bd = jnp.where(row_head == col_head, jnp.concatenate([q] * HPG, axis=0), 0.0).astype(BF16)
        s = _mm(q_bd, comb_bf[:GROUP_CH]) * (HEAD_DIM ** -0.5) + b_ref[...]
        m = jnp.max(s, axis=-1, keepdims=True)
        e = jnp.exp(s - m)
        l = jnp.sum(e, axis=-1, keepdims=True)
        p = (e * (1.0 / l)).astype(BF16)
        o_all = _mm_nt(p, comb_bf[GROUP_CH:])
        lse_all = jnp.broadcast_to(m + jnp.log(l), (rows, GROUP_CH))
        o = jnp.zeros((steps, GROUP_CH), F32)
        lse = jnp.zeros((steps, GROUP_CH), F32)
        for h in range(HPG):
            o = jnp.where(step_masks[h], o_all[h * steps:(h + 1) * steps], o)
            lse = jnp.where(step_masks[h], lse_all[h * steps:(h + 1) * steps], lse)
        outs.append(o)
        lses.append(lse)
    attn_ref[0] = _mix_groups(outs, lses)


def _attn_sample(q, new_t, caches, biases, steps):
    nb = q.shape[0]
    per_n = lambda n: (n, 0, 0)
    cache_specs = [pl.BlockSpec((1, 2 * GROUP_CH, w), per_n) for w in WINDOWS]
    return pl.pallas_call(
        functools.partial(_attn_sample_kernel, steps=steps),
        grid=(nb,),
        in_specs=[pl.BlockSpec((1, steps, N_GROUPS * GROUP_CH), per_n),
                  _const_spec((2 * N_GROUPS * GROUP_CH, LANES))] + cache_specs
                 + [_const_spec((HPG * steps, w + LANES)) for w in WINDOWS],
        out_specs=pl.BlockSpec((1, steps, GROUP_CH), per_n),
        out_shape=jax.ShapeDtypeStruct((nb, steps, GROUP_CH), F32),
        compiler_params=_params(1),
        name="l0_attn_sample",
    )(q, new_t, *caches, *biases)


def _in1_sample_kernel(x_ref, mod_ref, g_ref, w_ref, cbuf_ref, cw_ref, cb_ref, lg_ref, lb_ref, coef_ref, bs_ref,
                       wo_ref, y_ref, cst_ref, dv_ref, ext_ref, *, nb, steps):
    rows = nb * steps
    hist = (C_CONV - 1) * nb
    x = x_ref[0]
    h = _rms_mod(x, g_ref[...], mod_ref[0, 1], mod_ref[0, 0]).astype(BF16)
    proj = _mm(h, w_ref[...])
    ext_ref[0:hist, :] = cbuf_ref[...]
    ext_ref[hist:hist + rows, :] = proj[:, C_CH:2 * C_CH] * proj[:, 2 * C_CH:3 * C_CH]
    conv = jnp.broadcast_to(cb_ref[...], (rows, C_CH))
    for k in range(C_CONV):
        conv = conv + cw_ref[k:k + 1, :] * ext_ref[k * nb:k * nb + rows, :]
    yc = proj[:, :C_CH] * conv
    cst_ref[...] = ext_ref[rows:rows + hist, :]
    uv = jax.nn.gelu(proj[:, 3 * C_CH:], approximate=True)
    v = _layernorm(uv[:, D_CH:], lg_ref[...], lb_ref[...])
    dv_ref[...] = v
    v_r = v.astype(BF16).astype(F32)
    coef = coef_ref[...].astype(BF16).astype(F32)
    mixes = []
    for i in range(steps):
        mix = jnp.broadcast_to(bs_ref[i:i + 1, :], (nb, D_CH))
        for j in range(i + 1):
            mix = mix + coef[i * steps + j:i * steps + j + 1, :] * v_r[j * nb:(j + 1) * nb, :]
        mixes.append(mix)
    yd = uv[:, :D_CH] * jnp.concatenate(mixes, axis=0)
    cat = jnp.concatenate([yc.astype(BF16), yd.astype(BF16)], axis=1)
    y_ref[0] = x + mod_ref[0, 2] * _mm(cat, wo_ref[...])


def _in1_sample(x, mod, g, w, cbuf_t, cw, cb, lg, lb, coef, bs_rows, wo, nb, steps):
    rows = nb * steps
    hist = (C_CONV - 1) * nb
    return pl.pallas_call(
        functools.partial(_in1_sample_kernel, nb=nb, steps=steps),
        grid=(1,),
        in_specs=[_const_spec((1, rows, D_MODEL)), _const_spec((1, 6, rows, D_MODEL)),
                  _const_spec((1, D_MODEL)), _const_spec((D_MODEL, O_IN)), _const_spec((hist, C_CH)),
                  _const_spec((C_CONV, C_CH)), _const_spec((1, C_CH)), _const_spec((1, D_CH)),
                  _const_spec((1, D_CH)), _const_spec((steps * steps, D_CH)), _const_spec((steps, D_CH)),
                  _const_spec((C_CH + D_CH, D_MODEL))],
        out_specs=[_const_spec((1, rows, D_MODEL)), _const_spec((hist, C_CH)), _const_spec((rows, D_CH))],
        out_shape=[jax.ShapeDtypeStruct((1, rows, D_MODEL), F32), jax.ShapeDtypeStruct((hist, C_CH), F32),
                   jax.ShapeDtypeStruct((rows, D_CH), F32)],
        scratch_shapes=[pltpu.VMEM((hist + rows, C_CH), F32)],
        compiler_params=_params(1),
        name="l1_mixer_sample",
    )(x, mod, g, w, cbuf_t, cw, cb, lg, lb, coef, bs_rows, wo)


def _t5_bucket(dist):
    n = np.maximum(np.asarray(dist, dtype=np.int64), 0)
    max_exact = N_BUCKETS // 2
    large = max_exact + (np.log(np.maximum(n, 1) / max_exact) / np.log(MAX_DISTANCE / max_exact)
                         * (N_BUCKETS - max_exact)).astype(np.int32)
    return np.where(n < max_exact, n, np.minimum(large, N_BUCKETS - 1)).astype(np.int32)


def _prompt_bias_base(table, grp):
    rel = Q_BLOCK - np.arange(2 * Q_BLOCK)
    band = (rel >= 0) & (rel <= SUB_WINDOW)
    tab = table[:, grp * HPG:(grp + 1) * HPG]
    bias = tab[_t5_bucket(np.clip(rel, 0, None) * DILATIONS[grp])].T
    return jnp.where(band[None], bias, NEG).astype(F32)


def _sample_bias(table, grp, steps, pad):
    w, d = WINDOWS[grp], DILATIONS[grp]
    delta = w + steps - 1 - np.arange(w + pad + steps - 1)
    valid = (delta >= 0) & (delta % d == 0) & (delta // d <= SUB_WINDOW)
    tab = table[:, grp * HPG:(grp + 1) * HPG]
    vec = jnp.where(valid[None], tab[_t5_bucket(np.clip(delta, 0, None))].T, NEG).astype(F32)
    per_step = [vec[:, steps - 1 - s:steps - 1 - s + w + pad] for s in range(steps)]
    return jnp.stack(per_step, axis=1).reshape(HPG * steps, w + pad)


def _step_major(x):
    return jnp.swapaxes(x, 0, 1).reshape((x.shape[0] * x.shape[1],) + x.shape[2:])


def _batch_major(x, nb, steps):
    return jnp.swapaxes(x.reshape((steps, nb) + x.shape[1:]), 0, 1)


def kernel(x_prompt, x_sample, c_prompt, c_sample, state_a_conv, cache_b_kv0, cache_b_kv1, cache_b_kv2, state_c_conv, rel_bias_table, ada_w, ada_b, norm_mix_g, norm_ffn_g, ffn_w_gate_up, ffn_w_down, final_norm_g, e_w_in, a_conv_w, a_conv_b, a_ln_g, a_ln_b, e_w_out, o_w_in, c_conv_w, c_conv_b, d_ln_g, d_ln_b, d_spatial_w, d_spatial_b, o_w_out):
    nb_p = x_prompt.shape[0]
    nb_s, steps, _ = x_sample.shape
    pad = LANES

    w_in0 = e_w_in[0].astype(BF16)
    w_out0 = e_w_out[0].astype(BF16)
    w_in1 = o_w_in[0].astype(BF16)
    w_out1 = o_w_out[0].astype(BF16)
    w_gu = ffn_w_gate_up.astype(BF16)
    w_dn = ffn_w_down.astype(BF16)
    fin_g = final_norm_g.reshape(1, D_MODEL)

    mod = _ada(jnp.concatenate([c_prompt, c_sample], axis=0), ada_w, ada_b)
    mod = mod.reshape(mod.shape[0], nb_p + nb_s, 6, D_MODEL)
    mod_p = [mod[l, :nb_p].reshape(nb_p, 6, 1, D_MODEL) for l in range(2)]
    mod_s = [jnp.swapaxes(jnp.tile(mod[l, nb_p:], (steps, 1, 1)), 0, 1)[None] for l in range(2)]

    (a, a_st_p, q0, q1, q2, kv0, kv1, kv2, st0, st1, st2) = _in0_prompt(
        x_prompt, mod_p[0], norm_mix_g[0:1], w_in0, a_conv_w[0], a_conv_b, a_ln_g, a_ln_b)
    base = jnp.stack([_prompt_bias_base(rel_bias_table, grp) for grp in range(N_GROUPS)])
    attn = _attn_prompt((q0, q1, q2), (kv0, kv1, kv2), base)
    xs = _step_major(x_sample)[None]
    abuf_t = _step_major(state_a_conv[0])
    a_s, a_st_s, q_s, new_t = _in0_sample(xs, mod_s[0], norm_mix_g[0:1], w_in0, abuf_t, a_conv_w[0], a_conv_b,
                                          a_ln_g, a_ln_b, nb_s, steps)
    caches = [jnp.transpose(c[0], (0, 2, 3, 4, 1)).reshape(nb_s, 2 * GROUP_CH, c.shape[2])
              for c in (cache_b_kv0, cache_b_kv1, cache_b_kv2)]
    xp, s2 = _ffn(x_prompt, mod_p[0], norm_ffn_g[0:1], w_gu, w_dn, fin_g, 0, False, (a, attn, w_out0),
                  (((caches[2], 2),), new_t, steps))
    bs_full = jnp.repeat(d_spatial_b[0].T, D_GROUP_CH, axis=1)
    xp, c_st_p = _in1_prompt(xp, mod_p[1], norm_mix_g[1:2], w_in1, c_conv_w[0], c_conv_b, d_ln_g, d_ln_b,
                             d_spatial_w[0], bs_full, w_out1)
    y_prompt, s0, s1 = _ffn(xp, mod_p[1], norm_ffn_g[1:2], w_gu, w_dn, fin_g, 1, True, None,
                            (((caches[0], 0), (caches[1], 1)), new_t, steps))

    q_b = _batch_major(q_s, nb_s, steps)
    biases = [_sample_bias(rel_bias_table, grp, steps, pad) for grp in range(N_GROUPS)]
    attn_s = _attn_sample(q_b, new_t, caches, biases, steps)
    attn_s = _step_major(attn_s).astype(BF16)[None]
    (xs,) = _ffn(xs, mod_s[0], norm_ffn_g[0:1], w_gu, w_dn, fin_g, 0, False, (a_s, attn_s, w_out0))
    cbuf_t = _step_major(state_c_conv[0])
    coef = jnp.repeat(jnp.transpose(d_spatial_w[0][:, :steps, :steps], (1, 2, 0)).reshape(steps * steps, D_GROUPS),
                      D_GROUP_CH, axis=1)
    xs, c_st_s, dv_s = _in1_sample(xs, mod_s[1], norm_mix_g[1:2], w_in1, cbuf_t, c_conv_w[0], c_conv_b, d_ln_g,
                                   d_ln_b, coef, bs_full[:steps], w_out1, nb_s, steps)
    (y_sample,) = _ffn(xs, mod_s[1], norm_ffn_g[1:2], w_gu, w_dn, fin_g, 1, True)

    def kv_state_prompt(st):
        return st.reshape(1, nb_p, st.shape[1], 2, HPG, HEAD_DIM)

    def kv_state_sample(st):
        return jnp.transpose(st.reshape(nb_s, 2, HPG, HEAD_DIM, st.shape[-1]), (0, 4, 1, 2, 3))[None]

    return (y_prompt, _batch_major(y_sample[0], nb_s, steps),
            a_st_p[None], _batch_major(a_st_s, nb_s, A_CONV - 1)[None],
            kv_state_prompt(st0), kv_state_sample(s0), kv_state_prompt(st1), kv_state_sample(s1),
            kv_state_prompt(st2), kv_state_sample(s2),
            c_st_p[None], _batch_major(c_st_s, nb_s, C_CONV - 1)[None],
            _batch_major(dv_s, nb_s, steps)[None])
```

```python
import functools

import numpy as np
import jax
import jax.numpy as jnp
from jax import lax
from jax.experimental import pallas as pl
from jax.experimental.pallas import tpu as pltpu

F32 = jnp.float32
BF16 = jnp.bfloat16

D_MODEL = 1024
EPS = 1e-6
A_CH = 256
A_CONV = 31
HEAD_DIM = 64
HPG = 4
GROUP_CH = HPG * HEAD_DIM
WINDOWS = (128, 512, 2048)
DILATIONS = (1, 4, 16)
N_GROUPS = 3
SUB_WINDOW = 128
Q_BLOCK = 128
N_BUCKETS = 32
MAX_DISTANCE = 2048
C_CH = 512
C_CONV = 3
D_CH = 512
D_GROUPS = 4
D_GROUP_CH = D_CH // D_GROUPS
D_CHUNK = 128
FFN_HIDDEN = 2816
Q_OFF = 2 * A_CH
K_OFF = Q_OFF + N_GROUPS * GROUP_CH
V_OFF = K_OFF + N_GROUPS * GROUP_CH
E_IN = V_OFF + N_GROUPS * GROUP_CH
O_IN = 3 * C_CH + 2 * D_CH
LANES = 128
SUBLANES = 8
GROUP_TILES = GROUP_CH // LANES
NEG = -1e30
ROW_TILE = 512
FFN_ROW_TILE = 512
CONV_ROWS = 64
MIX_ROWS = 256
SHIFT_ROWS = 64
VMEM_LIMIT = 56 * 1024 * 1024


def _params(n_axes):
    return pltpu.CompilerParams(dimension_semantics=("arbitrary",) * n_axes, vmem_limit_bytes=VMEM_LIMIT)


def _const_spec(shape):
    return pl.BlockSpec(shape, lambda *_: (0,) * len(shape), pipeline_mode=pl.Buffered(1))


def _layer_spec(shape, layer):
    return pl.BlockSpec((1,) + shape, lambda *_: (layer,) + (0,) * len(shape), pipeline_mode=pl.Buffered(1))


def _rms_mod(x, g, scale, shift):
    return (x * lax.rsqrt(jnp.mean(x * x, axis=-1, keepdims=True) + EPS)) * (g * (1.0 + scale)) + shift


def _mod(mod_ref, k, rows):
    m = mod_ref[0, k]
    reps = rows // m.shape[0] if m.shape[0] > 1 else 1
    return m if reps == 1 else jnp.concatenate([m] * reps, axis=0)


def _layernorm(x, g, b):
    mu = jnp.mean(x, axis=-1, keepdims=True)
    xc = x - mu
    var = jnp.mean(xc * xc, axis=-1, keepdims=True)
    return xc * lax.rsqrt(var + EPS) * g + b


def _silu(x):
    return x * jax.nn.sigmoid(x)


def _mm(a, b):
    return jnp.dot(a, b, preferred_element_type=F32)


def _mm_nt(a, b):
    return lax.dot_general(a, b, (((1,), (1,)), ((), ())), preferred_element_type=F32)


def _group_cols(proj, off, g):
    return proj[:, off + g * GROUP_CH:off + (g + 1) * GROUP_CH]


def _ada_kernel(c_ref, w_ref, b_ref, o_ref):
    cs = _silu(c_ref[...]).astype(BF16)
    o_ref[0] = _mm(cs, w_ref[0].astype(BF16)) + b_ref[0]


def _ada(c_all, ada_w, ada_b):
    depth, _, width = ada_w.shape
    nb = c_all.shape[0]
    tn = 1536
    return pl.pallas_call(
        _ada_kernel,
        grid=(depth, width // tn),
        in_specs=[_const_spec((nb, D_MODEL)),
                  pl.BlockSpec((1, D_MODEL, tn), lambda l, j: (l, 0, j)),
                  pl.BlockSpec((1, 1, tn), lambda l, j: (l, 0, j))],
        out_specs=pl.BlockSpec((1, nb, tn), lambda l, j: (l, 0, j)),
        out_shape=jax.ShapeDtypeStruct((depth, nb, width), F32),
        compiler_params=_params(2),
        name="ada_mod",
    )(c_all, ada_w, ada_b.reshape(depth, 1, width))


def _ordering_zero(x):
    bits = lax.shift_right_logical(lax.bitcast_convert_type(x, jnp.uint32), jnp.uint32(16))
    return lax.bitcast_convert_type(lax.shift_right_logical(bits, jnp.uint32(16)), F32)


def _conv_ln_silu(ext_ref, shift_ref, base, rows, wb_ref, cb_ref, lg_ref, lb_ref, taps, after):
    ch = cb_ref.shape[-1]
    zero = _ordering_zero(after[-SUBLANES:, :ch])
    acc = jnp.broadcast_to(cb_ref[...], (rows // SUBLANES, SUBLANES, ch))
    for k in range(taps):
        m = (base + k) % SUBLANES
        src = ext_ref if m == 0 else shift_ref.at[m - 1]
        x = src[pl.ds(base + k - m, rows), :].reshape(rows // SUBLANES, SUBLANES, ch)
        acc = acc + (wb_ref[k] + zero)[None] * x
    return _silu(_layernorm(acc.reshape(rows, ch), lg_ref[...], lb_ref[...]))


def _in0_prompt_kernel(x_ref, mod_ref, g_ref, w_ref, cw_ref, cb_ref, lg_ref, lb_ref,
                       a_ref, ast_ref, q0_ref, q1_ref, q2_ref, kv0_ref, kv1_ref, kv2_ref,
                       st0_ref, st1_ref, st2_ref, ext_ref, shift_ref, perm_ref, wb_ref, *, tm):
    t = pl.program_id(1)
    last = pl.num_programs(1) - 1
    halo = A_CONV - 1
    pad = 32

    @pl.when(t == 0)
    def _():
        ext_ref[0:pad, :] = jnp.zeros((pad, A_CH), F32)
        for k in range(A_CONV):
            wb_ref[k] = jnp.broadcast_to(cw_ref[k:k + 1, :], (SUBLANES, A_CH))

    h = _rms_mod(x_ref[0], g_ref[...], mod_ref[0, 1], mod_ref[0, 0]).astype(BF16)
    a_in = _mm(h, w_ref[:, :Q_OFF])
    ext_ref[pad:pad + tm, :] = a_in[:, :A_CH] * jax.nn.sigmoid(a_in[:, A_CH:])
    for m in range(1, SUBLANES):
        shift_ref[m - 1] = ext_ref[pl.ds(m, tm + pad - SUBLANES), :]

    def conv_chunk(c, after):
        r0 = c * CONV_ROWS
        y = _conv_ln_silu(ext_ref, shift_ref, pad - halo + r0, CONV_ROWS, wb_ref, cb_ref, lg_ref, lb_ref, A_CONV,
                          after)
        a_ref[0, r0:r0 + CONV_ROWS, :] = y.astype(BF16)

    def project(g, off, dst, col0, slot0):
        d = DILATIONS[g]
        val = _mm(h, w_ref[:, off + g * GROUP_CH:off + (g + 1) * GROUP_CH])
        if off != Q_OFF:
            st_refs[g][0, :, col0:col0 + GROUP_CH] = val[tm - min(WINDOWS[g], tm):]
        if d == 1:
            dst[0, 0, :, col0:col0 + GROUP_CH] = val.astype(BF16)
            return val
        for c in range(GROUP_TILES):
            tile = perm_ref.at[slot0 + c]
            tile[...] = val[:, c * LANES:(c + 1) * LANES]
            for r in range(d):
                dst[0, r, :, col0 + c * LANES:col0 + (c + 1) * LANES] = (
                    tile[pl.ds(r, tm // d, stride=d), :].astype(BF16))
        return val

    q_refs, kv_refs, st_refs = (q0_ref, q1_ref, q2_ref), (kv0_ref, kv1_ref, kv2_ref), (st0_ref, st1_ref, st2_ref)
    tasks = [(g, off, (q_refs if off == Q_OFF else kv_refs)[g], GROUP_CH if off == V_OFF else 0)
             for g in range(N_GROUPS) for off in (Q_OFF, K_OFF, V_OFF)]
    n_chunks = tm // CONV_ROWS
    for i, (g, off, dst, col0) in enumerate(tasks):
        val = project(g, off, dst, col0, i * GROUP_TILES)
        if i < n_chunks:
            conv_chunk(i, val)
    for c in range(len(tasks), n_chunks):
        conv_chunk(c, val)
    ext_ref[pl.ds(pad - halo, halo), :] = ext_ref[pl.ds(tm + pad - halo, halo), :]

    @pl.when(t == last)
    def _():
        ast_ref[0] = ext_ref[pl.ds(pad - halo, halo), :]


def _in0_prompt(x, mod, g, w, cw, cb, lg, lb):
    nb, t, _ = x.shape
    tm = ROW_TILE
    assert t % tm == 0 and tm >= WINDOWS[1] and WINDOWS[2] == t
    row = lambda n, i: (n, i, 0)
    per_n = lambda n, i: (n, 0, 0)
    res = lambda n, i: (n, 0, i, 0)
    out_shape = [jax.ShapeDtypeStruct((nb, t, A_CH), BF16), jax.ShapeDtypeStruct((nb, A_CONV - 1, A_CH), F32)]
    out_specs = [pl.BlockSpec((1, tm, A_CH), row), pl.BlockSpec((1, A_CONV - 1, A_CH), per_n)]
    for width in (GROUP_CH, 2 * GROUP_CH):
        for d in DILATIONS:
            out_shape.append(jax.ShapeDtypeStruct((nb, d, t // d, width), BF16))
            out_specs.append(pl.BlockSpec((1, d, tm // d, width), res))
    out_shape += [jax.ShapeDtypeStruct((nb, WINDOWS[0], 2 * GROUP_CH), F32),
                  jax.ShapeDtypeStruct((nb, WINDOWS[1], 2 * GROUP_CH), F32),
                  jax.ShapeDtypeStruct((nb, t, 2 * GROUP_CH), F32)]
    out_specs += [pl.BlockSpec((1, WINDOWS[0], 2 * GROUP_CH), per_n),
                  pl.BlockSpec((1, WINDOWS[1], 2 * GROUP_CH), per_n),
                  pl.BlockSpec((1, tm, 2 * GROUP_CH), row)]
    return pl.pallas_call(
        functools.partial(_in0_prompt_kernel, tm=tm),
        grid=(nb, t // tm),
        in_specs=[pl.BlockSpec((1, tm, D_MODEL), row),
                  pl.BlockSpec((1, 6, 1, D_MODEL), lambda n, i: (n, 0, 0, 0)),
                  _const_spec((1, D_MODEL)), _const_spec((D_MODEL, E_IN)),
                  _const_spec((A_CONV, A_CH)), _const_spec((1, A_CH)),
                  _const_spec((1, A_CH)), _const_spec((1, A_CH))],
        out_specs=out_specs,
        out_shape=out_shape,
        scratch_shapes=[pltpu.VMEM((tm + 32, A_CH), F32), pltpu.VMEM((SUBLANES - 1, tm + 32 - SUBLANES, A_CH), F32),
                        pltpu.VMEM((3 * N_GROUPS * GROUP_TILES, tm, LANES), F32),
                        pltpu.VMEM((A_CONV, SUBLANES, A_CH), F32)],
        compiler_params=_params(2),
        name="l0_in_prompt",
    )(x, mod, g, w, cw, cb, lg, lb)


def _head_masks(rows):
    col_head = lax.broadcasted_iota(jnp.int32, (rows, GROUP_CH), 1) // HEAD_DIM
    return [col_head == h for h in range(HPG)]


def _mix_groups(outs, lses):
    m = jnp.maximum(jnp.maximum(lses[0], lses[1]), lses[2])
    es = [jnp.exp(l - m) for l in lses]
    inv = 1.0 / (es[0] + es[1] + es[2])
    return (es[0] * inv) * outs[0] + (es[1] * inv) * outs[1] + (es[2] * inv) * outs[2]


def _attn_prompt_kernel(q0_ref, kv0_ref, q1_ref, kv1_ref, q2_ref, kv2_ref, base_ref, attn_ref,
                        bias_ref, o_ref, lse_ref, *, t):
    masks = _head_masks(Q_BLOCK)
    masks_bf = [jnp.where(m, HEAD_DIM ** -0.5, 0.0).astype(BF16) for m in masks]
    key_idx = lax.broadcasted_iota(jnp.int32, (HPG * Q_BLOCK, 2 * Q_BLOCK), 1)

    @pl.when(pl.program_id(0) == 0)
    def _():
        for g in range(N_GROUPS):
            for h in range(HPG):
                row = jnp.broadcast_to(base_ref[g, h:h + 1, :], (Q_BLOCK, 2 * Q_BLOCK))
                bias_ref[g, h * Q_BLOCK:(h + 1) * Q_BLOCK, :] = pltpu.roll(row, 0, 1, stride=1, stride_axis=0)

    for g, (d, q_ref, kv_ref) in enumerate(zip(DILATIONS, (q0_ref, q1_ref, q2_ref), (kv0_ref, kv1_ref, kv2_ref))):
        nblk = t // d // Q_BLOCK
        single = nblk == 1

        def block(j, carry, g=g, d=d, q_ref=q_ref, kv_ref=kv_ref, nblk=nblk, single=single):
            r = j // nblk
            i = j % nblk
            rows = pl.ds(pl.multiple_of(i * Q_BLOCK, Q_BLOCK), Q_BLOCK)
            q = q_ref[0, r, rows, :]
            qs = jnp.concatenate([q * mb for mb in masks_bf], axis=0)
            if single:
                kk = kv_ref[0, r, rows, :]
                bias = bias_ref[g, :, Q_BLOCK:]
            else:
                prev = pl.ds(pl.multiple_of(jnp.maximum(i - 1, 0) * Q_BLOCK, Q_BLOCK), Q_BLOCK)
                kk = jnp.concatenate([kv_ref[0, r, prev, :], kv_ref[0, r, rows, :]], axis=0)
                bias = bias_ref[g] + jnp.where(key_idx < jnp.where(i == 0, Q_BLOCK, 0), NEG, 0.0)
            s = _mm_nt(qs, kk[:, :GROUP_CH]) + bias
            m = jnp.max(s, axis=-1, keepdims=True)
            e = jnp.exp(s - m)
            l = jnp.sum(e, axis=-1, keepdims=True)
            o_all = _mm((e * (1.0 / l)).astype(BF16), kk[:, GROUP_CH:])
            lse_all = jnp.broadcast_to(m + jnp.log(l), (HPG * Q_BLOCK, GROUP_CH))
            o = o_all[:Q_BLOCK]
            lse = lse_all[:Q_BLOCK]
            for h in range(1, HPG):
                o = jnp.where(masks[h], o_all[h * Q_BLOCK:(h + 1) * Q_BLOCK], o)
                lse = jnp.where(masks[h], lse_all[h * Q_BLOCK:(h + 1) * Q_BLOCK], lse)
            tokens = rows if d == 1 else pl.ds(r + i * (Q_BLOCK * d), Q_BLOCK, stride=d)
            for c in range(GROUP_TILES):
                o_ref[g, c, tokens, :] = o[:, c * LANES:(c + 1) * LANES]
                lse_ref[g, c, tokens, :] = lse[:, c * LANES:(c + 1) * LANES]
            return carry

        lax.fori_loop(0, d * nblk, block, 0, unroll=16)

    for r0 in range(0, t, MIX_ROWS):
        for c in range(GROUP_TILES):
            outs = [o_ref[g, c, r0:r0 + MIX_ROWS, :] for g in range(N_GROUPS)]
            lses = [lse_ref[g, c, r0:r0 + MIX_ROWS, :] for g in range(N_GROUPS)]
            attn_ref[0, r0:r0 + MIX_ROWS, c * LANES:(c + 1) * LANES] = _mix_groups(outs, lses).astype(BF16)


def _attn_prompt(qs, kvs, base):
    nb, _, t, _ = qs[0].shape
    blk = lambda n: (n, 0, 0, 0)
    in_specs = []
    for d in DILATIONS:
        in_specs += [pl.BlockSpec((1, d, t // d, GROUP_CH), blk), pl.BlockSpec((1, d, t // d, 2 * GROUP_CH), blk)]
    operands = [x for pair in zip(qs, kvs) for x in pair]
    return pl.pallas_call(
        functools.partial(_attn_prompt_kernel, t=t),
        grid=(nb,),
        in_specs=in_specs + [_const_spec((N_GROUPS, HPG, 2 * Q_BLOCK))],
        out_specs=pl.BlockSpec((1, t, GROUP_CH), lambda n: (n, 0, 0)),
        out_shape=jax.ShapeDtypeStruct((nb, t, GROUP_CH), BF16),
        scratch_shapes=[pltpu.VMEM((N_GROUPS, HPG * Q_BLOCK, 2 * Q_BLOCK), F32),
                        pltpu.VMEM((N_GROUPS, GROUP_TILES, t, LANES), F32),
                        pltpu.VMEM((N_GROUPS, GROUP_TILES, t, LANES), F32)],
        compiler_params=_params(1),
        name="l0_attn_prompt",
    )(*operands, base)


def _ffn_rows(x, mod_ref, g_ref, wgu_ref, wd_ref, fg_ref, mixer, wo_ref, final, chunk):
    rows = x.shape[0]
    if mixer is not None:
        a, attn = mixer
        x = x + _mod(mod_ref, 2, rows) * _mm(jnp.concatenate([a, attn], axis=1), wo_ref[...])
    h = _rms_mod(x, g_ref[...], _mod(mod_ref, 4, rows), _mod(mod_ref, 3, rows)).astype(BF16)
    acc = jnp.zeros(x.shape, F32)
    for c0 in range(0, FFN_HIDDEN, chunk):
        c1 = min(c0 + chunk, FFN_HIDDEN)
        gate = _mm(h, wgu_ref[0, :, c0:c1])
        up = _mm(h, wgu_ref[0, :, FFN_HIDDEN + c0:FFN_HIDDEN + c1])
        acc = acc + _mm((_silu(gate) * up).astype(BF16), wd_ref[0, c0:c1, :])
    y = x + _mod(mod_ref, 5, rows) * acc
    if final:
        y = y * lax.rsqrt(jnp.mean(y * y, axis=-1, keepdims=True) + EPS) * fg_ref[...]
    return y


def _ffn_kernel(*refs, final, chunk, mixer_out, n_shift, shift_steps, n_prompt_steps):
    refs = list(refs)
    x_ref, mod_ref, g_ref, wgu_ref, wd_ref, fg_ref = refs[:6]
    pos = 6
    a_ref = attn_ref = wo_ref = as_ref = attns_ref = None
    if mixer_out:
        a_ref, attn_ref, wo_ref = refs[pos:pos + 3]
        pos += 3
    shift_in = refs[pos:pos + 2 * n_shift]
    pos += 2 * n_shift
    xs_ref, mods_ref = refs[pos:pos + 2]
    pos += 2
    if mixer_out:
        as_ref, attns_ref = refs[pos:pos + 2]
        pos += 2
    y_ref = refs[pos]
    state_refs = refs[pos + 1:pos + 1 + n_shift]
    ys_ref = refs[pos + 1 + n_shift]
    step = pl.program_id(0)

    @pl.when(step < n_prompt_steps)
    def _():
        for k, state_ref in enumerate(state_refs):
            cache_ref, newt_ref = shift_in[2 * k], shift_in[2 * k + 1]
            for r0 in range(0, 2 * GROUP_CH, SHIFT_ROWS):
                new = _new_steps(newt_ref, r0, r0 + SHIFT_ROWS, step, shift_steps)
                state_ref[0, r0:r0 + SHIFT_ROWS, :] = _shift_window(cache_ref[0, r0:r0 + SHIFT_ROWS, :], new,
                                                                    shift_steps)[1]
        mixer = (a_ref[0], attn_ref[0]) if mixer_out else None
        y_ref[0] = _ffn_rows(x_ref[0], mod_ref, g_ref, wgu_ref, wd_ref, fg_ref, mixer, wo_ref, final, chunk)

    @pl.when(step == n_prompt_steps)
    def _():
        mixer = (as_ref[0], attns_ref[0]) if mixer_out else None
        ys_ref[0] = _ffn_rows(xs_ref[0], mods_ref, g_ref, wgu_ref, wd_ref, fg_ref, mixer, wo_ref, final, chunk)


def _ffn(x, mod, xs, mod_s, g, wgu, wd, final_g, layer, final, mixer_out=None, shift=None):
    nb, t, _ = x.shape
    tm = min(FFN_ROW_TILE, t)
    n_tiles = t // tm
    n_steps = nb * n_tiles
    rows_s = xs.shape[1]

    def prompt_map(*tail):
        def index_map(s):
            p = jnp.minimum(s, n_steps - 1)
            return (p // n_tiles, p % n_tiles) + tail
        return index_map

    per_n = lambda s: (jnp.minimum(s, n_steps - 1) // n_tiles, 0, 0, 0)
    in_specs = [pl.BlockSpec((1, tm, D_MODEL), prompt_map(0)),
                pl.BlockSpec((1, 6, 1, D_MODEL), per_n),
                _const_spec((1, D_MODEL)),
                _layer_spec((D_MODEL, 2 * FFN_HIDDEN), layer), _layer_spec((FFN_HIDDEN, D_MODEL), layer),
                _const_spec((1, D_MODEL))]
    operands = [x, mod, g, wgu, wd, final_g]
    sample_specs = [_const_spec((1, rows_s, D_MODEL)), _const_spec((1, 6) + mod_s.shape[2:])]
    sample_operands = [xs, mod_s]
    if mixer_out is not None:
        a, attn, a_s, attn_s, wo = mixer_out
        in_specs += [pl.BlockSpec((1, tm, A_CH), prompt_map(0)), pl.BlockSpec((1, tm, GROUP_CH), prompt_map(0)),
                     _const_spec((A_CH + GROUP_CH, D_MODEL))]
        operands += [a, attn, wo]
        sample_specs += [_const_spec((1, rows_s, A_CH)), _const_spec((1, rows_s, GROUP_CH))]
        sample_operands += [a_s, attn_s]
    out_specs = [pl.BlockSpec((1, tm, D_MODEL), prompt_map(0))]
    out_shape = [jax.ShapeDtypeStruct((nb, t, D_MODEL), F32)]
    shift_groups, shift_steps = (), 0
    if shift is not None:
        shift_groups, new_t, shift_steps = shift
        per_step = lambda s: (jnp.minimum(s, n_steps - 1), 0, 0)
        for cache, grp in shift_groups:
            assert cache.shape[0] == n_steps
            in_specs += [pl.BlockSpec((1,) + cache.shape[1:], per_step),
                         pl.BlockSpec((2 * GROUP_CH, LANES), lambda s, grp=grp: (grp, 0),
                                      pipeline_mode=pl.Buffered(1))]
            operands += [cache, new_t]
            out_specs.append(pl.BlockSpec((1,) + cache.shape[1:], per_step))
            out_shape.append(jax.ShapeDtypeStruct(cache.shape, F32))
    out_specs.append(_const_spec((1, rows_s, D_MODEL)))
    out_shape.append(jax.ShapeDtypeStruct((1, rows_s, D_MODEL), F32))
    return pl.pallas_call(
        functools.partial(_ffn_kernel, final=final, chunk=512, mixer_out=mixer_out is not None,
                          n_shift=len(shift_groups), shift_steps=shift_steps, n_prompt_steps=n_steps),
        grid=(n_steps + 1,),
        in_specs=in_specs + sample_specs,
        out_specs=out_specs,
        out_shape=out_shape,
        compiler_params=_params(1),
        name="ffn_final" if final else "l0_out_ffn",
    )(*operands, *sample_operands)


def _spatial_tril(ws_ref, grp):
    i = lax.broadcasted_iota(jnp.int32, (D_CHUNK, D_CHUNK), 0)
    j = lax.broadcasted_iota(jnp.int32, (D_CHUNK, D_CHUNK), 1)
    return jnp.where(j <= i, ws_ref[grp], 0.0).astype(BF16)


def _in1_prompt_kernel(x_ref, mod_ref, g_ref, w_ref, cw_ref, cb_ref, lg_ref, lb_ref, ws_ref, bs_ref, wo_ref,
                       y_ref, cst_ref, ext_ref, cat_ref, *, tm):
    t = pl.program_id(1)
    last = pl.num_programs(1) - 1
    halo = C_CONV - 1
    pad = 8

    @pl.when(t == 0)
    def _():
        ext_ref[0:pad, :] = jnp.zeros((pad, C_CH), F32)

    x = x_ref[0]
    h = _rms_mod(x, g_ref[...], mod_ref[0, 1], mod_ref[0, 0]).astype(BF16)
    v_in, u_in, xi, cg, bg = (_mm(h, w_ref[:, c0:c0 + C_CH]) for c0 in range(O_IN - C_CH, -1, -C_CH))
    ext_ref[pad:pad + tm, :] = cg * xi
    conv = jnp.broadcast_to(cb_ref[...], (tm, C_CH))
    for k in range(C_CONV):
        conv = conv + cw_ref[k:k + 1, :] * ext_ref[pl.ds(pad - halo + k, tm), :]
    cat_ref[:, :C_CH] = (bg * conv).astype(BF16)
    ext_ref[pl.ds(pad - halo, halo), :] = ext_ref[pl.ds(tm + pad - halo, halo), :]
    u = jax.nn.gelu(u_in, approximate=True)
    v = _layernorm(jax.nn.gelu(v_in, approximate=True), lg_ref[...], lb_ref[...]).astype(BF16)
    for grp in range(D_GROUPS):
        ws = _spatial_tril(ws_ref, grp)
        cols = slice(grp * D_GROUP_CH, (grp + 1) * D_GROUP_CH)
        for c in range(tm // D_CHUNK):
            rows = slice(c * D_CHUNK, (c + 1) * D_CHUNK)
            mix = _mm(ws, v[rows, cols]) + bs_ref[:, cols]
            cat_ref[rows, C_CH + grp * D_GROUP_CH:C_CH + (grp + 1) * D_GROUP_CH] = (u[rows, cols] * mix).astype(BF16)
    y_ref[0] = x + mod_ref[0, 2] * _mm(cat_ref[...], wo_ref[...])

    @pl.when(t == last)
    def _():
        cst_ref[0] = ext_ref[pl.ds(pad - halo, halo), :]


def _in1_prompt(x, mod, g, w, cw, cb, lg, lb, ws, bs_full, wo):
    nb, t, _ = x.shape
    tm = ROW_TILE
    row = lambda n, i: (n, i, 0)
    return pl.pallas_call(
        functools.partial(_in1_prompt_kernel, tm=tm),
        grid=(nb, t // tm),
        in_specs=[pl.BlockSpec((1, tm, D_MODEL), row),
                  pl.BlockSpec((1, 6, 1, D_MODEL), lambda n, i: (n, 0, 0, 0)),
                  _const_spec((1, D_MODEL)), _const_spec((D_MODEL, O_IN)),
                  _const_spec((C_CONV, C_CH)), _const_spec((1, C_CH)),
                  _const_spec((1, D_CH)), _const_spec((1, D_CH)),
                  _const_spec((D_GROUPS, D_CHUNK, D_CHUNK)), _const_spec((D_CHUNK, D_CH)),
                  _const_spec((C_CH + D_CH, D_MODEL))],
        out_specs=[pl.BlockSpec((1, tm, D_MODEL), row),
                   pl.BlockSpec((1, C_CONV - 1, C_CH), lambda n, i: (n, 0, 0))],
        out_shape=[jax.ShapeDtypeStruct((nb, t, D_MODEL), F32),
                   jax.ShapeDtypeStruct((nb, C_CONV - 1, C_CH), F32)],
        scratch_shapes=[pltpu.VMEM((tm + 8, C_CH), F32), pltpu.VMEM((tm, C_CH + D_CH), BF16)],
        compiler_params=_params(2),
        name="l1_mixer_prompt",
    )(x, mod, g, w, cw, cb, lg, lb, ws, bs_full, wo)


def _in0_sample_kernel(x_ref, mod_ref, g_ref, w_ref, abuf_ref, cw_ref, cb_ref, lg_ref, lb_ref,
                       a_ref, ast_ref, q_ref, newt_ref, ext_ref, perm_ref, *, nb, steps):
    rows = nb * steps
    hist = (A_CONV - 1) * nb
    h = _rms_mod(x_ref[0], g_ref[...], _mod(mod_ref, 1, rows), _mod(mod_ref, 0, rows)).astype(BF16)
    proj = _mm(h, w_ref[...])
    ext_ref[0:hist, :] = abuf_ref[...]
    ext_ref[hist:hist + rows, :] = proj[:, :A_CH] * jax.nn.sigmoid(proj[:, A_CH:2 * A_CH])
    acc = jnp.broadcast_to(cb_ref[...], (rows, A_CH))
    for k in range(A_CONV):
        acc = acc + cw_ref[k:k + 1, :] * ext_ref[k * nb:k * nb + rows, :]
    a_ref[0] = _silu(_layernorm(acc, lg_ref[...], lb_ref[...])).astype(BF16)
    ast_ref[...] = ext_ref[rows:rows + hist, :]
    q_ref[...] = proj[:, Q_OFF:K_OFF]
    tile = 0
    for g in range(N_GROUPS):
        for off in (K_OFF, V_OFF):
            for c in range(GROUP_TILES):
                col = off + g * GROUP_CH + c * LANES
                for s in range(steps):
                    perm_ref[tile, pl.ds(s, nb, stride=steps), :] = proj[s * nb:(s + 1) * nb, col:col + LANES]
                newt_ref[tile * LANES:(tile + 1) * LANES, :] = perm_ref[tile].T
                tile += 1


def _in0_sample(x, mod, g, w, abuf_t, cw, cb, lg, lb, nb, steps):
    rows = nb * steps
    assert rows == LANES
    hist = (A_CONV - 1) * nb
    return pl.pallas_call(
        functools.partial(_in0_sample_kernel, nb=nb, steps=steps),
        grid=(1,),
        in_specs=[_const_spec((1, rows, D_MODEL)), _const_spec((1, 6, nb, D_MODEL)),
                  _const_spec((1, D_MODEL)), _const_spec((D_MODEL, E_IN)), _const_spec((hist, A_CH)),
                  _const_spec((A_CONV, A_CH)), _const_spec((1, A_CH)), _const_spec((1, A_CH)),
                  _const_spec((1, A_CH))],
        out_specs=[_const_spec((1, rows, A_CH)), _const_spec((hist, A_CH)),
                   _const_spec((rows, N_GROUPS * GROUP_CH)), _const_spec((2 * N_GROUPS * GROUP_CH, LANES))],
        out_shape=[jax.ShapeDtypeStruct((1, rows, A_CH), BF16), jax.ShapeDtypeStruct((hist, A_CH), F32),
                   jax.ShapeDtypeStruct((rows, N_GROUPS * GROUP_CH), F32),
                   jax.ShapeDtypeStruct((2 * N_GROUPS * GROUP_CH, LANES), F32)],
        scratch_shapes=[pltpu.VMEM((hist + rows, A_CH), F32),
                        pltpu.VMEM((2 * N_GROUPS * GROUP_TILES, LANES, LANES), F32)],
        compiler_params=_params(1),
        name="l0_in_sample",
    )(x, mod, g, w, abuf_t, cw, cb, lg, lb)


def _new_steps(newt_ref, r0, r1, n, steps):
    return pltpu.roll(newt_ref[r0:r1, :], lax.rem(LANES - n * steps, LANES), axis=1)


def _shift_window(cache_rows, new_rows, steps):
    w = cache_rows.shape[-1]
    comb = jnp.concatenate([cache_rows, new_rows], axis=1)
    return comb, pltpu.roll(comb, w + LANES - steps, axis=1)[:, :w]


def _attn_sample_kernel(q_ref, newt_ref, c0_ref, c1_ref, c2_ref, base0_ref, base1_ref, base2_ref,
                        attn_ref, b0_ref, b1_ref, b2_ref, *, steps):
    n = pl.program_id(0)
    rows = HPG * steps

    @pl.when(n == 0)
    def _():
        for base_ref, b_ref in ((base0_ref, b0_ref), (base1_ref, b1_ref), (base2_ref, b2_ref)):
            for h in range(HPG):
                row = jnp.broadcast_to(base_ref[h:h + 1, :], (SUBLANES, base_ref.shape[-1]))
                b_ref[h * steps:(h + 1) * steps, :] = pltpu.roll(row, 0, 1, stride=1, stride_axis=0)[:steps]

    row_head = lax.broadcasted_iota(jnp.int32, (rows, GROUP_CH), 0) // steps
    col_head = lax.broadcasted_iota(jnp.int32, (rows, GROUP_CH), 1) // HEAD_DIM
    step_masks = _head_masks(steps)
    outs, lses = [], []
    for g, (c_ref, b_ref) in enumerate(zip((c0_ref, c1_ref, c2_ref), (b0_ref, b1_ref, b2_ref))):
        new = _new_steps(newt_ref, 2 * g * GROUP_CH, 2 * (g + 1) * GROUP_CH, n, steps)
        comb_bf = jnp.concatenate([c_ref[0], new], axis=1).astype(BF16)
        q = q_ref[0][:, g * GROUP_CH:(g + 1) * GROUP_CH]
        q_bd = jnp.where(row_head == col_head, jnp.concatenate([q] * HPG, axis=0), 0.0).astype(BF16)
        s = _mm(q_bd, comb_bf[:GROUP_CH]) * (HEAD_DIM ** -0.5) + b_ref[...]
        m = jnp.max(s, axis=-1, keepdims=True)
        e = jnp.exp(s - m)
        l = jnp.sum(e, axis=-1, keepdims=True)
        p = (e * (1.0 / l)).astype(BF16)
        o_all = _mm_nt(p, comb_bf[GROUP_CH:])
        lse_all = jnp.broadcast_to(m + jnp.log(l), (rows, GROUP_CH))
        o = jnp.zeros((steps, GROUP_CH), F32)
        lse = jnp.zeros((steps, GROUP_CH), F32)
        for h in range(HPG):
            o = jnp.where(step_masks[h], o_all[h * steps:(h + 1) * steps], o)
            lse = jnp.where(step_masks[h], lse_all[h * steps:(h + 1) * steps], lse)
        outs.append(o)
        lses.append(lse)
    attn_ref[0] = _mix_groups(outs, lses)


def _attn_sample(q, new_t, caches, biases, steps):
    nb = q.shape[0]
    per_n = lambda n: (n, 0, 0)
    cache_specs = [pl.BlockSpec((1, 2 * GROUP_CH, w), per_n) for w in WINDOWS]
    return pl.pallas_call(
        functools.partial(_attn_sample_kernel, steps=steps),
        grid=(nb,),
        in_specs=[pl.BlockSpec((1, steps, N_GROUPS * GROUP_CH), per_n),
                  _const_spec((2 * N_GROUPS * GROUP_CH, LANES))] + cache_specs
                 + [_const_spec((HPG, w + LANES)) for w in WINDOWS],
        out_specs=pl.BlockSpec((1, steps, GROUP_CH), per_n),
        out_shape=jax.ShapeDtypeStruct((nb, steps, GROUP_CH), F32),
        scratch_shapes=[pltpu.VMEM((HPG * steps, w + LANES), F32) for w in WINDOWS],
        compiler_params=_params(1),
        name="l0_attn_sample",
    )(q, new_t, *caches, *biases)


def _in1_sample_kernel(x_ref, mod_ref, g_ref, w_ref, cbuf_ref, cw_ref, cb_ref, lg_ref, lb_ref, coef_ref, bs_ref,
                       wo_ref, y_ref, cst_ref, dv_ref, ext_ref, *, nb, steps):
    rows = nb * steps
    hist = (C_CONV - 1) * nb
    x = x_ref[0]
    h = _rms_mod(x, g_ref[...], _mod(mod_ref, 1, rows), _mod(mod_ref, 0, rows)).astype(BF16)
    proj = _mm(h, w_ref[...])
    ext_ref[0:hist, :] = cbuf_ref[...]
    ext_ref[hist:hist + rows, :] = proj[:, C_CH:2 * C_CH] * proj[:, 2 * C_CH:3 * C_CH]
    conv = jnp.broadcast_to(cb_ref[...], (rows, C_CH))
    for k in range(C_CONV):
        conv = conv + cw_ref[k:k + 1, :] * ext_ref[k * nb:k * nb + rows, :]
    yc = proj[:, :C_CH] * conv
    cst_ref[...] = ext_ref[rows:rows + hist, :]
    uv = jax.nn.gelu(proj[:, 3 * C_CH:], approximate=True)
    v = _layernorm(uv[:, D_CH:], lg_ref[...], lb_ref[...])
    dv_ref[...] = v
    v_r = v.astype(BF16).astype(F32)
    coef = coef_ref[...].astype(BF16).astype(F32)
    mixes = []
    for i in range(steps):
        mix = jnp.broadcast_to(bs_ref[i:i + 1, :], (nb, D_CH))
        for j in range(i + 1):
            mix = mix + coef[i * steps + j:i * steps + j + 1, :] * v_r[j * nb:(j + 1) * nb, :]
        mixes.append(mix)
    yd = uv[:, :D_CH] * jnp.concatenate(mixes, axis=0)
    cat = jnp.concatenate([yc.astype(BF16), yd.astype(BF16)], axis=1)
    y_ref[0] = x + _mod(mod_ref, 2, rows) * _mm(cat, wo_ref[...])


def _in1_sample(x, mod, g, w, cbuf_t, cw, cb, lg, lb, coef, bs_rows, wo, nb, steps):
    rows = nb * steps
    hist = (C_CONV - 1) * nb
    return pl.pallas_call(
        functools.partial(_in1_sample_kernel, nb=nb, steps=steps),
        grid=(1,),
        in_specs=[_const_spec((1, rows, D_MODEL)), _const_spec((1, 6, nb, D_MODEL)),
                  _const_spec((1, D_MODEL)), _const_spec((D_MODEL, O_IN)), _const_spec((hist, C_CH)),
                  _const_spec((C_CONV, C_CH)), _const_spec((1, C_CH)), _const_spec((1, D_CH)),
                  _const_spec((1, D_CH)), _const_spec((steps * steps, D_CH)), _const_spec((steps, D_CH)),
                  _const_spec((C_CH + D_CH, D_MODEL))],
        out_specs=[_const_spec((1, rows, D_MODEL)), _const_spec((hist, C_CH)), _const_spec((rows, D_CH))],
        out_shape=[jax.ShapeDtypeStruct((1, rows, D_MODEL), F32), jax.ShapeDtypeStruct((hist, C_CH), F32),
                   jax.ShapeDtypeStruct((rows, D_CH), F32)],
        scratch_shapes=[pltpu.VMEM((hist + rows, C_CH), F32)],
        compiler_params=_params(1),
        name="l1_mixer_sample",
    )(x, mod, g, w, cbuf_t, cw, cb, lg, lb, coef, bs_rows, wo)


def _t5_bucket(dist):
    n = np.maximum(np.asarray(dist, dtype=np.int64), 0)
    max_exact = N_BUCKETS // 2
    large = max_exact + (np.log(np.maximum(n, 1) / max_exact) / np.log(MAX_DISTANCE / max_exact)
                         * (N_BUCKETS - max_exact)).astype(np.int32)
    return np.where(n < max_exact, n, np.minimum(large, N_BUCKETS - 1)).astype(np.int32)


def _prompt_bias_base(table, grp):
    rel = Q_BLOCK - np.arange(2 * Q_BLOCK)
    band = (rel >= 0) & (rel <= SUB_WINDOW)
    tab = table[:, grp * HPG:(grp + 1) * HPG]
    bias = tab[_t5_bucket(np.clip(rel, 0, None) * DILATIONS[grp])].T
    return jnp.where(band[None], bias, NEG).astype(F32)


def _sample_bias_base(table, grp):
    w, d = WINDOWS[grp], DILATIONS[grp]
    delta = w - np.arange(w + LANES)
    valid = (delta >= 0) & (delta % d == 0) & (delta // d <= SUB_WINDOW)
    tab = table[:, grp * HPG:(grp + 1) * HPG]
    return jnp.where(valid[None], tab[_t5_bucket(np.clip(delta, 0, None))].T, NEG).astype(F32)


def _step_major(x):
    return jnp.swapaxes(x, 0, 1).reshape((x.shape[0] * x.shape[1],) + x.shape[2:])


def _batch_major(x, nb, steps):
    return jnp.swapaxes(x.reshape((steps, nb) + x.shape[1:]), 0, 1)


def kernel(x_prompt, x_sample, c_prompt, c_sample, state_a_conv, cache_b_kv0, cache_b_kv1, cache_b_kv2, state_c_conv, rel_bias_table, ada_w, ada_b, norm_mix_g, norm_ffn_g, ffn_w_gate_up, ffn_w_down, final_norm_g, e_w_in, a_conv_w, a_conv_b, a_ln_g, a_ln_b, e_w_out, o_w_in, c_conv_w, c_conv_b, d_ln_g, d_ln_b, d_spatial_w, d_spatial_b, o_w_out):
    nb_p = x_prompt.shape[0]
    nb_s, steps, _ = x_sample.shape

    w_in0 = e_w_in[0].astype(BF16)
    w_out0 = e_w_out[0].astype(BF16)
    w_in1 = o_w_in[0].astype(BF16)
    w_out1 = o_w_out[0].astype(BF16)
    w_gu = ffn_w_gate_up.astype(BF16)
    w_dn = ffn_w_down.astype(BF16)
    fin_g = final_norm_g.reshape(1, D_MODEL)

    mod = _ada(jnp.concatenate([c_prompt, c_sample], axis=0), ada_w, ada_b)
    mod = mod.reshape(mod.shape[0], nb_p + nb_s, 6, D_MODEL)
    mod_p = [mod[l, :nb_p].reshape(nb_p, 6, 1, D_MODEL) for l in range(2)]
    mod_s = [jnp.swapaxes(mod[l, nb_p:], 0, 1)[None] for l in range(2)]

    xs = _step_major(x_sample)[None]
    abuf_t = _step_major(state_a_conv[0])
    a_s, a_st_s, q_s, new_t = _in0_sample(xs, mod_s[0], norm_mix_g[0:1], w_in0, abuf_t, a_conv_w[0], a_conv_b,
                                          a_ln_g, a_ln_b, nb_s, steps)
    caches = [jnp.transpose(c[0], (0, 2, 3, 4, 1)).reshape(nb_s, 2 * GROUP_CH, c.shape[2])
              for c in (cache_b_kv0, cache_b_kv1, cache_b_kv2)]
    biases = [_sample_bias_base(rel_bias_table, grp) for grp in range(N_GROUPS)]
    attn_s = _attn_sample(_batch_major(q_s, nb_s, steps), new_t, caches, biases, steps)
    attn_s = _step_major(attn_s).astype(BF16)[None]
    (a, a_st_p, q0, q1, q2, kv0, kv1, kv2, st0, st1, st2) = _in0_prompt(
        x_prompt, mod_p[0], norm_mix_g[0:1], w_in0, a_conv_w[0], a_conv_b, a_ln_g, a_ln_b)
    base = jnp.stack([_prompt_bias_base(rel_bias_table, grp) for grp in range(N_GROUPS)])
    attn = _attn_prompt((q0, q1, q2), (kv0, kv1, kv2), base)
    xp, s2, xs = _ffn(x_prompt, mod_p[0], xs, mod_s[0], norm_ffn_g[0:1], w_gu, w_dn, fin_g, 0, False,
                      (a, attn, a_s, attn_s, w_out0), (((caches[2], 2),), new_t, steps))

    bs_full = jnp.repeat(d_spatial_b[0].T, D_GROUP_CH, axis=1)
    xp, c_st_p = _in1_prompt(xp, mod_p[1], norm_mix_g[1:2], w_in1, c_conv_w[0], c_conv_b, d_ln_g, d_ln_b,
                             d_spatial_w[0], bs_full, w_out1)
    cbuf_t = _step_major(state_c_conv[0])
    coef = jnp.repeat(jnp.transpose(d_spatial_w[0][:, :steps, :steps], (1, 2, 0)).reshape(steps * steps, D_GROUPS),
                      D_GROUP_CH, axis=1)
    xs, c_st_s, dv_s = _in1_sample(xs, mod_s[1], norm_mix_g[1:2], w_in1, cbuf_t, c_conv_w[0], c_conv_b, d_ln_g,
                                   d_ln_b, coef, bs_full[:steps], w_out1, nb_s, steps)
    y_prompt, s0, s1, y_sample = _ffn(xp, mod_p[1], xs, mod_s[1], norm_ffn_g[1:2], w_gu, w_dn, fin_g, 1, True,
                                      None, (((caches[0], 0), (caches[1], 1)), new_t, steps))

    def kv_state_prompt(st):
        return st.reshape(1, nb_p, st.shape[1], 2, HPG, HEAD_DIM)

    def kv_state_sample(st):
        return jnp.transpose(st.reshape(nb_s, 2, HPG, HEAD_DIM, st.shape[-1]), (0, 4, 1, 2, 3))[None]

    return (y_prompt, _batch_major(y_sample[0], nb_s, steps),
            a_st_p[None], _batch_major(a_st_s, nb_s, A_CONV - 1)[None],
            kv_state_prompt(st0), kv_state_sample(s0), kv_state_prompt(st1), kv_state_sample(s1),
            kv_state_prompt(st2), kv_state_sample(s2),
            c_st_p[None], _batch_major(c_st_s, nb_s, C_CONV - 1)[None],
            _batch_major(dv_s, nb_s, steps)[None])
```

```python
import functools

import numpy as np
import jax
import jax.numpy as jnp
from jax import lax
from jax.experimental import pallas as pl
from jax.experimental.pallas import tpu as pltpu

F32 = jnp.float32
BF16 = jnp.bfloat16

D_MODEL = 1024
EPS = 1e-6
A_CH = 256
A_CONV = 31
HEAD_DIM = 64
HPG = 4
GROUP_CH = HPG * HEAD_DIM
WINDOWS = (128, 512, 2048)
DILATIONS = (1, 4, 16)
N_GROUPS = 3
SUB_WINDOW = 128
Q_BLOCK = 128
N_BUCKETS = 32
MAX_DISTANCE = 2048
C_CH = 512
C_CONV = 3
D_CH = 512
D_GROUPS = 4
D_GROUP_CH = D_CH // D_GROUPS
D_CHUNK = 128
FFN_HIDDEN = 2816
Q_OFF = 2 * A_CH
K_OFF = Q_OFF + N_GROUPS * GROUP_CH
V_OFF = K_OFF + N_GROUPS * GROUP_CH
E_IN = V_OFF + N_GROUPS * GROUP_CH
O_IN = 3 * C_CH + 2 * D_CH
LANES = 128
SUBLANES = 8
GROUP_TILES = GROUP_CH // LANES
NEG = -1e30
ROW_TILE = 512
FFN_ROW_TILE = 512
FFN_CHUNK = 512
WEIGHT_CHUNK = 256
CONV_ROWS = 64
MIX_ROWS = 256
SHIFT_ROWS = 64
VMEM_LIMIT = 58 * 1024 * 1024


def _params(n_axes):
    return pltpu.CompilerParams(dimension_semantics=("arbitrary",) * n_axes, vmem_limit_bytes=VMEM_LIMIT)


def _const_spec(shape):
    return pl.BlockSpec(shape, lambda *_: (0,) * len(shape), pipeline_mode=pl.Buffered(1))


def _rms_mod(x, g, scale, shift):
    return (x * lax.rsqrt(jnp.mean(x * x, axis=-1, keepdims=True) + EPS)) * (g * (1.0 + scale)) + shift


def _mod(mod_ref, k, rows):
    m = mod_ref[0, k]
    reps = rows // m.shape[0] if m.shape[0] > 1 else 1
    return m if reps == 1 else jnp.concatenate([m] * reps, axis=0)


def _layernorm(x, g, b):
    mu = jnp.mean(x, axis=-1, keepdims=True)
    xc = x - mu
    var = jnp.mean(xc * xc, axis=-1, keepdims=True)
    return xc * lax.rsqrt(var + EPS) * g + b


def _silu(x):
    return x * jax.nn.sigmoid(x)


def _mm(a, b):
    return jnp.dot(a, b, preferred_element_type=F32)


def _mm_nt(a, b):
    return lax.dot_general(a, b, (((1,), (1,)), ((), ())), preferred_element_type=F32)


def _group_cols(proj, off, g):
    return proj[:, off + g * GROUP_CH:off + (g + 1) * GROUP_CH]


def _ada_kernel(c_ref, w_ref, b_ref, o_ref):
    cs = _silu(c_ref[...]).astype(BF16)
    o_ref[0] = _mm(cs, w_ref[0].astype(BF16)) + b_ref[0]


def _ada(c_all, ada_w, ada_b):
    depth, _, width = ada_w.shape
    nb = c_all.shape[0]
    tn = 1536
    return pl.pallas_call(
        _ada_kernel,
        grid=(depth, width // tn),
        in_specs=[_const_spec((nb, D_MODEL)),
                  pl.BlockSpec((1, D_MODEL, tn), lambda l, j: (l, 0, j)),
                  pl.BlockSpec((1, 1, tn), lambda l, j: (l, 0, j))],
        out_specs=pl.BlockSpec((1, nb, tn), lambda l, j: (l, 0, j)),
        out_shape=jax.ShapeDtypeStruct((depth, nb, width), F32),
        compiler_params=_params(2),
        name="ada_mod",
    )(c_all, ada_w, ada_b.reshape(depth, 1, width))


def _ordering_zero(x):
    bits = lax.shift_right_logical(lax.bitcast_convert_type(x, jnp.uint32), jnp.uint32(16))
    return lax.bitcast_convert_type(lax.shift_right_logical(bits, jnp.uint32(16)), F32)


def _conv_ln_silu(ext_ref, shift_ref, base, rows, wb_ref, cb_ref, lg_ref, lb_ref, taps, after):
    ch = cb_ref.shape[-1]
    zero = _ordering_zero(after[-SUBLANES:, :ch])
    acc = jnp.broadcast_to(cb_ref[...], (rows // SUBLANES, SUBLANES, ch))
    for k in range(taps):
        m = (base + k) % SUBLANES
        src = ext_ref if m == 0 else shift_ref.at[m - 1]
        x = src[pl.ds(base + k - m, rows), :].reshape(rows // SUBLANES, SUBLANES, ch)
        acc = acc + (wb_ref[k] + zero)[None] * x
    return _silu(_layernorm(acc.reshape(rows, ch), lg_ref[...], lb_ref[...]))


def _in0_prompt_kernel(x_ref, mod_ref, g_ref, w_ref, cw_ref, cb_ref, lg_ref, lb_ref,
                       a_ref, ast_ref, q0_ref, q1_ref, q2_ref, kv0_ref, kv1_ref, kv2_ref,
                       st0_ref, st1_ref, st2_ref, ext_ref, shift_ref, perm_ref, wb_ref, *, tm):
    t = pl.program_id(1)
    last = pl.num_programs(1) - 1
    halo = A_CONV - 1
    pad = 32

    @pl.when(t == 0)
    def _():
        ext_ref[0:pad, :] = jnp.zeros((pad, A_CH), F32)
        for k in range(A_CONV):
            wb_ref[k] = jnp.broadcast_to(cw_ref[k:k + 1, :], (SUBLANES, A_CH))

    h = _rms_mod(x_ref[0], g_ref[...], mod_ref[0, 1], mod_ref[0, 0]).astype(BF16)
    a_in = _mm(h, w_ref[:, :Q_OFF])
    ext_ref[pad:pad + tm, :] = a_in[:, :A_CH] * jax.nn.sigmoid(a_in[:, A_CH:])
    for m in range(1, SUBLANES):
        shift_ref[m - 1] = ext_ref[pl.ds(m, tm + pad - SUBLANES), :]

    def conv_chunk(c, after):
        r0 = c * CONV_ROWS
        y = _conv_ln_silu(ext_ref, shift_ref, pad - halo + r0, CONV_ROWS, wb_ref, cb_ref, lg_ref, lb_ref, A_CONV,
                          after)
        a_ref[0, r0:r0 + CONV_ROWS, :] = y.astype(BF16)

    def project(g, off, dst, col0, slot0):
        d = DILATIONS[g]
        val = _mm(h, w_ref[:, off + g * GROUP_CH:off + (g + 1) * GROUP_CH])
        if off != Q_OFF:
            st_refs[g][0, :, col0:col0 + GROUP_CH] = val[tm - min(WINDOWS[g], tm):]
        if d == 1:
            dst[0, 0, :, col0:col0 + GROUP_CH] = val.astype(BF16)
            return val
        for c in range(GROUP_TILES):
            tile = perm_ref.at[slot0 + c]
            tile[...] = val[:, c * LANES:(c + 1) * LANES]
            for r in range(d):
                dst[0, r, :, col0 + c * LANES:col0 + (c + 1) * LANES] = (
                    tile[pl.ds(r, tm // d, stride=d), :].astype(BF16))
        return val

    q_refs, kv_refs, st_refs = (q0_ref, q1_ref, q2_ref), (kv0_ref, kv1_ref, kv2_ref), (st0_ref, st1_ref, st2_ref)
    tasks = [(g, off, (q_refs if off == Q_OFF else kv_refs)[g], GROUP_CH if off == V_OFF else 0)
             for g in range(N_GROUPS) for off in (Q_OFF, K_OFF, V_OFF)]
    n_chunks = tm // CONV_ROWS
    for i, (g, off, dst, col0) in enumerate(tasks):
        val = project(g, off, dst, col0, i * GROUP_TILES)
        if i < n_chunks:
            conv_chunk(i, val)
    for c in range(len(tasks), n_chunks):
        conv_chunk(c, val)
    ext_ref[pl.ds(pad - halo, halo), :] = ext_ref[pl.ds(tm + pad - halo, halo), :]

    @pl.when(t == last)
    def _():
        ast_ref[0] = ext_ref[pl.ds(pad - halo, halo), :]


def _in0_prompt(x, mod, g, w, cw, cb, lg, lb):
    nb, t, _ = x.shape
    tm = ROW_TILE
    assert t % tm == 0 and tm >= WINDOWS[1] and WINDOWS[2] == t
    row = lambda n, i: (n, i, 0)
    per_n = lambda n, i: (n, 0, 0)
    res = lambda n, i: (n, 0, i, 0)
    out_shape = [jax.ShapeDtypeStruct((nb, t, A_CH), BF16), jax.ShapeDtypeStruct((nb, A_CONV - 1, A_CH), F32)]
    out_specs = [pl.BlockSpec((1, tm, A_CH), row), pl.BlockSpec((1, A_CONV - 1, A_CH), per_n)]
    for width in (GROUP_CH, 2 * GROUP_CH):
        for d in DILATIONS:
            out_shape.append(jax.ShapeDtypeStruct((nb, d, t // d, width), BF16))
            out_specs.append(pl.BlockSpec((1, d, tm // d, width), res))
    out_shape += [jax.ShapeDtypeStruct((nb, WINDOWS[0], 2 * GROUP_CH), F32),
                  jax.ShapeDtypeStruct((nb, WINDOWS[1], 2 * GROUP_CH), F32),
                  jax.ShapeDtypeStruct((nb, t, 2 * GROUP_CH), F32)]
    out_specs += [pl.BlockSpec((1, WINDOWS[0], 2 * GROUP_CH), per_n),
                  pl.BlockSpec((1, WINDOWS[1], 2 * GROUP_CH), per_n),
                  pl.BlockSpec((1, tm, 2 * GROUP_CH), row)]
    return pl.pallas_call(
        functools.partial(_in0_prompt_kernel, tm=tm),
        grid=(nb, t // tm),
        in_specs=[pl.BlockSpec((1, tm, D_MODEL), row),
                  pl.BlockSpec((1, 6, 1, D_MODEL), lambda n, i: (n, 0, 0, 0)),
                  _const_spec((1, D_MODEL)), _const_spec((D_MODEL, E_IN)),
                  _const_spec((A_CONV, A_CH)), _const_spec((1, A_CH)),
                  _const_spec((1, A_CH)), _const_spec((1, A_CH))],
        out_specs=out_specs,
        out_shape=out_shape,
        scratch_shapes=[pltpu.VMEM((tm + 32, A_CH), F32), pltpu.VMEM((SUBLANES - 1, tm + 32 - SUBLANES, A_CH), F32),
                        pltpu.VMEM((3 * N_GROUPS * GROUP_TILES, tm, LANES), F32),
                        pltpu.VMEM((A_CONV, SUBLANES, A_CH), F32)],
        compiler_params=_params(2),
        name="l0_in_prompt",
    )(x, mod, g, w, cw, cb, lg, lb)


def _head_masks(rows):
    col_head = lax.broadcasted_iota(jnp.int32, (rows, GROUP_CH), 1) // HEAD_DIM
    return [col_head == h for h in range(HPG)]


def _mix_groups(outs, lses):
    m = jnp.maximum(jnp.maximum(lses[0], lses[1]), lses[2])
    es = [jnp.exp(l - m) for l in lses]
    inv = 1.0 / (es[0] + es[1] + es[2])
    return (es[0] * inv) * outs[0] + (es[1] * inv) * outs[1] + (es[2] * inv) * outs[2]


def _attn_prompt_kernel(q0_ref, kv0_ref, q1_ref, kv1_ref, q2_ref, kv2_ref, base_ref, attn_ref,
                        bias_ref, o_ref, lse_ref, *, t):
    masks = _head_masks(Q_BLOCK)
    masks_bf = [jnp.where(m, HEAD_DIM ** -0.5, 0.0).astype(BF16) for m in masks]
    key_idx = lax.broadcasted_iota(jnp.int32, (HPG * Q_BLOCK, 2 * Q_BLOCK), 1)

    @pl.when(pl.program_id(0) == 0)
    def _():
        for g in range(N_GROUPS):
            for h in range(HPG):
                row = jnp.broadcast_to(base_ref[g, h:h + 1, :], (Q_BLOCK, 2 * Q_BLOCK))
                bias_ref[g, h * Q_BLOCK:(h + 1) * Q_BLOCK, :] = pltpu.roll(row, 0, 1, stride=1, stride_axis=0)

    for g, (d, q_ref, kv_ref) in enumerate(zip(DILATIONS, (q0_ref, q1_ref, q2_ref), (kv0_ref, kv1_ref, kv2_ref))):
        nblk = t // d // Q_BLOCK
        single = nblk == 1

        def block(j, carry, g=g, d=d, q_ref=q_ref, kv_ref=kv_ref, nblk=nblk, single=single):
            r = j // nblk
            i = j % nblk
            rows = pl.ds(pl.multiple_of(i * Q_BLOCK, Q_BLOCK), Q_BLOCK)
            q = q_ref[0, r, rows, :]
            qs = jnp.concatenate([q * mb for mb in masks_bf], axis=0)
            if single:
                kk = kv_ref[0, r, rows, :]
                bias = bias_ref[g, :, Q_BLOCK:]
            else:
                prev = pl.ds(pl.multiple_of(jnp.maximum(i - 1, 0) * Q_BLOCK, Q_BLOCK), Q_BLOCK)
                kk = jnp.concatenate([kv_ref[0, r, prev, :], kv_ref[0, r, rows, :]], axis=0)
                bias = bias_ref[g] + jnp.where(key_idx < jnp.where(i == 0, Q_BLOCK, 0), NEG, 0.0)
            s = _mm_nt(qs, kk[:, :GROUP_CH]) + bias
            m = jnp.max(s, axis=-1, keepdims=True)
            e = jnp.exp(s - m)
            l = jnp.sum(e, axis=-1, keepdims=True)
            o_all = _mm((e * (1.0 / l)).astype(BF16), kk[:, GROUP_CH:])
            lse_all = jnp.broadcast_to(m + jnp.log(l), (HPG * Q_BLOCK, GROUP_CH))
            o = o_all[:Q_BLOCK]
            lse = lse_all[:Q_BLOCK]
            for h in range(1, HPG):
                o = jnp.where(masks[h], o_all[h * Q_BLOCK:(h + 1) * Q_BLOCK], o)
                lse = jnp.where(masks[h], lse_all[h * Q_BLOCK:(h + 1) * Q_BLOCK], lse)
            tokens = rows if d == 1 else pl.ds(r + i * (Q_BLOCK * d), Q_BLOCK, stride=d)
            for c in range(GROUP_TILES):
                o_ref[g, c, tokens, :] = o[:, c * LANES:(c + 1) * LANES]
                lse_ref[g, c, tokens, :] = lse[:, c * LANES:(c + 1) * LANES]
            return carry

        lax.fori_loop(0, d * nblk, block, 0, unroll=16)

    for r0 in range(0, t, MIX_ROWS):
        for c in range(GROUP_TILES):
            outs = [o_ref[g, c, r0:r0 + MIX_ROWS, :] for g in range(N_GROUPS)]
            lses = [lse_ref[g, c, r0:r0 + MIX_ROWS, :] for g in range(N_GROUPS)]
            attn_ref[0, r0:r0 + MIX_ROWS, c * LANES:(c + 1) * LANES] = _mix_groups(outs, lses).astype(BF16)


def _attn_prompt(qs, kvs, base):
    nb, _, t, _ = qs[0].shape
    blk = lambda n: (n, 0, 0, 0)
    in_specs = []
    for d in DILATIONS:
        in_specs += [pl.BlockSpec((1, d, t // d, GROUP_CH), blk), pl.BlockSpec((1, d, t // d, 2 * GROUP_CH), blk)]
    operands = [x for pair in zip(qs, kvs) for x in pair]
    return pl.pallas_call(
        functools.partial(_attn_prompt_kernel, t=t),
        grid=(nb,),
        in_specs=in_specs + [_const_spec((N_GROUPS, HPG, 2 * Q_BLOCK))],
        out_specs=pl.BlockSpec((1, t, GROUP_CH), lambda n: (n, 0, 0)),
        out_shape=jax.ShapeDtypeStruct((nb, t, GROUP_CH), BF16),
        scratch_shapes=[pltpu.VMEM((N_GROUPS, HPG * Q_BLOCK, 2 * Q_BLOCK), F32),
                        pltpu.VMEM((N_GROUPS, GROUP_TILES, t, LANES), F32),
                        pltpu.VMEM((N_GROUPS, GROUP_TILES, t, LANES), F32)],
        compiler_params=_params(1),
        name="l0_attn_prompt",
    )(*operands, base)


def _ffn_rows(x, mod_ref, g_ref, wgu_ref, wd_ref, fg_ref, mixer, wo_ref, final, chunk):
    rows = x.shape[0]
    if mixer is not None:
        a, attn = mixer
        x = x + _mod(mod_ref, 2, rows) * _mm(jnp.concatenate([a, attn], axis=1), wo_ref[...])
    h = _rms_mod(x, g_ref[...], _mod(mod_ref, 4, rows), _mod(mod_ref, 3, rows)).astype(BF16)
    acc = jnp.zeros(x.shape, F32)
    for c0 in range(0, FFN_HIDDEN, chunk):
        c1 = min(c0 + chunk, FFN_HIDDEN)
        gate = _mm(h, wgu_ref[:, c0:c1])
        up = _mm(h, wgu_ref[:, FFN_HIDDEN + c0:FFN_HIDDEN + c1])
        acc = acc + _mm((_silu(gate) * up).astype(BF16), wd_ref[c0:c1, :])
    y = x + _mod(mod_ref, 5, rows) * acc
    if final:
        y = y * lax.rsqrt(jnp.mean(y * y, axis=-1, keepdims=True) + EPS) * fg_ref[...]
    return y


def _load_as_bf16(src_hbm, layer, dst_ref, stage_ref, sem_ref, axis):
    chunk = stage_ref.shape[1 + axis]
    n_chunks = dst_ref.shape[axis] // chunk

    def window(c):
        return (slice(None), pl.ds(c * chunk, chunk)) if axis == 1 else (pl.ds(c * chunk, chunk), slice(None))

    def copy(c):
        return pltpu.make_async_copy(src_hbm.at[(layer,) + window(c)], stage_ref.at[c % 2], sem_ref.at[c % 2])

    copy(0).start()
    for c in range(n_chunks):
        if c + 1 < n_chunks:
            copy(c + 1).start()
        copy(c).wait()
        dst_ref[window(c)] = stage_ref[c % 2].astype(BF16)


def _ffn_kernel(*refs, layer, final, chunk, mixer_out, n_shift, shift_steps, n_prompt_steps):
    refs = list(refs)
    wgu_ref, wd_ref, stage_gu_ref, stage_dn_ref, sem_ref = refs[-5:]
    refs = refs[:-5]
    x_ref, mod_ref, g_ref, wgu_hbm, wd_hbm, fg_ref = refs[:6]
    pos = 6
    a_ref = attn_ref = wo_ref = as_ref = attns_ref = None
    if mixer_out:
        a_ref, attn_ref, wo_ref = refs[pos:pos + 3]
        pos += 3
    shift_in = refs[pos:pos + 2 * n_shift]
    pos += 2 * n_shift
    xs_ref, mods_ref = refs[pos:pos + 2]
    pos += 2
    if mixer_out:
        as_ref, attns_ref = refs[pos:pos + 2]
        pos += 2
    y_ref = refs[pos]
    state_refs = refs[pos + 1:pos + 1 + n_shift]
    ys_ref = refs[pos + 1 + n_shift]
    step = pl.program_id(0)

    @pl.when(step == 0)
    def _():
        _load_as_bf16(wgu_hbm, layer, wgu_ref, stage_gu_ref, sem_ref, 1)
        _load_as_bf16(wd_hbm, layer, wd_ref, stage_dn_ref, sem_ref, 0)

    @pl.when(step < n_prompt_steps)
    def _():
        for k, state_ref in enumerate(state_refs):
            cache_ref, newt_ref = shift_in[2 * k], shift_in[2 * k + 1]
            for r0 in range(0, 2 * GROUP_CH, SHIFT_ROWS):
                new = _new_steps(newt_ref, r0, r0 + SHIFT_ROWS, step, shift_steps)
                state_ref[0, r0:r0 + SHIFT_ROWS, :] = _shift_window(cache_ref[0, r0:r0 + SHIFT_ROWS, :], new,
                                                                    shift_steps)[1]
        mixer = (a_ref[0], attn_ref[0]) if mixer_out else None
        y_ref[0] = _ffn_rows(x_ref[0], mod_ref, g_ref, wgu_ref, wd_ref, fg_ref, mixer, wo_ref, final, chunk)

    @pl.when(step == n_prompt_steps)
    def _():
        mixer = (as_ref[0], attns_ref[0]) if mixer_out else None
        ys_ref[0] = _ffn_rows(xs_ref[0], mods_ref, g_ref, wgu_ref, wd_ref, fg_ref, mixer, wo_ref, final, chunk)


def _ffn(x, mod, xs, mod_s, g, wgu, wd, final_g, layer, final, mixer_out=None, shift=None):
    nb, t, _ = x.shape
    tm = min(FFN_ROW_TILE, t)
    n_tiles = t // tm
    n_steps = nb * n_tiles
    rows_s = xs.shape[1]

    def prompt_map(*tail):
        def index_map(s):
            p = jnp.minimum(s, n_steps - 1)
            return (p // n_tiles, p % n_tiles) + tail
        return index_map

    per_n = lambda s: (jnp.minimum(s, n_steps - 1) // n_tiles, 0, 0, 0)
    in_specs = [pl.BlockSpec((1, tm, D_MODEL), prompt_map(0)),
                pl.BlockSpec((1, 6, 1, D_MODEL), per_n),
                _const_spec((1, D_MODEL)),
                pl.BlockSpec(memory_space=pl.ANY), pl.BlockSpec(memory_space=pl.ANY),
                _const_spec((1, D_MODEL))]
    operands = [x, mod, g, wgu, wd, final_g]
    sample_specs = [_const_spec((1, rows_s, D_MODEL)), _const_spec((1, 6) + mod_s.shape[2:])]
    sample_operands = [xs, mod_s]
    if mixer_out is not None:
        a, attn, a_s, attn_s, wo = mixer_out
        in_specs += [pl.BlockSpec((1, tm, A_CH), prompt_map(0)), pl.BlockSpec((1, tm, GROUP_CH), prompt_map(0)),
                     _const_spec((A_CH + GROUP_CH, D_MODEL))]
        operands += [a, attn, wo]
        sample_specs += [_const_spec((1, rows_s, A_CH)), _const_spec((1, rows_s, GROUP_CH))]
        sample_operands += [a_s, attn_s]
    out_specs = [pl.BlockSpec((1, tm, D_MODEL), prompt_map(0))]
    out_shape = [jax.ShapeDtypeStruct((nb, t, D_MODEL), F32)]
    shift_groups, shift_steps = (), 0
    if shift is not None:
        shift_groups, new_t, shift_steps = shift
        per_step = lambda s: (jnp.minimum(s, n_steps - 1), 0, 0)
        for cache, grp in shift_groups:
            assert cache.shape[0] == n_steps
            in_specs += [pl.BlockSpec((1,) + cache.shape[1:], per_step),
                         pl.BlockSpec((2 * GROUP_CH, LANES), lambda s, grp=grp: (grp, 0),
                                      pipeline_mode=pl.Buffered(1))]
            operands += [cache, new_t]
            out_specs.append(pl.BlockSpec((1,) + cache.shape[1:], per_step))
            out_shape.append(jax.ShapeDtypeStruct(cache.shape, F32))
    out_specs.append(_const_spec((1, rows_s, D_MODEL)))
    out_shape.append(jax.ShapeDtypeStruct((1, rows_s, D_MODEL), F32))
    return pl.pallas_call(
        functools.partial(_ffn_kernel, layer=layer, final=final, chunk=FFN_CHUNK, mixer_out=mixer_out is not None,
                          n_shift=len(shift_groups), shift_steps=shift_steps, n_prompt_steps=n_steps),
        grid=(n_steps + 1,),
        in_specs=in_specs + sample_specs,
        out_specs=out_specs,
        out_shape=out_shape,
        scratch_shapes=[pltpu.VMEM((D_MODEL, 2 * FFN_HIDDEN), BF16), pltpu.VMEM((FFN_HIDDEN, D_MODEL), BF16),
                        pltpu.VMEM((2, D_MODEL, WEIGHT_CHUNK), F32), pltpu.VMEM((2, WEIGHT_CHUNK, D_MODEL), F32),
                        pltpu.SemaphoreType.DMA((2,))],
        compiler_params=_params(1),
        name="ffn_final" if final else "l0_out_ffn",
    )(*operands, *sample_operands)


def _spatial_tril(ws_ref, grp):
    i = lax.broadcasted_iota(jnp.int32, (D_CHUNK, D_CHUNK), 0)
    j = lax.broadcasted_iota(jnp.int32, (D_CHUNK, D_CHUNK), 1)
    return jnp.where(j <= i, ws_ref[grp], 0.0).astype(BF16)


def _in1_prompt_kernel(x_ref, mod_ref, g_ref, w_ref, cw_ref, cb_ref, lg_ref, lb_ref, ws_ref, bs_ref, wo_ref,
                       y_ref, cst_ref, ext_ref, cat_ref, *, tm):
    t = pl.program_id(1)
    last = pl.num_programs(1) - 1
    halo = C_CONV - 1
    pad = 8

    @pl.when(t == 0)
    def _():
        ext_ref[0:pad, :] = jnp.zeros((pad, C_CH), F32)

    x = x_ref[0]
    h = _rms_mod(x, g_ref[...], mod_ref[0, 1], mod_ref[0, 0]).astype(BF16)
    v_in, u_in, xi, cg, bg = (_mm(h, w_ref[:, c0:c0 + C_CH]) for c0 in range(O_IN - C_CH, -1, -C_CH))
    ext_ref[pad:pad + tm, :] = cg * xi
    conv = jnp.broadcast_to(cb_ref[...], (tm, C_CH))
    for k in range(C_CONV):
        conv = conv + cw_ref[k:k + 1, :] * ext_ref[pl.ds(pad - halo + k, tm), :]
    cat_ref[:, :C_CH] = (bg * conv).astype(BF16)
    ext_ref[pl.ds(pad - halo, halo), :] = ext_ref[pl.ds(tm + pad - halo, halo), :]
    u = jax.nn.gelu(u_in, approximate=True)
    v = _layernorm(jax.nn.gelu(v_in, approximate=True), lg_ref[...], lb_ref[...]).astype(BF16)
    for grp in range(D_GROUPS):
        ws = _spatial_tril(ws_ref, grp)
        cols = slice(grp * D_GROUP_CH, (grp + 1) * D_GROUP_CH)
        for c in range(tm // D_CHUNK):
            rows = slice(c * D_CHUNK, (c + 1) * D_CHUNK)
            mix = _mm(ws, v[rows, cols]) + bs_ref[:, cols]
            cat_ref[rows, C_CH + grp * D_GROUP_CH:C_CH + (grp + 1) * D_GROUP_CH] = (u[rows, cols] * mix).astype(BF16)
    y_ref[0] = x + mod_ref[0, 2] * _mm(cat_ref[...], wo_ref[...])

    @pl.when(t == last)
    def _():
        cst_ref[0] = ext_ref[pl.ds(pad - halo, halo), :]


def _in1_prompt(x, mod, g, w, cw, cb, lg, lb, ws, bs_full, wo):
    nb, t, _ = x.shape
    tm = ROW_TILE
    row = lambda n, i: (n, i, 0)
    return pl.pallas_call(
        functools.partial(_in1_prompt_kernel, tm=tm),
        grid=(nb, t // tm),
        in_specs=[pl.BlockSpec((1, tm, D_MODEL), row),
                  pl.BlockSpec((1, 6, 1, D_MODEL), lambda n, i: (n, 0, 0, 0)),
                  _const_spec((1, D_MODEL)), _const_spec((D_MODEL, O_IN)),
                  _const_spec((C_CONV, C_CH)), _const_spec((1, C_CH)),
                  _const_spec((1, D_CH)), _const_spec((1, D_CH)),
                  _const_spec((D_GROUPS, D_CHUNK, D_CHUNK)), _const_spec((D_CHUNK, D_CH)),
                  _const_spec((C_CH + D_CH, D_MODEL))],
        out_specs=[pl.BlockSpec((1, tm, D_MODEL), row),
                   pl.BlockSpec((1, C_CONV - 1, C_CH), lambda n, i: (n, 0, 0))],
        out_shape=[jax.ShapeDtypeStruct((nb, t, D_MODEL), F32),
                   jax.ShapeDtypeStruct((nb, C_CONV - 1, C_CH), F32)],
        scratch_shapes=[pltpu.VMEM((tm + 8, C_CH), F32), pltpu.VMEM((tm, C_CH + D_CH), BF16)],
        compiler_params=_params(2),
        name="l1_mixer_prompt",
    )(x, mod, g, w, cw, cb, lg, lb, ws, bs_full, wo)


def _in0_sample_kernel(x_ref, mod_ref, g_ref, w_ref, abuf_ref, cw_ref, cb_ref, lg_ref, lb_ref,
                       a_ref, ast_ref, q_ref, newt_ref, ext_ref, perm_ref, *, nb, steps):
    rows = nb * steps
    hist = (A_CONV - 1) * nb
    h = _rms_mod(x_ref[0], g_ref[...], _mod(mod_ref, 1, rows), _mod(mod_ref, 0, rows)).astype(BF16)
    proj = _mm(h, w_ref[...])
    ext_ref[0:hist, :] = abuf_ref[...]
    ext_ref[hist:hist + rows, :] = proj[:, :A_CH] * jax.nn.sigmoid(proj[:, A_CH:2 * A_CH])
    acc = jnp.broadcast_to(cb_ref[...], (rows, A_CH))
    for k in range(A_CONV):
        acc = acc + cw_ref[k:k + 1, :] * ext_ref[k * nb:k * nb + rows, :]
    a_ref[0] = _silu(_layernorm(acc, lg_ref[...], lb_ref[...])).astype(BF16)
    ast_ref[...] = ext_ref[rows:rows + hist, :]
    q_ref[...] = proj[:, Q_OFF:K_OFF]
    tile = 0
    for g in range(N_GROUPS):
        for off in (K_OFF, V_OFF):
            for c in range(GROUP_TILES):
                col = off + g * GROUP_CH + c * LANES
                for s in range(steps):
                    perm_ref[tile, pl.ds(s, nb, stride=steps), :] = proj[s * nb:(s + 1) * nb, col:col + LANES]
                newt_ref[tile * LANES:(tile + 1) * LANES, :] = perm_ref[tile].T
                tile += 1


def _in0_sample(x, mod, g, w, abuf_t, cw, cb, lg, lb, nb, steps):
    rows = nb * steps
    assert rows == LANES
    hist = (A_CONV - 1) * nb
    return pl.pallas_call(
        functools.partial(_in0_sample_kernel, nb=nb, steps=steps),
        grid=(1,),
        in_specs=[_const_spec((1, rows, D_MODEL)), _const_spec((1, 6, nb, D_MODEL)),
                  _const_spec((1, D_MODEL)), _const_spec((D_MODEL, E_IN)), _const_spec((hist, A_CH)),
                  _const_spec((A_CONV, A_CH)), _const_spec((1, A_CH)), _const_spec((1, A_CH)),
                  _const_spec((1, A_CH))],
        out_specs=[_const_spec((1, rows, A_CH)), _const_spec((hist, A_CH)),
                   _const_spec((rows, N_GROUPS * GROUP_CH)), _const_spec((2 * N_GROUPS * GROUP_CH, LANES))],
        out_shape=[jax.ShapeDtypeStruct((1, rows, A_CH), BF16), jax.ShapeDtypeStruct((hist, A_CH), F32),
                   jax.ShapeDtypeStruct((rows, N_GROUPS * GROUP_CH), F32),
                   jax.ShapeDtypeStruct((2 * N_GROUPS * GROUP_CH, LANES), F32)],
        scratch_shapes=[pltpu.VMEM((hist + rows, A_CH), F32),
                        pltpu.VMEM((2 * N_GROUPS * GROUP_TILES, LANES, LANES), F32)],
        compiler_params=_params(1),
        name="l0_in_sample",
    )(x, mod, g, w, abuf_t, cw, cb, lg, lb)


def _new_steps(newt_ref, r0, r1, n, steps):
    return pltpu.roll(newt_ref[r0:r1, :], lax.rem(LANES - n * steps, LANES), axis=1)


def _shift_window(cache_rows, new_rows, steps):
    w = cache_rows.shape[-1]
    comb = jnp.concatenate([cache_rows, new_rows], axis=1)
    return comb, pltpu.roll(comb, w + LANES - steps, axis=1)[:, :w]


def _attn_sample_kernel(q_ref, newt_ref, c0_ref, c1_ref, c2_ref, base0_ref, base1_ref, base2_ref,
                        attn_ref, b0_ref, b1_ref, b2_ref, *, steps):
    n = pl.program_id(0)
    rows = HPG * steps

    @pl.when(n == 0)
    def _():
        for base_ref, b_ref in ((base0_ref, b0_ref), (base1_ref, b1_ref), (base2_ref, b2_ref)):
            for h in range(HPG):
                row = jnp.broadcast_to(base_ref[h:h + 1, :], (SUBLANES, base_ref.shape[-1]))
                b_ref[h * steps:(h + 1) * steps, :] = pltpu.roll(row, 0, 1, stride=1, stride_axis=0)[:steps]

    row_head = lax.broadcasted_iota(jnp.int32, (rows, GROUP_CH), 0) // steps
    col_head = lax.broadcasted_iota(jnp.int32, (rows, GROUP_CH), 1) // HEAD_DIM
    step_masks = _head_masks(steps)
    outs, lses = [], []
    for g, (c_ref, b_ref) in enumerate(zip((c0_ref, c1_ref, c2_ref), (b0_ref, b1_ref, b2_ref))):
        new = _new_steps(newt_ref, 2 * g * GROUP_CH, 2 * (g + 1) * GROUP_CH, n, steps)
        comb_bf = jnp.concatenate([c_ref[0], new], axis=1).astype(BF16)
        q = q_ref[0][:, g * GROUP_CH:(g + 1) * GROUP_CH]
        q_bd = jnp.where(row_head == col_head, jnp.concatenate([q] * HPG, axis=0), 0.0).astype(BF16)
        s = _mm(q_bd, comb_bf[:GROUP_CH]) * (HEAD_DIM ** -0.5) + b_ref[...]
        m = jnp.max(s, axis=-1, keepdims=True)
        e = jnp.exp(s - m)
        l = jnp.sum(e, axis=-1, keepdims=True)
        p = (e * (1.0 / l)).astype(BF16)
        o_all = _mm_nt(p, comb_bf[GROUP_CH:])
        lse_all = jnp.broadcast_to(m + jnp.log(l), (rows, GROUP_CH))
        o = jnp.zeros((steps, GROUP_CH), F32)
        lse = jnp.zeros((steps, GROUP_CH), F32)
        for h in range(HPG):
            o = jnp.where(step_masks[h], o_all[h * steps:(h + 1) * steps], o)
            lse = jnp.where(step_masks[h], lse_all[h * steps:(h + 1) * steps], lse)
        outs.append(o)
        lses.append(lse)
    attn_ref[0] = _mix_groups(outs, lses)


def _attn_sample(q, new_t, caches, biases, steps):
    nb = q.shape[0]
    per_n = lambda n: (n, 0, 0)
    cache_specs = [pl.BlockSpec((1, 2 * GROUP_CH, w), per_n) for w in WINDOWS]
    return pl.pallas_call(
        functools.partial(_attn_sample_kernel, steps=steps),
        grid=(nb,),
        in_specs=[pl.BlockSpec((1, steps, N_GROUPS * GROUP_CH), per_n),
                  _const_spec((2 * N_GROUPS * GROUP_CH, LANES))] + cache_specs
                 + [_const_spec((HPG, w + LANES)) for w in WINDOWS],
        out_specs=pl.BlockSpec((1, steps, GROUP_CH), per_n),
        out_shape=jax.ShapeDtypeStruct((nb, steps, GROUP_CH), F32),
        scratch_shapes=[pltpu.VMEM((HPG * steps, w + LANES), F32) for w in WINDOWS],
        compiler_params=_params(1),
        name="l0_attn_sample",
    )(q, new_t, *caches, *biases)


def _in1_sample_kernel(x_ref, mod_ref, g_ref, w_ref, cbuf_ref, cw_ref, cb_ref, lg_ref, lb_ref, coef_ref, bs_ref,
                       wo_ref, y_ref, cst_ref, dv_ref, ext_ref, *, nb, steps):
    rows = nb * steps
    hist = (C_CONV - 1) * nb
    x = x_ref[0]
    h = _rms_mod(x, g_ref[...], _mod(mod_ref, 1, rows), _mod(mod_ref, 0, rows)).astype(BF16)
    proj = _mm(h, w_ref[...])
    ext_ref[0:hist, :] = cbuf_ref[...]
    ext_ref[hist:hist + rows, :] = proj[:, C_CH:2 * C_CH] * proj[:, 2 * C_CH:3 * C_CH]
    conv = jnp.broadcast_to(cb_ref[...], (rows, C_CH))
    for k in range(C_CONV):
        conv = conv + cw_ref[k:k + 1, :] * ext_ref[k * nb:k * nb + rows, :]
    yc = proj[:, :C_CH] * conv
    cst_ref[...] = ext_ref[rows:rows + hist, :]
    uv = jax.nn.gelu(proj[:, 3 * C_CH:], approximate=True)
    v = _layernorm(uv[:, D_CH:], lg_ref[...], lb_ref[...])
    dv_ref[...] = v
    v_r = v.astype(BF16).astype(F32)
    coef = coef_ref[...].astype(BF16).astype(F32)
    mixes = []
    for i in range(steps):
        mix = jnp.broadcast_to(bs_ref[i:i + 1, :], (nb, D_CH))
        for j in range(i + 1):
            mix = mix + coef[i * steps + j:i * steps + j + 1, :] * v_r[j * nb:(j + 1) * nb, :]
        mixes.append(mix)
    yd = uv[:, :D_CH] * jnp.concatenate(mixes, axis=0)
    cat = jnp.concatenate([yc.astype(BF16), yd.astype(BF16)], axis=1)
    y_ref[0] = x + _mod(mod_ref, 2, rows) * _mm(cat, wo_ref[...])


def _in1_sample(x, mod, g, w, cbuf_t, cw, cb, lg, lb, coef, bs_rows, wo, nb, steps):
    rows = nb * steps
    hist = (C_CONV - 1) * nb
    return pl.pallas_call(
        functools.partial(_in1_sample_kernel, nb=nb, steps=steps),
        grid=(1,),
        in_specs=[_const_spec((1, rows, D_MODEL)), _const_spec((1, 6, nb, D_MODEL)),
                  _const_spec((1, D_MODEL)), _const_spec((D_MODEL, O_IN)), _const_spec((hist, C_CH)),
                  _const_spec((C_CONV, C_CH)), _const_spec((1, C_CH)), _const_spec((1, D_CH)),
                  _const_spec((1, D_CH)), _const_spec((steps * steps, D_CH)), _const_spec((steps, D_CH)),
                  _const_spec((C_CH + D_CH, D_MODEL))],
        out_specs=[_const_spec((1, rows, D_MODEL)), _const_spec((hist, C_CH)), _const_spec((rows, D_CH))],
        out_shape=[jax.ShapeDtypeStruct((1, rows, D_MODEL), F32), jax.ShapeDtypeStruct((hist, C_CH), F32),
                   jax.ShapeDtypeStruct((rows, D_CH), F32)],
        scratch_shapes=[pltpu.VMEM((hist + rows, C_CH), F32)],
        compiler_params=_params(1),
        name="l1_mixer_sample",
    )(x, mod, g, w, cbuf_t, cw, cb, lg, lb, coef, bs_rows, wo)


def _t5_bucket(dist):
    n = np.maximum(np.asarray(dist, dtype=np.int64), 0)
    max_exact = N_BUCKETS // 2
    large = max_exact + (np.log(np.maximum(n, 1) / max_exact) / np.log(MAX_DISTANCE / max_exact)
                         * (N_BUCKETS - max_exact)).astype(np.int32)
    return np.where(n < max_exact, n, np.minimum(large, N_BUCKETS - 1)).astype(np.int32)


def _prompt_bias_base(table, grp):
    rel = Q_BLOCK - np.arange(2 * Q_BLOCK)
    band = (rel >= 0) & (rel <= SUB_WINDOW)
    tab = table[:, grp * HPG:(grp + 1) * HPG]
    bias = tab[_t5_bucket(np.clip(rel, 0, None) * DILATIONS[grp])].T
    return jnp.where(band[None], bias, NEG).astype(F32)


def _sample_bias_base(table, grp):
    w, d = WINDOWS[grp], DILATIONS[grp]
    delta = w - np.arange(w + LANES)
    valid = (delta >= 0) & (delta % d == 0) & (delta // d <= SUB_WINDOW)
    tab = table[:, grp * HPG:(grp + 1) * HPG]
    return jnp.where(valid[None], tab[_t5_bucket(np.clip(delta, 0, None))].T, NEG).astype(F32)


def _step_major(x):
    return jnp.swapaxes(x, 0, 1).reshape((x.shape[0] * x.shape[1],) + x.shape[2:])


def _batch_major(x, nb, steps):
    return jnp.swapaxes(x.reshape((steps, nb) + x.shape[1:]), 0, 1)


def kernel(x_prompt, x_sample, c_prompt, c_sample, state_a_conv, cache_b_kv0, cache_b_kv1, cache_b_kv2, state_c_conv, rel_bias_table, ada_w, ada_b, norm_mix_g, norm_ffn_g, ffn_w_gate_up, ffn_w_down, final_norm_g, e_w_in, a_conv_w, a_conv_b, a_ln_g, a_ln_b, e_w_out, o_w_in, c_conv_w, c_conv_b, d_ln_g, d_ln_b, d_spatial_w, d_spatial_b, o_w_out):
    nb_p = x_prompt.shape[0]
    nb_s, steps, _ = x_sample.shape

    w_in0 = e_w_in[0].astype(BF16)
    w_out0 = e_w_out[0].astype(BF16)
    w_in1 = o_w_in[0].astype(BF16)
    w_out1 = o_w_out[0].astype(BF16)
    fin_g = final_norm_g.reshape(1, D_MODEL)

    mod = _ada(jnp.concatenate([c_prompt, c_sample], axis=0), ada_w, ada_b)
    mod = mod.reshape(mod.shape[0], nb_p + nb_s, 6, D_MODEL)
    mod_p = [mod[l, :nb_p].reshape(nb_p, 6, 1, D_MODEL) for l in range(2)]
    mod_s = [jnp.swapaxes(mod[l, nb_p:], 0, 1)[None] for l in range(2)]

    xs = _step_major(x_sample)[None]
    abuf_t = _step_major(state_a_conv[0])
    a_s, a_st_s, q_s, new_t = _in0_sample(xs, mod_s[0], norm_mix_g[0:1], w_in0, abuf_t, a_conv_w[0], a_conv_b,
                                          a_ln_g, a_ln_b, nb_s, steps)
    caches = [jnp.transpose(c[0], (0, 2, 3, 4, 1)).reshape(nb_s, 2 * GROUP_CH, c.shape[2])
              for c in (cache_b_kv0, cache_b_kv1, cache_b_kv2)]
    biases = [_sample_bias_base(rel_bias_table, grp) for grp in range(N_GROUPS)]
    attn_s = _attn_sample(_batch_major(q_s, nb_s, steps), new_t, caches, biases, steps)
    attn_s = _step_major(attn_s).astype(BF16)[None]
    (a, a_st_p, q0, q1, q2, kv0, kv1, kv2, st0, st1, st2) = _in0_prompt(
        x_prompt, mod_p[0], norm_mix_g[0:1], w_in0, a_conv_w[0], a_conv_b, a_ln_g, a_ln_b)
    base = jnp.stack([_prompt_bias_base(rel_bias_table, grp) for grp in range(N_GROUPS)])
    attn = _attn_prompt((q0, q1, q2), (kv0, kv1, kv2), base)
    xp, s2, xs = _ffn(x_prompt, mod_p[0], xs, mod_s[0], norm_ffn_g[0:1], ffn_w_gate_up, ffn_w_down, fin_g, 0, False,
                      (a, attn, a_s, attn_s, w_out0), (((caches[2], 2),), new_t, steps))

    bs_full = jnp.repeat(d_spatial_b[0].T, D_GROUP_CH, axis=1)
    xp, c_st_p = _in1_prompt(xp, mod_p[1], norm_mix_g[1:2], w_in1, c_conv_w[0], c_conv_b, d_ln_g, d_ln_b,
                             d_spatial_w[0], bs_full, w_out1)
    cbuf_t = _step_major(state_c_conv[0])
    coef = jnp.repeat(jnp.transpose(d_spatial_w[0][:, :steps, :steps], (1, 2, 0)).reshape(steps * steps, D_GROUPS),
                      D_GROUP_CH, axis=1)
    xs, c_st_s, dv_s = _in1_sample(xs, mod_s[1], norm_mix_g[1:2], w_in1, cbuf_t, c_conv_w[0], c_conv_b, d_ln_g,
                                   d_ln_b, coef, bs_full[:steps], w_out1, nb_s, steps)
    y_prompt, s0, s1, y_sample = _ffn(xp, mod_p[1], xs, mod_s[1], norm_ffn_g[1:2], ffn_w_gate_up, ffn_w_down, fin_g, 1, True,
                                      None, (((caches[0], 0), (caches[1], 1)), new_t, steps))

    def kv_state_prompt(st):
        return st.reshape(1, nb_p, st.shape[1], 2, HPG, HEAD_DIM)

    def kv_state_sample(st):
        return jnp.transpose(st.reshape(nb_s, 2, HPG, HEAD_DIM, st.shape[-1]), (0, 4, 1, 2, 3))[None]

    return (y_prompt, _batch_major(y_sample[0], nb_s, steps),
            a_st_p[None], _batch_major(a_st_s, nb_s, A_CONV - 1)[None],
            kv_state_prompt(st0), kv_state_sample(s0), kv_state_prompt(st1), kv_state_sample(s1),
            kv_state_prompt(st2), kv_state_sample(s2),
            c_st_p[None], _batch_major(c_st_s, nb_s, C_CONV - 1)[None],
            _batch_major(dv_s, nb_s, steps)[None])
```

```python
import functools

import numpy as np
import jax
import jax.numpy as jnp
from jax import lax
from jax.experimental import pallas as pl
from jax.experimental.pallas import tpu as pltpu

F32 = jnp.float32
BF16 = jnp.bfloat16

D_MODEL = 1024
EPS = 1e-6
A_CH = 256
A_CONV = 31
HEAD_DIM = 64
HPG = 4
GROUP_CH = HPG * HEAD_DIM
WINDOWS = (128, 512, 2048)
DILATIONS = (1, 4, 16)
N_GROUPS = 3
SUB_WINDOW = 128
Q_BLOCK = 128
N_BUCKETS = 32
MAX_DISTANCE = 2048
C_CH = 512
C_CONV = 3
D_CH = 512
D_GROUPS = 4
D_GROUP_CH = D_CH // D_GROUPS
D_CHUNK = 128
FFN_HIDDEN = 2816
Q_OFF = 2 * A_CH
K_OFF = Q_OFF + N_GROUPS * GROUP_CH
V_OFF = K_OFF + N_GROUPS * GROUP_CH
E_IN = V_OFF + N_GROUPS * GROUP_CH
O_IN = 3 * C_CH + 2 * D_CH
LANES = 128
SUBLANES = 8
GROUP_TILES = GROUP_CH // LANES
NEG = -1e30
ROW_TILE = 512
FFN_ROW_TILE = 512
CONV_ROWS = 64
SAMPLE_ROWS_PER_STEP = 2
SHIFT_ROWS = 64
VMEM_LIMIT = 56 * 1024 * 1024


def _params(n_axes):
    return pltpu.CompilerParams(dimension_semantics=("arbitrary",) * n_axes, vmem_limit_bytes=VMEM_LIMIT)


def _const_spec(shape):
    return pl.BlockSpec(shape, lambda *_: (0,) * len(shape), pipeline_mode=pl.Buffered(1))


def _layer_spec(shape, layer):
    return pl.BlockSpec((1,) + shape, lambda *_: (layer,) + (0,) * len(shape), pipeline_mode=pl.Buffered(1))


def _rms_mod(x, g, scale, shift):
    return (x * lax.rsqrt(jnp.mean(x * x, axis=-1, keepdims=True) + EPS)) * (g * (1.0 + scale)) + shift


def _mod(mod_ref, k, rows):
    m = mod_ref[0, k]
    reps = rows // m.shape[0] if m.shape[0] > 1 else 1
    return m if reps == 1 else jnp.concatenate([m] * reps, axis=0)


def _layernorm(x, g, b):
    mu = jnp.mean(x, axis=-1, keepdims=True)
    xc = x - mu
    var = jnp.mean(xc * xc, axis=-1, keepdims=True)
    return xc * lax.rsqrt(var + EPS) * g + b


def _silu(x):
    return x * jax.nn.sigmoid(x)


def _mm(a, b):
    return jnp.dot(a, b, preferred_element_type=F32)


def _mm_nt(a, b):
    return lax.dot_general(a, b, (((1,), (1,)), ((), ())), preferred_element_type=F32)


def _group_cols(proj, off, g):
    return proj[:, off + g * GROUP_CH:off + (g + 1) * GROUP_CH]


def _ada_kernel(c_ref, w_ref, b_ref, o_ref):
    cs = _silu(c_ref[...]).astype(BF16)
    o_ref[0] = _mm(cs, w_ref[0].astype(BF16)) + b_ref[0]


def _ada(c_all, ada_w, ada_b):
    depth, _, width = ada_w.shape
    nb = c_all.shape[0]
    tn = 1536
    return pl.pallas_call(
        _ada_kernel,
        grid=(depth, width // tn),
        in_specs=[_const_spec((nb, D_MODEL)),
                  pl.BlockSpec((1, D_MODEL, tn), lambda l, j: (l, 0, j)),
                  pl.BlockSpec((1, 1, tn), lambda l, j: (l, 0, j))],
        out_specs=pl.BlockSpec((1, nb, tn), lambda l, j: (l, 0, j)),
        out_shape=jax.ShapeDtypeStruct((depth, nb, width), F32),
        compiler_params=_params(2),
        name="ada_mod",
    )(c_all, ada_w, ada_b.reshape(depth, 1, width))


def _ordering_zero(x):
    bits = lax.shift_right_logical(lax.bitcast_convert_type(x, jnp.uint32), jnp.uint32(16))
    return lax.bitcast_convert_type(lax.shift_right_logical(bits, jnp.uint32(16)), F32)


def _conv_ln_silu(ext_ref, shift_ref, base, rows, wb_ref, cb_ref, lg_ref, lb_ref, taps, after):
    ch = cb_ref.shape[-1]
    zero = _ordering_zero(after[-SUBLANES:, :ch])
    acc = jnp.broadcast_to(cb_ref[...], (rows // SUBLANES, SUBLANES, ch))
    for k in range(taps):
        m = (base + k) % SUBLANES
        src = ext_ref if m == 0 else shift_ref.at[m - 1]
        x = src[pl.ds(base + k - m, rows), :].reshape(rows // SUBLANES, SUBLANES, ch)
        acc = acc + (wb_ref[k] + zero)[None] * x
    return _silu(_layernorm(acc.reshape(rows, ch), lg_ref[...], lb_ref[...]))


def _in0_prompt_kernel(x_ref, mod_ref, g_ref, w_ref, cw_ref, cb_ref, lg_ref, lb_ref,
                       a_ref, ast_ref, q0_ref, q1_ref, q2_ref, kv0_ref, kv1_ref, kv2_ref,
                       st0_ref, st1_ref, st2_ref, ext_ref, shift_ref, perm_ref, wb_ref, *, tm):
    t = pl.program_id(1)
    last = pl.num_programs(1) - 1
    halo = A_CONV - 1
    pad = 32

    @pl.when(t == 0)
    def _():
        ext_ref[0:pad, :] = jnp.zeros((pad, A_CH), F32)
        for k in range(A_CONV):
            wb_ref[k] = jnp.broadcast_to(cw_ref[k:k + 1, :], (SUBLANES, A_CH))

    h = _rms_mod(x_ref[0], g_ref[...], mod_ref[0, 1], mod_ref[0, 0]).astype(BF16)
    a_in = _mm(h, w_ref[:, :Q_OFF])
    ext_ref[pad:pad + tm, :] = a_in[:, :A_CH] * jax.nn.sigmoid(a_in[:, A_CH:])
    for m in range(1, SUBLANES):
        shift_ref[m - 1] = ext_ref[pl.ds(m, tm + pad - SUBLANES), :]

    def conv_chunk(c, after):
        r0 = c * CONV_ROWS
        y = _conv_ln_silu(ext_ref, shift_ref, pad - halo + r0, CONV_ROWS, wb_ref, cb_ref, lg_ref, lb_ref, A_CONV,
                          after)
        a_ref[0, r0:r0 + CONV_ROWS, :] = y.astype(BF16)

    def project(g, off, dst, col0, slot0):
        d = DILATIONS[g]
        val = _mm(h, w_ref[:, off + g * GROUP_CH:off + (g + 1) * GROUP_CH])
        if off != Q_OFF:
            st_refs[g][0, :, col0:col0 + GROUP_CH] = val[tm - min(WINDOWS[g], tm):]
        if d == 1:
            dst[0, 0, :, col0:col0 + GROUP_CH] = val.astype(BF16)
            return val
        for c in range(GROUP_TILES):
            tile = perm_ref.at[slot0 + c]
            tile[...] = val[:, c * LANES:(c + 1) * LANES]
            for r in range(d):
                dst[0, r, :, col0 + c * LANES:col0 + (c + 1) * LANES] = (
                    tile[pl.ds(r, tm // d, stride=d), :].astype(BF16))
        return val

    q_refs, kv_refs, st_refs = (q0_ref, q1_ref, q2_ref), (kv0_ref, kv1_ref, kv2_ref), (st0_ref, st1_ref, st2_ref)
    tasks = [(g, off, (q_refs if off == Q_OFF else kv_refs)[g], GROUP_CH if off == V_OFF else 0)
             for g in range(N_GROUPS) for off in (Q_OFF, K_OFF, V_OFF)]
    n_chunks = tm // CONV_ROWS
    for i, (g, off, dst, col0) in enumerate(tasks):
        val = project(g, off, dst, col0, i * GROUP_TILES)
        if i < n_chunks:
            conv_chunk(i, val)
    for c in range(len(tasks), n_chunks):
        conv_chunk(c, val)
    ext_ref[pl.ds(pad - halo, halo), :] = ext_ref[pl.ds(tm + pad - halo, halo), :]

    @pl.when(t == last)
    def _():
        ast_ref[0] = ext_ref[pl.ds(pad - halo, halo), :]


def _in0_prompt(x, mod, g, w, cw, cb, lg, lb):
    nb, t, _ = x.shape
    tm = ROW_TILE
    assert t % tm == 0 and tm >= WINDOWS[1] and WINDOWS[2] == t
    row = lambda n, i: (n, i, 0)
    per_n = lambda n, i: (n, 0, 0)
    res = lambda n, i: (n, 0, i, 0)
    out_shape = [jax.ShapeDtypeStruct((nb, t, A_CH), BF16), jax.ShapeDtypeStruct((nb, A_CONV - 1, A_CH), F32)]
    out_specs = [pl.BlockSpec((1, tm, A_CH), row), pl.BlockSpec((1, A_CONV - 1, A_CH), per_n)]
    for width in (GROUP_CH, 2 * GROUP_CH):
        for d in DILATIONS:
            out_shape.append(jax.ShapeDtypeStruct((nb, d, t // d, width), BF16))
            out_specs.append(pl.BlockSpec((1, d, tm // d, width), res))
    out_shape += [jax.ShapeDtypeStruct((nb, WINDOWS[0], 2 * GROUP_CH), F32),
                  jax.ShapeDtypeStruct((nb, WINDOWS[1], 2 * GROUP_CH), F32),
                  jax.ShapeDtypeStruct((nb, t, 2 * GROUP_CH), F32)]
    out_specs += [pl.BlockSpec((1, WINDOWS[0], 2 * GROUP_CH), per_n),
                  pl.BlockSpec((1, WINDOWS[1], 2 * GROUP_CH), per_n),
                  pl.BlockSpec((1, tm, 2 * GROUP_CH), row)]
    return pl.pallas_call(
        functools.partial(_in0_prompt_kernel, tm=tm),
        grid=(nb, t // tm),
        in_specs=[pl.BlockSpec((1, tm, D_MODEL), row),
                  pl.BlockSpec((1, 6, 1, D_MODEL), lambda n, i: (n, 0, 0, 0)),
                  _const_spec((1, D_MODEL)), _const_spec((D_MODEL, E_IN)),
                  _const_spec((A_CONV, A_CH)), _const_spec((1, A_CH)),
                  _const_spec((1, A_CH)), _const_spec((1, A_CH))],
        out_specs=out_specs,
        out_shape=out_shape,
        scratch_shapes=[pltpu.VMEM((tm + 32, A_CH), F32), pltpu.VMEM((SUBLANES - 1, tm + 32 - SUBLANES, A_CH), F32),
                        pltpu.VMEM((3 * N_GROUPS * GROUP_TILES, tm, LANES), F32),
                        pltpu.VMEM((A_CONV, SUBLANES, A_CH), F32)],
        compiler_params=_params(2),
        name="l0_in_prompt",
    )(x, mod, g, w, cw, cb, lg, lb)


def _head_masks(rows):
    col_head = lax.broadcasted_iota(jnp.int32, (rows, GROUP_CH), 1) // HEAD_DIM
    return [col_head == h for h in range(HPG)]


def _mix_groups(outs, lses):
    m = jnp.maximum(jnp.maximum(lses[0], lses[1]), lses[2])
    es = [jnp.exp(l - m) for l in lses]
    inv = 1.0 / (es[0] + es[1] + es[2])
    return (es[0] * inv) * outs[0] + (es[1] * inv) * outs[1] + (es[2] * inv) * outs[2]


def _attn_prompt_kernel(q0_ref, kv0_ref, q1_ref, kv1_ref, q2_ref, kv2_ref, base_ref, attn_ref,
                        bias_ref, o_ref, lse_ref, *, t):
    masks = _head_masks(Q_BLOCK)
    masks_bf = [jnp.where(m, HEAD_DIM ** -0.5, 0.0).astype(BF16) for m in masks]
    key_idx = lax.broadcasted_iota(jnp.int32, (HPG * Q_BLOCK, 2 * Q_BLOCK), 1)

    @pl.when(pl.program_id(0) == 0)
    def _():
        for g in range(N_GROUPS):
            for h in range(HPG):
                row = jnp.broadcast_to(base_ref[g, h:h + 1, :], (Q_BLOCK, 2 * Q_BLOCK))
                bias_ref[g, h * Q_BLOCK:(h + 1) * Q_BLOCK, :] = pltpu.roll(row, 0, 1, stride=1, stride_axis=0)

    for g, d, q_ref, kv_ref in ((1, DILATIONS[1], q1_ref, kv1_ref), (2, DILATIONS[2], q2_ref, kv2_ref),
                                (0, DILATIONS[0], q0_ref, kv0_ref)):
        nblk = t // d // Q_BLOCK
        single = nblk == 1

        def block(j, carry, g=g, d=d, q_ref=q_ref, kv_ref=kv_ref, nblk=nblk, single=single):
            r = j // nblk
            i = j % nblk
            rows = pl.ds(pl.multiple_of(i * Q_BLOCK, Q_BLOCK), Q_BLOCK)
            q = q_ref[0, r, rows, :]
            qs = jnp.concatenate([q * mb for mb in masks_bf], axis=0)
            if single:
                kk = kv_ref[0, r, rows, :]
                bias = bias_ref[g, :, Q_BLOCK:]
            else:
                prev = pl.ds(pl.multiple_of(jnp.maximum(i - 1, 0) * Q_BLOCK, Q_BLOCK), Q_BLOCK)
                kk = jnp.concatenate([kv_ref[0, r, prev, :], kv_ref[0, r, rows, :]], axis=0)
                bias = bias_ref[g] + jnp.where(key_idx < jnp.where(i == 0, Q_BLOCK, 0), NEG, 0.0)
            s = _mm_nt(qs, kk[:, :GROUP_CH]) + bias
            m = jnp.max(s, axis=-1, keepdims=True)
            e = jnp.exp(s - m)
            l = jnp.sum(e, axis=-1, keepdims=True)
            o_all = _mm((e * (1.0 / l)).astype(BF16), kk[:, GROUP_CH:])
            lse_all = jnp.broadcast_to(m + jnp.log(l), (HPG * Q_BLOCK, GROUP_CH))
            o = o_all[:Q_BLOCK]
            lse = lse_all[:Q_BLOCK]
            for h in range(1, HPG):
                o = jnp.where(masks[h], o_all[h * Q_BLOCK:(h + 1) * Q_BLOCK], o)
                lse = jnp.where(masks[h], lse_all[h * Q_BLOCK:(h + 1) * Q_BLOCK], lse)
            for c in range(GROUP_TILES):
                cols = slice(c * LANES, (c + 1) * LANES)
                if d == 1:
                    outs = [o[:, cols]] + [o_ref[k, c, rows, :] for k in range(N_GROUPS - 1)]
                    lses = [lse[:, cols]] + [lse_ref[k, c, rows, :] for k in range(N_GROUPS - 1)]
                    attn_ref[0, rows, cols] = _mix_groups(outs, lses).astype(BF16)
                else:
                    tokens = pl.ds(r + i * (Q_BLOCK * d), Q_BLOCK, stride=d)
                    o_ref[g - 1, c, tokens, :] = o[:, cols]
                    lse_ref[g - 1, c, tokens, :] = lse[:, cols]
            return carry

        lax.fori_loop(0, d * nblk, block, 0, unroll=16)


def _attn_prompt(qs, kvs, base):
    nb, _, t, _ = qs[0].shape
    blk = lambda n: (n, 0, 0, 0)
    in_specs = []
    for d in DILATIONS:
        in_specs += [pl.BlockSpec((1, d, t // d, GROUP_CH), blk), pl.BlockSpec((1, d, t // d, 2 * GROUP_CH), blk)]
    operands = [x for pair in zip(qs, kvs) for x in pair]
    return pl.pallas_call(
        functools.partial(_attn_prompt_kernel, t=t),
        grid=(nb,),
        in_specs=in_specs + [_const_spec((N_GROUPS, HPG, 2 * Q_BLOCK))],
        out_specs=pl.BlockSpec((1, t, GROUP_CH), lambda n: (n, 0, 0)),
        out_shape=jax.ShapeDtypeStruct((nb, t, GROUP_CH), BF16),
        scratch_shapes=[pltpu.VMEM((N_GROUPS, HPG * Q_BLOCK, 2 * Q_BLOCK), F32),
                        pltpu.VMEM((N_GROUPS - 1, GROUP_TILES, t, LANES), F32),
                        pltpu.VMEM((N_GROUPS - 1, GROUP_TILES, t, LANES), F32)],
        compiler_params=_params(1),
        name="l0_attn_prompt",
    )(*operands, base)


def _ffn_rows(x, mod_ref, g_ref, wgu_ref, wd_ref, fg_ref, mixer, wo_ref, final, chunk):
    rows = x.shape[0]
    if mixer is not None:
        a, attn = mixer
        x = x + _mod(mod_ref, 2, rows) * _mm(jnp.concatenate([a, attn], axis=1), wo_ref[...])
    h = _rms_mod(x, g_ref[...], _mod(mod_ref, 4, rows), _mod(mod_ref, 3, rows)).astype(BF16)
    acc = jnp.zeros(x.shape, F32)
    for c0 in range(0, FFN_HIDDEN, chunk):
        c1 = min(c0 + chunk, FFN_HIDDEN)
        gate = _mm(h, wgu_ref[0, :, c0:c1])
        up = _mm(h, wgu_ref[0, :, FFN_HIDDEN + c0:FFN_HIDDEN + c1])
        acc = acc + _mm((_silu(gate) * up).astype(BF16), wd_ref[0, c0:c1, :])
    y = x + _mod(mod_ref, 5, rows) * acc
    if final:
        y = y * lax.rsqrt(jnp.mean(y * y, axis=-1, keepdims=True) + EPS) * fg_ref[...]
    return y


def _ffn_kernel(*refs, final, chunk, mixer_out, n_shift, shift_steps, n_prompt_steps):
    refs = list(refs)
    x_ref, mod_ref, g_ref, wgu_ref, wd_ref, fg_ref = refs[:6]
    pos = 6
    a_ref = attn_ref = wo_ref = as_ref = attns_ref = None
    if mixer_out:
        a_ref, attn_ref, wo_ref = refs[pos:pos + 3]
        pos += 3
    shift_in = refs[pos:pos + 2 * n_shift]
    pos += 2 * n_shift
    xs_ref, mods_ref = refs[pos:pos + 2]
    pos += 2
    if mixer_out:
        as_ref, attns_ref = refs[pos:pos + 2]
        pos += 2
    y_ref = refs[pos]
    state_refs = refs[pos + 1:pos + 1 + n_shift]
    ys_ref = refs[pos + 1 + n_shift]
    step = pl.program_id(0)

    @pl.when(step < n_prompt_steps)
    def _():
        for k, state_ref in enumerate(state_refs):
            cache_ref, newt_ref = shift_in[2 * k], shift_in[2 * k + 1]
            for r0 in range(0, 2 * GROUP_CH, SHIFT_ROWS):
                new = _new_steps(newt_ref, r0, r0 + SHIFT_ROWS, step, shift_steps)
                state_ref[0, r0:r0 + SHIFT_ROWS, :] = _shift_window(cache_ref[0, r0:r0 + SHIFT_ROWS, :], new,
                                                                    shift_steps)[1]
        mixer = (a_ref[0], attn_ref[0]) if mixer_out else None
        y_ref[0] = _ffn_rows(x_ref[0], mod_ref, g_ref, wgu_ref, wd_ref, fg_ref, mixer, wo_ref, final, chunk)

    @pl.when(step == n_prompt_steps)
    def _():
        mixer = (as_ref[0], attns_ref[0]) if mixer_out else None
        ys_ref[0] = _ffn_rows(xs_ref[0], mods_ref, g_ref, wgu_ref, wd_ref, fg_ref, mixer, wo_ref, final, chunk)


def _ffn(x, mod, xs, mod_s, g, wgu, wd, final_g, layer, final, mixer_out=None, shift=None):
    nb, t, _ = x.shape
    tm = min(FFN_ROW_TILE, t)
    n_tiles = t // tm
    n_steps = nb * n_tiles
    rows_s = xs.shape[1]

    def prompt_map(*tail):
        def index_map(s):
            p = jnp.minimum(s, n_steps - 1)
            return (p // n_tiles, p % n_tiles) + tail
        return index_map

    per_n = lambda s: (jnp.minimum(s, n_steps - 1) // n_tiles, 0, 0, 0)
    in_specs = [pl.BlockSpec((1, tm, D_MODEL), prompt_map(0)),
                pl.BlockSpec((1, 6, 1, D_MODEL), per_n),
                _const_spec((1, D_MODEL)),
                _layer_spec((D_MODEL, 2 * FFN_HIDDEN), layer), _layer_spec((FFN_HIDDEN, D_MODEL), layer),
                _const_spec((1, D_MODEL))]
    operands = [x, mod, g, wgu, wd, final_g]
    sample_specs = [_const_spec((1, rows_s, D_MODEL)), _const_spec((1, 6) + mod_s.shape[2:])]
    sample_operands = [xs, mod_s]
    if mixer_out is not None:
        a, attn, a_s, attn_s, wo = mixer_out
        in_specs += [pl.BlockSpec((1, tm, A_CH), prompt_map(0)), pl.BlockSpec((1, tm, GROUP_CH), prompt_map(0)),
                     _const_spec((A_CH + GROUP_CH, D_MODEL))]
        operands += [a, attn, wo]
        sample_specs += [_const_spec((1, rows_s, A_CH)), _const_spec((1, rows_s, GROUP_CH))]
        sample_operands += [a_s, attn_s]
    out_specs = [pl.BlockSpec((1, tm, D_MODEL), prompt_map(0))]
    out_shape = [jax.ShapeDtypeStruct((nb, t, D_MODEL), F32)]
    shift_groups, shift_steps = (), 0
    if shift is not None:
        shift_groups, new_t, shift_steps = shift
        per_step = lambda s: (jnp.minimum(s, n_steps - 1), 0, 0)
        for cache, grp in shift_groups:
            assert cache.shape[0] == n_steps
            in_specs += [pl.BlockSpec((1,) + cache.shape[1:], per_step),
                         pl.BlockSpec((2 * GROUP_CH, LANES), lambda s, grp=grp: (grp, 0),
                                      pipeline_mode=pl.Buffered(1))]
            operands += [cache, new_t]
            out_specs.append(pl.BlockSpec((1,) + cache.shape[1:], per_step))
            out_shape.append(jax.ShapeDtypeStruct(cache.shape, F32))
    out_specs.append(_const_spec((1, rows_s, D_MODEL)))
    out_shape.append(jax.ShapeDtypeStruct((1, rows_s, D_MODEL), F32))
    return pl.pallas_call(
        functools.partial(_ffn_kernel, final=final, chunk=512, mixer_out=mixer_out is not None,
                          n_shift=len(shift_groups), shift_steps=shift_steps, n_prompt_steps=n_steps),
        grid=(n_steps + 1,),
        in_specs=in_specs + sample_specs,
        out_specs=out_specs,
        out_shape=out_shape,
        compiler_params=_params(1),
        name="ffn_final" if final else "l0_out_ffn",
    )(*operands, *sample_operands)


def _spatial_tril(ws_ref, grp):
    i = lax.broadcasted_iota(jnp.int32, (D_CHUNK, D_CHUNK), 0)
    j = lax.broadcasted_iota(jnp.int32, (D_CHUNK, D_CHUNK), 1)
    return jnp.where(j <= i, ws_ref[grp], 0.0).astype(BF16)


def _in1_prompt_kernel(x_ref, mod_ref, g_ref, w_ref, cw_ref, cb_ref, lg_ref, lb_ref, ws_ref, bs_ref, wo_ref,
                       y_ref, cst_ref, ext_ref, cat_ref, *, tm):
    t = pl.program_id(1)
    last = pl.num_programs(1) - 1
    halo = C_CONV - 1
    pad = 8

    @pl.when(t == 0)
    def _():
        ext_ref[0:pad, :] = jnp.zeros((pad, C_CH), F32)

    x = x_ref[0]
    h = _rms_mod(x, g_ref[...], mod_ref[0, 1], mod_ref[0, 0]).astype(BF16)
    v_in, u_in, xi, cg, bg = (_mm(h, w_ref[:, c0:c0 + C_CH]) for c0 in range(O_IN - C_CH, -1, -C_CH))
    ext_ref[pad:pad + tm, :] = cg * xi
    conv = jnp.broadcast_to(cb_ref[...], (tm, C_CH))
    for k in range(C_CONV):
        conv = conv + cw_ref[k:k + 1, :] * ext_ref[pl.ds(pad - halo + k, tm), :]
    cat_ref[:, :C_CH] = (bg * conv).astype(BF16)
    ext_ref[pl.ds(pad - halo, halo), :] = ext_ref[pl.ds(tm + pad - halo, halo), :]
    u = jax.nn.gelu(u_in, approximate=True)
    v = _layernorm(jax.nn.gelu(v_in, approximate=True), lg_ref[...], lb_ref[...]).astype(BF16)
    for grp in range(D_GROUPS):
        ws = _spatial_tril(ws_ref, grp)
        cols = slice(grp * D_GROUP_CH, (grp + 1) * D_GROUP_CH)
        for c in range(tm // D_CHUNK):
            rows = slice(c * D_CHUNK, (c + 1) * D_CHUNK)
            mix = _mm(ws, v[rows, cols]) + bs_ref[:, cols]
            cat_ref[rows, C_CH + grp * D_GROUP_CH:C_CH + (grp + 1) * D_GROUP_CH] = (u[rows, cols] * mix).astype(BF16)
    y_ref[0] = x + mod_ref[0, 2] * _mm(cat_ref[...], wo_ref[...])

    @pl.when(t == last)
    def _():
        cst_ref[0] = ext_ref[pl.ds(pad - halo, halo), :]


def _in1_prompt(x, mod, g, w, cw, cb, lg, lb, ws, bs_full, wo):
    nb, t, _ = x.shape
    tm = ROW_TILE
    row = lambda n, i: (n, i, 0)
    return pl.pallas_call(
        functools.partial(_in1_prompt_kernel, tm=tm),
        grid=(nb, t // tm),
        in_specs=[pl.BlockSpec((1, tm, D_MODEL), row),
                  pl.BlockSpec((1, 6, 1, D_MODEL), lambda n, i: (n, 0, 0, 0)),
                  _const_spec((1, D_MODEL)), _const_spec((D_MODEL, O_IN)),
                  _const_spec((C_CONV, C_CH)), _const_spec((1, C_CH)),
                  _const_spec((1, D_CH)), _const_spec((1, D_CH)),
                  _const_spec((D_GROUPS, D_CHUNK, D_CHUNK)), _const_spec((D_CHUNK, D_CH)),
                  _const_spec((C_CH + D_CH, D_MODEL))],
        out_specs=[pl.BlockSpec((1, tm, D_MODEL), row),
                   pl.BlockSpec((1, C_CONV - 1, C_CH), lambda n, i: (n, 0, 0))],
        out_shape=[jax.ShapeDtypeStruct((nb, t, D_MODEL), F32),
                   jax.ShapeDtypeStruct((nb, C_CONV - 1, C_CH), F32)],
        scratch_shapes=[pltpu.VMEM((tm + 8, C_CH), F32), pltpu.VMEM((tm, C_CH + D_CH), BF16)],
        compiler_params=_params(2),
        name="l1_mixer_prompt",
    )(x, mod, g, w, cw, cb, lg, lb, ws, bs_full, wo)


def _in0_sample_kernel(x_ref, mod_ref, g_ref, w_ref, abuf_ref, cw_ref, cb_ref, lg_ref, lb_ref,
                       a_ref, ast_ref, q_ref, newt_ref, ext_ref, perm_ref, *, nb, steps):
    rows = nb * steps
    hist = (A_CONV - 1) * nb
    h = _rms_mod(x_ref[0], g_ref[...], _mod(mod_ref, 1, rows), _mod(mod_ref, 0, rows)).astype(BF16)
    proj = _mm(h, w_ref[...])
    ext_ref[0:hist, :] = abuf_ref[...]
    ext_ref[hist:hist + rows, :] = proj[:, :A_CH] * jax.nn.sigmoid(proj[:, A_CH:2 * A_CH])
    acc = jnp.broadcast_to(cb_ref[...], (rows, A_CH))
    for k in range(A_CONV):
        acc = acc + cw_ref[k:k + 1, :] * ext_ref[k * nb:k * nb + rows, :]
    a_ref[0] = _silu(_layernorm(acc, lg_ref[...], lb_ref[...])).astype(BF16)
    ast_ref[...] = ext_ref[rows:rows + hist, :]
    q_ref[...] = proj[:, Q_OFF:K_OFF]
    tile = 0
    for g in range(N_GROUPS):
        for off in (K_OFF, V_OFF):
            for c in range(GROUP_TILES):
                col = off + g * GROUP_CH + c * LANES
                for s in range(steps):
                    perm_ref[tile, pl.ds(s, nb, stride=steps), :] = proj[s * nb:(s + 1) * nb, col:col + LANES]
                newt_ref[tile * LANES:(tile + 1) * LANES, :] = perm_ref[tile].T
                tile += 1


def _in0_sample(x, mod, g, w, abuf_t, cw, cb, lg, lb, nb, steps):
    rows = nb * steps
    assert rows == LANES
    hist = (A_CONV - 1) * nb
    return pl.pallas_call(
        functools.partial(_in0_sample_kernel, nb=nb, steps=steps),
        grid=(1,),
        in_specs=[_const_spec((1, rows, D_MODEL)), _const_spec((1, 6, nb, D_MODEL)),
                  _const_spec((1, D_MODEL)), _const_spec((D_MODEL, E_IN)), _const_spec((hist, A_CH)),
                  _const_spec((A_CONV, A_CH)), _const_spec((1, A_CH)), _const_spec((1, A_CH)),
                  _const_spec((1, A_CH))],
        out_specs=[_const_spec((1, rows, A_CH)), _const_spec((hist, A_CH)),
                   _const_spec((rows, N_GROUPS * GROUP_CH)), _const_spec((2 * N_GROUPS * GROUP_CH, LANES))],
        out_shape=[jax.ShapeDtypeStruct((1, rows, A_CH), BF16), jax.ShapeDtypeStruct((hist, A_CH), F32),
                   jax.ShapeDtypeStruct((rows, N_GROUPS * GROUP_CH), F32),
                   jax.ShapeDtypeStruct((2 * N_GROUPS * GROUP_CH, LANES), F32)],
        scratch_shapes=[pltpu.VMEM((hist + rows, A_CH), F32),
                        pltpu.VMEM((2 * N_GROUPS * GROUP_TILES, LANES, LANES), F32)],
        compiler_params=_params(1),
        name="l0_in_sample",
    )(x, mod, g, w, abuf_t, cw, cb, lg, lb)


def _new_steps(newt_ref, r0, r1, n, steps):
    return pltpu.roll(newt_ref[r0:r1, :], lax.rem(LANES - n * steps, LANES), axis=1)


def _shift_window(cache_rows, new_rows, steps):
    w = cache_rows.shape[-1]
    comb = jnp.concatenate([cache_rows, new_rows], axis=1)
    return comb, pltpu.roll(comb, w + LANES - steps, axis=1)[:, :w]


def _attn_sample_kernel(q_ref, newt_ref, c0_ref, c1_ref, c2_ref, base0_ref, base1_ref, base2_ref,
                        attn_ref, b0_ref, b1_ref, b2_ref, *, steps):
    step = pl.program_id(0)
    rows = HPG * steps

    @pl.when(step == 0)
    def _():
        for base_ref, b_ref in ((base0_ref, b0_ref), (base1_ref, b1_ref), (base2_ref, b2_ref)):
            for h in range(HPG):
                row = jnp.broadcast_to(base_ref[h:h + 1, :], (SUBLANES, base_ref.shape[-1]))
                b_ref[h * steps:(h + 1) * steps, :] = pltpu.roll(row, 0, 1, stride=1, stride_axis=0)[:steps]

    row_head = lax.broadcasted_iota(jnp.int32, (rows, GROUP_CH), 0) // steps
    col_head = lax.broadcasted_iota(jnp.int32, (rows, GROUP_CH), 1) // HEAD_DIM
    step_masks = _head_masks(steps)
    for j in range(SAMPLE_ROWS_PER_STEP):
        n = step * SAMPLE_ROWS_PER_STEP + j
        outs, lses = [], []
        for g, (c_ref, b_ref) in enumerate(zip((c0_ref, c1_ref, c2_ref), (b0_ref, b1_ref, b2_ref))):
            new = _new_steps(newt_ref, 2 * g * GROUP_CH, 2 * (g + 1) * GROUP_CH, n, steps)
            comb_bf = jnp.concatenate([c_ref[j], new], axis=1).astype(BF16)
            q = q_ref[j][:, g * GROUP_CH:(g + 1) * GROUP_CH]
            q_bd = jnp.where(row_head == col_head, jnp.concatenate([q] * HPG, axis=0), 0.0).astype(BF16)
            s = _mm(q_bd, comb_bf[:GROUP_CH]) * (HEAD_DIM ** -0.5) + b_ref[...]
            m = jnp.max(s, axis=-1, keepdims=True)
            e = jnp.exp(s - m)
            l = jnp.sum(e, axis=-1, keepdims=True)
            p = (e * (1.0 / l)).astype(BF16)
            o_all = _mm_nt(p, comb_bf[GROUP_CH:])
            lse_all = jnp.broadcast_to(m + jnp.log(l), (rows, GROUP_CH))
            o = jnp.zeros((steps, GROUP_CH), F32)
            lse = jnp.zeros((steps, GROUP_CH), F32)
            for h in range(HPG):
                o = jnp.where(step_masks[h], o_all[h * steps:(h + 1) * steps], o)
                lse = jnp.where(step_masks[h], lse_all[h * steps:(h + 1) * steps], lse)
            outs.append(o)
            lses.append(lse)
        attn_ref[j] = _mix_groups(outs, lses)


def _attn_sample(q, new_t, caches, biases, steps):
    nb = q.shape[0]
    rps = SAMPLE_ROWS_PER_STEP
    assert nb % rps == 0
    per_n = lambda n: (n, 0, 0)
    cache_specs = [pl.BlockSpec((rps, 2 * GROUP_CH, w), per_n) for w in WINDOWS]
    return pl.pallas_call(
        functools.partial(_attn_sample_kernel, steps=steps),
        grid=(nb // rps,),
        in_specs=[pl.BlockSpec((rps, steps, N_GROUPS * GROUP_CH), per_n),
                  _const_spec((2 * N_GROUPS * GROUP_CH, LANES))] + cache_specs
                 + [_const_spec((HPG, w + LANES)) for w in WINDOWS],
        out_specs=pl.BlockSpec((rps, steps, GROUP_CH), per_n),
        out_shape=jax.ShapeDtypeStruct((nb, steps, GROUP_CH), F32),
        scratch_shapes=[pltpu.VMEM((HPG * steps, w + LANES), F32) for w in WINDOWS],
        compiler_params=_params(1),
        name="l0_attn_sample",
    )(q, new_t, *caches, *biases)


def _in1_sample_kernel(x_ref, mod_ref, g_ref, w_ref, cbuf_ref, cw_ref, cb_ref, lg_ref, lb_ref, coef_ref, bs_ref,
                       wo_ref, y_ref, cst_ref, dv_ref, ext_ref, *, nb, steps):
    rows = nb * steps
    hist = (C_CONV - 1) * nb
    x = x_ref[0]
    h = _rms_mod(x, g_ref[...], _mod(mod_ref, 1, rows), _mod(mod_ref, 0, rows)).astype(BF16)
    proj = _mm(h, w_ref[...])
    ext_ref[0:hist, :] = cbuf_ref[...]
    ext_ref[hist:hist + rows, :] = proj[:, C_CH:2 * C_CH] * proj[:, 2 * C_CH:3 * C_CH]
    conv = jnp.broadcast_to(cb_ref[...], (rows, C_CH))
    for k in range(C_CONV):
        conv = conv + cw_ref[k:k + 1, :] * ext_ref[k * nb:k * nb + rows, :]
    yc = proj[:, :C_CH] * conv
    cst_ref[...] = ext_ref[rows:rows + hist, :]
    uv = jax.nn.gelu(proj[:, 3 * C_CH:], approximate=True)
    v = _layernorm(uv[:, D_CH:], lg_ref[...], lb_ref[...])
    dv_ref[...] = v
    v_r = v.astype(BF16).astype(F32)
    coef = coef_ref[...].astype(BF16).astype(F32)
    mixes = []
    for i in range(steps):
        mix = jnp.broadcast_to(bs_ref[i:i + 1, :], (nb, D_CH))
        for j in range(i + 1):
            mix = mix + coef[i * steps + j:i * steps + j + 1, :] * v_r[j * nb:(j + 1) * nb, :]
        mixes.append(mix)
    yd = uv[:, :D_CH] * jnp.concatenate(mixes, axis=0)
    cat = jnp.concatenate([yc.astype(BF16), yd.astype(BF16)], axis=1)
    y_ref[0] = x + _mod(mod_ref, 2, rows) * _mm(cat, wo_ref[...])


def _in1_sample(x, mod, g, w, cbuf_t, cw, cb, lg, lb, coef, bs_rows, wo, nb, steps):
    rows = nb * steps
    hist = (C_CONV - 1) * nb
    return pl.pallas_call(
        functools.partial(_in1_sample_kernel, nb=nb, steps=steps),
        grid=(1,),
        in_specs=[_const_spec((1, rows, D_MODEL)), _const_spec((1, 6, nb, D_MODEL)),
                  _const_spec((1, D_MODEL)), _const_spec((D_MODEL, O_IN)), _const_spec((hist, C_CH)),
                  _const_spec((C_CONV, C_CH)), _const_spec((1, C_CH)), _const_spec((1, D_CH)),
                  _const_spec((1, D_CH)), _const_spec((steps * steps, D_CH)), _const_spec((steps, D_CH)),
                  _const_spec((C_CH + D_CH, D_MODEL))],
        out_specs=[_const_spec((1, rows, D_MODEL)), _const_spec((hist, C_CH)), _const_spec((rows, D_CH))],
        out_shape=[jax.ShapeDtypeStruct((1, rows, D_MODEL), F32), jax.ShapeDtypeStruct((hist, C_CH), F32),
                   jax.ShapeDtypeStruct((rows, D_CH), F32)],
        scratch_shapes=[pltpu.VMEM((hist + rows, C_CH), F32)],
        compiler_params=_params(1),
        name="l1_mixer_sample",
    )(x, mod, g, w, cbuf_t, cw, cb, lg, lb, coef, bs_rows, wo)


def _t5_bucket(dist):
    n = np.maximum(np.asarray(dist, dtype=np.int64), 0)
    max_exact = N_BUCKETS // 2
    large = max_exact + (np.log(np.maximum(n, 1) / max_exact) / np.log(MAX_DISTANCE / max_exact)
                         * (N_BUCKETS - max_exact)).astype(np.int32)
    return np.where(n < max_exact, n, np.minimum(large, N_BUCKETS - 1)).astype(np.int32)


def _prompt_bias_base(table, grp):
    rel = Q_BLOCK - np.arange(2 * Q_BLOCK)
    band = (rel >= 0) & (rel <= SUB_WINDOW)
    tab = table[:, grp * HPG:(grp + 1) * HPG]
    bias = tab[_t5_bucket(np.clip(rel, 0, None) * DILATIONS[grp])].T
    return jnp.where(band[None], bias, NEG).astype(F32)


def _sample_bias_base(table, grp):
    w, d = WINDOWS[grp], DILATIONS[grp]
    delta = w - np.arange(w + LANES)
    valid = (delta >= 0) & (delta % d == 0) & (delta // d <= SUB_WINDOW)
    tab = table[:, grp * HPG:(grp + 1) * HPG]
    return jnp.where(valid[None], tab[_t5_bucket(np.clip(delta, 0, None))].T, NEG).astype(F32)


def _step_major(x):
    return jnp.swapaxes(x, 0, 1).reshape((x.shape[0] * x.shape[1],) + x.shape[2:])


def _batch_major(x, nb, steps):
    return jnp.swapaxes(x.reshape((steps, nb) + x.shape[1:]), 0, 1)


def kernel(x_prompt, x_sample, c_prompt, c_sample, state_a_conv, cache_b_kv0, cache_b_kv1, cache_b_kv2, state_c_conv, rel_bias_table, ada_w, ada_b, norm_mix_g, norm_ffn_g, ffn_w_gate_up, ffn_w_down, final_norm_g, e_w_in, a_conv_w, a_conv_b, a_ln_g, a_ln_b, e_w_out, o_w_in, c_conv_w, c_conv_b, d_ln_g, d_ln_b, d_spatial_w, d_spatial_b, o_w_out):
    nb_p = x_prompt.shape[0]
    nb_s, steps, _ = x_sample.shape

    w_in0 = e_w_in[0].astype(BF16)
    w_out0 = e_w_out[0].astype(BF16)
    w_in1 = o_w_in[0].astype(BF16)
    w_out1 = o_w_out[0].astype(BF16)
    w_gu = ffn_w_gate_up.astype(BF16)
    w_dn = ffn_w_down.astype(BF16)
    fin_g = final_norm_g.reshape(1, D_MODEL)

    mod = _ada(jnp.concatenate([c_prompt, c_sample], axis=0), ada_w, ada_b)
    mod = mod.reshape(mod.shape[0], nb_p + nb_s, 6, D_MODEL)
    mod_p = [mod[l, :nb_p].reshape(nb_p, 6, 1, D_MODEL) for l in range(2)]
    mod_s = [jnp.swapaxes(mod[l, nb_p:], 0, 1)[None] for l in range(2)]

    xs = _step_major(x_sample)[None]
    abuf_t = _step_major(state_a_conv[0])
    a_s, a_st_s, q_s, new_t = _in0_sample(xs, mod_s[0], norm_mix_g[0:1], w_in0, abuf_t, a_conv_w[0], a_conv_b,
                                          a_ln_g, a_ln_b, nb_s, steps)
    caches = [jnp.transpose(c[0], (0, 2, 3, 4, 1)).reshape(nb_s, 2 * GROUP_CH, c.shape[2])
              for c in (cache_b_kv0, cache_b_kv1, cache_b_kv2)]
    biases = [_sample_bias_base(rel_bias_table, grp) for grp in range(N_GROUPS)]
    attn_s = _attn_sample(_batch_major(q_s, nb_s, steps), new_t, caches, biases, steps)
    attn_s = _step_major(attn_s).astype(BF16)[None]
    (a, a_st_p, q0, q1, q2, kv0, kv1, kv2, st0, st1, st2) = _in0_prompt(
        x_prompt, mod_p[0], norm_mix_g[0:1], w_in0, a_conv_w[0], a_conv_b, a_ln_g, a_ln_b)
    base = jnp.stack([_prompt_bias_base(rel_bias_table, grp) for grp in range(N_GROUPS)])
    attn = _attn_prompt((q0, q1, q2), (kv0, kv1, kv2), base)
    xp, s2, xs = _ffn(x_prompt, mod_p[0], xs, mod_s[0], norm_ffn_g[0:1], w_gu, w_dn, fin_g, 0, False,
                      (a, attn, a_s, attn_s, w_out0), (((caches[2], 2),), new_t, steps))

    bs_full = jnp.repeat(d_spatial_b[0].T, D_GROUP_CH, axis=1)
    xp, c_st_p = _in1_prompt(xp, mod_p[1], norm_mix_g[1:2], w_in1, c_conv_w[0], c_conv_b, d_ln_g, d_ln_b,
                             d_spatial_w[0], bs_full, w_out1)
    cbuf_t = _step_major(state_c_conv[0])
    coef = jnp.repeat(jnp.transpose(d_spatial_w[0][:, :steps, :steps], (1, 2, 0)).reshape(steps * steps, D_GROUPS),
                      D_GROUP_CH, axis=1)
    xs, c_st_s, dv_s = _in1_sample(xs, mod_s[1], norm_mix_g[1:2], w_in1, cbuf_t, c_conv_w[0], c_conv_b, d_ln_g,
                                   d_ln_b, coef, bs_full[:steps], w_out1, nb_s, steps)
    y_prompt, s0, s1, y_sample = _ffn(xp, mod_p[1], xs, mod_s[1], norm_ffn_g[1:2], w_gu, w_dn, fin_g, 1, True,
                                      None, (((caches[0], 0), (caches[1], 1)), new_t, steps))

    def kv_state_prompt(st):
        return st.reshape(1, nb_p, st.shape[1], 2, HPG, HEAD_DIM)

    def kv_state_sample(st):
        return jnp.transpose(st.reshape(nb_s, 2, HPG, HEAD_DIM, st.shape[-1]), (0, 4, 1, 2, 3))[None]

    return (y_prompt, _batch_major(y_sample[0], nb_s, steps),
            a_st_p[None], _batch_major(a_st_s, nb_s, A_CONV - 1)[None],
            kv_state_prompt(st0), kv_state_sample(s0), kv_state_prompt(st1), kv_state_sample(s1),
            kv_state_prompt(st2), kv_state_sample(s2),
            c_st_p[None], _batch_major(c_st_s, nb_s, C_CONV - 1)[None],
            _batch_major(dv_s, nb_s, steps)[None])
```

```python
import functools

import numpy as np
import jax
import jax.numpy as jnp
from jax import lax
from jax.experimental import pallas as pl
from jax.experimental.pallas import tpu as pltpu

F32 = jnp.float32
BF16 = jnp.bfloat16

D_MODEL = 1024
EPS = 1e-6
A_CH = 256
A_CONV = 31
HEAD_DIM = 64
HPG = 4
GROUP_CH = HPG * HEAD_DIM
WINDOWS = (128, 512, 2048)
DILATIONS = (1, 4, 16)
N_GROUPS = 3
SUB_WINDOW = 128
Q_BLOCK = 128
N_BUCKETS = 32
MAX_DISTANCE = 2048
C_CH = 512
C_CONV = 3
D_CH = 512
D_GROUPS = 4
D_GROUP_CH = D_CH // D_GROUPS
D_CHUNK = 128
FFN_HIDDEN = 2816
Q_OFF = 2 * A_CH
K_OFF = Q_OFF + N_GROUPS * GROUP_CH
V_OFF = K_OFF + N_GROUPS * GROUP_CH
E_IN = V_OFF + N_GROUPS * GROUP_CH
O_IN = 3 * C_CH + 2 * D_CH
LANES = 128
SUBLANES = 8
GROUP_TILES = GROUP_CH // LANES
NEG = -1e30
ROW_TILE = 1024
FFN_ROW_TILE = 512
CONV_ROWS = 64
SAMPLE_ROWS_PER_STEP = 4
SHIFT_ROWS = 64
VMEM_LIMIT = 56 * 1024 * 1024


def _params(n_axes):
    return pltpu.CompilerParams(dimension_semantics=("arbitrary",) * n_axes, vmem_limit_bytes=VMEM_LIMIT)


def _const_spec(shape):
    return pl.BlockSpec(shape, lambda *_: (0,) * len(shape), pipeline_mode=pl.Buffered(1))


def _layer_spec(shape, layer):
    return pl.BlockSpec((1,) + shape, lambda *_: (layer,) + (0,) * len(shape), pipeline_mode=pl.Buffered(1))


def _rms_mod(x, g, scale, shift):
    return (x * lax.rsqrt(jnp.mean(x * x, axis=-1, keepdims=True) + EPS)) * (g * (1.0 + scale)) + shift


def _mod(mod_ref, k, rows):
    m = mod_ref[0, k]
    reps = rows // m.shape[0] if m.shape[0] > 1 else 1
    return m if reps == 1 else jnp.concatenate([m] * reps, axis=0)


def _layernorm(x, g, b):
    mu = jnp.mean(x, axis=-1, keepdims=True)
    xc = x - mu
    var = jnp.mean(xc * xc, axis=-1, keepdims=True)
    return xc * lax.rsqrt(var + EPS) * g + b


def _silu(x):
    return x * jax.nn.sigmoid(x)


def _mm(a, b):
    return jnp.dot(a, b, preferred_element_type=F32)


def _mm_nt(a, b):
    return lax.dot_general(a, b, (((1,), (1,)), ((), ())), preferred_element_type=F32)


def _group_cols(proj, off, g):
    return proj[:, off + g * GROUP_CH:off + (g + 1) * GROUP_CH]


def _ada_kernel(c_ref, w_ref, b_ref, o_ref):
    cs = _silu(c_ref[...]).astype(BF16)
    o_ref[0] = _mm(cs, w_ref[0].astype(BF16)) + b_ref[0]


def _ada(c_all, ada_w, ada_b):
    depth, _, width = ada_w.shape
    nb = c_all.shape[0]
    tn = 1536
    return pl.pallas_call(
        _ada_kernel,
        grid=(depth, width // tn),
        in_specs=[_const_spec((nb, D_MODEL)),
                  pl.BlockSpec((1, D_MODEL, tn), lambda l, j: (l, 0, j)),
                  pl.BlockSpec((1, 1, tn), lambda l, j: (l, 0, j))],
        out_specs=pl.BlockSpec((1, nb, tn), lambda l, j: (l, 0, j)),
        out_shape=jax.ShapeDtypeStruct((depth, nb, width), F32),
        compiler_params=_params(2),
        name="ada_mod",
    )(c_all, ada_w, ada_b.reshape(depth, 1, width))


def _ordering_zero(x):
    bits = lax.shift_right_logical(lax.bitcast_convert_type(x, jnp.uint32), jnp.uint32(16))
    return lax.bitcast_convert_type(lax.shift_right_logical(bits, jnp.uint32(16)), F32)


def _conv_ln_silu(ext_ref, shift_ref, base, rows, wb_ref, cb_ref, lg_ref, lb_ref, taps, after):
    ch = cb_ref.shape[-1]
    zero = _ordering_zero(after[-SUBLANES:, :ch])
    acc = jnp.broadcast_to(cb_ref[...], (rows // SUBLANES, SUBLANES, ch))
    for k in range(taps):
        m = (base + k) % SUBLANES
        src = ext_ref if m == 0 else shift_ref.at[m - 1]
        x = src[pl.ds(base + k - m, rows), :].reshape(rows // SUBLANES, SUBLANES, ch)
        acc = acc + (wb_ref[k] + zero)[None] * x
    return _silu(_layernorm(acc.reshape(rows, ch), lg_ref[...], lb_ref[...]))


def _in0_prompt_kernel(x_ref, mod_ref, g_ref, w_ref, cw_ref, cb_ref, lg_ref, lb_ref,
                       a_ref, ast_ref, q0_ref, q1_ref, q2_ref, kv0_ref, kv1_ref, kv2_ref,
                       st0_ref, st1_ref, st2_ref, ext_ref, shift_ref, perm_ref, wb_ref, *, tm):
    t = pl.program_id(1)
    last = pl.num_programs(1) - 1
    halo = A_CONV - 1
    pad = 32

    @pl.when(t == 0)
    def _():
        ext_ref[0:pad, :] = jnp.zeros((pad, A_CH), F32)
        for k in range(A_CONV):
            wb_ref[k] = jnp.broadcast_to(cw_ref[k:k + 1, :], (SUBLANES, A_CH))

    h = _rms_mod(x_ref[0], g_ref[...], mod_ref[0, 1], mod_ref[0, 0]).astype(BF16)
    a_in = _mm(h, w_ref[:, :Q_OFF])
    ext_ref[pad:pad + tm, :] = a_in[:, :A_CH] * jax.nn.sigmoid(a_in[:, A_CH:])
    for m in range(1, SUBLANES):
        shift_ref[m - 1] = ext_ref[pl.ds(m, tm + pad - SUBLANES), :]

    def conv_chunk(c, after):
        r0 = c * CONV_ROWS
        y = _conv_ln_silu(ext_ref, shift_ref, pad - halo + r0, CONV_ROWS, wb_ref, cb_ref, lg_ref, lb_ref, A_CONV,
                          after)
        a_ref[0, r0:r0 + CONV_ROWS, :] = y.astype(BF16)

    def project(g, off, dst, col0, slot0):
        d = DILATIONS[g]
        val = _mm(h, w_ref[:, off + g * GROUP_CH:off + (g + 1) * GROUP_CH])
        if off != Q_OFF:
            st_refs[g][0, :, col0:col0 + GROUP_CH] = val[tm - min(WINDOWS[g], tm):]
        if d == 1:
            dst[0, 0, :, col0:col0 + GROUP_CH] = val.astype(BF16)
            return val
        for c in range(GROUP_TILES):
            tile = perm_ref.at[slot0 + c]
            tile[...] = val[:, c * LANES:(c + 1) * LANES]
            for r in range(d):
                dst[0, r, :, col0 + c * LANES:col0 + (c + 1) * LANES] = (
                    tile[pl.ds(r, tm // d, stride=d), :].astype(BF16))
        return val

    q_refs, kv_refs, st_refs = (q0_ref, q1_ref, q2_ref), (kv0_ref, kv1_ref, kv2_ref), (st0_ref, st1_ref, st2_ref)
    tasks = [(g, off, (q_refs if off == Q_OFF else kv_refs)[g], GROUP_CH if off == V_OFF else 0)
             for g in range(N_GROUPS) for off in (Q_OFF, K_OFF, V_OFF)]
    n_chunks = tm // CONV_ROWS
    per_dot = -(-n_chunks // len(tasks))
    for i, (g, off, dst, col0) in enumerate(tasks):
        val = project(g, off, dst, col0, i * GROUP_TILES)
        for c in range(i * per_dot, min((i + 1) * per_dot, n_chunks)):
            conv_chunk(c, val)
    ext_ref[pl.ds(pad - halo, halo), :] = ext_ref[pl.ds(tm + pad - halo, halo), :]

    @pl.when(t == last)
    def _():
        ast_ref[0] = ext_ref[pl.ds(pad - halo, halo), :]


def _in0_prompt(x, mod, g, w, cw, cb, lg, lb):
    nb, t, _ = x.shape
    tm = ROW_TILE
    assert t % tm == 0 and tm >= WINDOWS[1] and WINDOWS[2] == t
    row = lambda n, i: (n, i, 0)
    per_n = lambda n, i: (n, 0, 0)
    res = lambda n, i: (n, 0, i, 0)
    out_shape = [jax.ShapeDtypeStruct((nb, t, A_CH), BF16), jax.ShapeDtypeStruct((nb, A_CONV - 1, A_CH), F32)]
    out_specs = [pl.BlockSpec((1, tm, A_CH), row), pl.BlockSpec((1, A_CONV - 1, A_CH), per_n)]
    for width in (GROUP_CH, 2 * GROUP_CH):
        for d in DILATIONS:
            out_shape.append(jax.ShapeDtypeStruct((nb, d, t // d, width), BF16))
            out_specs.append(pl.BlockSpec((1, d, tm // d, width), res))
    out_shape += [jax.ShapeDtypeStruct((nb, WINDOWS[0], 2 * GROUP_CH), F32),
                  jax.ShapeDtypeStruct((nb, WINDOWS[1], 2 * GROUP_CH), F32),
                  jax.ShapeDtypeStruct((nb, t, 2 * GROUP_CH), F32)]
    out_specs += [pl.BlockSpec((1, WINDOWS[0], 2 * GROUP_CH), per_n),
                  pl.BlockSpec((1, WINDOWS[1], 2 * GROUP_CH), per_n),
                  pl.BlockSpec((1, tm, 2 * GROUP_CH), row)]
    return pl.pallas_call(
        functools.partial(_in0_prompt_kernel, tm=tm),
        grid=(nb, t // tm),
        in_specs=[pl.BlockSpec((1, tm, D_MODEL), row),
                  pl.BlockSpec((1, 6, 1, D_MODEL), lambda n, i: (n, 0, 0, 0)),
                  _const_spec((1, D_MODEL)), _const_spec((D_MODEL, E_IN)),
                  _const_spec((A_CONV, A_CH)), _const_spec((1, A_CH)),
                  _const_spec((1, A_CH)), _const_spec((1, A_CH))],
        out_specs=out_specs,
        out_shape=out_shape,
        scratch_shapes=[pltpu.VMEM((tm + 32, A_CH), F32), pltpu.VMEM((SUBLANES - 1, tm + 32 - SUBLANES, A_CH), F32),
                        pltpu.VMEM((3 * N_GROUPS * GROUP_TILES, tm, LANES), F32),
                        pltpu.VMEM((A_CONV, SUBLANES, A_CH), F32)],
        compiler_params=_params(2),
        name="l0_in_prompt",
    )(x, mod, g, w, cw, cb, lg, lb)


def _head_masks(rows):
    col_head = lax.broadcasted_iota(jnp.int32, (rows, GROUP_CH), 1) // HEAD_DIM
    return [col_head == h for h in range(HPG)]


def _mix_groups(outs, lses):
    m = jnp.maximum(jnp.maximum(lses[0], lses[1]), lses[2])
    es = [jnp.exp(l - m) for l in lses]
    inv = 1.0 / (es[0] + es[1] + es[2])
    return (es[0] * inv) * outs[0] + (es[1] * inv) * outs[1] + (es[2] * inv) * outs[2]


def _attn_prompt_kernel(q0_ref, kv0_ref, q1_ref, kv1_ref, q2_ref, kv2_ref, base_ref, attn_ref,
                        bias_ref, o_ref, lse_ref, *, t):
    masks = _head_masks(Q_BLOCK)
    masks_bf = [jnp.where(m, HEAD_DIM ** -0.5, 0.0).astype(BF16) for m in masks]
    key_idx = lax.broadcasted_iota(jnp.int32, (HPG * Q_BLOCK, 2 * Q_BLOCK), 1)

    @pl.when(pl.program_id(0) == 0)
    def _():
        for g in range(N_GROUPS):
            for h in range(HPG):
                row = jnp.broadcast_to(base_ref[g, h:h + 1, :], (Q_BLOCK, 2 * Q_BLOCK))
                bias_ref[g, h * Q_BLOCK:(h + 1) * Q_BLOCK, :] = pltpu.roll(row, 0, 1, stride=1, stride_axis=0)

    for g, d, q_ref, kv_ref in ((1, DILATIONS[1], q1_ref, kv1_ref), (2, DILATIONS[2], q2_ref, kv2_ref),
                                (0, DILATIONS[0], q0_ref, kv0_ref)):
        nblk = t // d // Q_BLOCK
        single = nblk == 1

        def block(j, carry, g=g, d=d, q_ref=q_ref, kv_ref=kv_ref, nblk=nblk, single=single):
            r = j // nblk
            i = j % nblk
            rows = pl.ds(pl.multiple_of(i * Q_BLOCK, Q_BLOCK), Q_BLOCK)
            q = q_ref[0, r, rows, :]
            qs = jnp.concatenate([q * mb for mb in masks_bf], axis=0)
            if single:
                kk = kv_ref[0, r, rows, :]
                bias = bias_ref[g, :, Q_BLOCK:]
            else:
                prev = pl.ds(pl.multiple_of(jnp.maximum(i - 1, 0) * Q_BLOCK, Q_BLOCK), Q_BLOCK)
                kk = jnp.concatenate([kv_ref[0, r, prev, :], kv_ref[0, r, rows, :]], axis=0)
                bias = bias_ref[g] + jnp.where(key_idx < jnp.where(i == 0, Q_BLOCK, 0), NEG, 0.0)
            s = _mm_nt(qs, kk[:, :GROUP_CH]) + bias
            m = jnp.max(s, axis=-1, keepdims=True)
            e = jnp.exp(s - m)
            l = jnp.sum(e, axis=-1, keepdims=True)
            o_all = _mm((e * (1.0 / l)).astype(BF16), kk[:, GROUP_CH:])
            lse_all = jnp.broadcast_to(m + jnp.log(l), (HPG * Q_BLOCK, GROUP_CH))
            o = o_all[:Q_BLOCK]
            lse = lse_all[:Q_BLOCK]
            for h in range(1, HPG):
                o = jnp.where(masks[h], o_all[h * Q_BLOCK:(h + 1) * Q_BLOCK], o)
                lse = jnp.where(masks[h], lse_all[h * Q_BLOCK:(h + 1) * Q_BLOCK], lse)
            for c in range(GROUP_TILES):
                cols = slice(c * LANES, (c + 1) * LANES)
                if d == 1:
                    outs = [o[:, cols]] + [o_ref[k, c, rows, :] for k in range(N_GROUPS - 1)]
                    lses = [lse[:, cols]] + [lse_ref[k, c, rows, :] for k in range(N_GROUPS - 1)]
                    attn_ref[0, rows, cols] = _mix_groups(outs, lses).astype(BF16)
                else:
                    tokens = pl.ds(r + i * (Q_BLOCK * d), Q_BLOCK, stride=d)
                    o_ref[g - 1, c, tokens, :] = o[:, cols]
                    lse_ref[g - 1, c, tokens, :] = lse[:, cols]
            return carry

        lax.fori_loop(0, d * nblk, block, 0, unroll=16)


def _attn_prompt(qs, kvs, base):
    nb, _, t, _ = qs[0].shape
    blk = lambda n: (n, 0, 0, 0)
    in_specs = []
    for d in DILATIONS:
        in_specs += [pl.BlockSpec((1, d, t // d, GROUP_CH), blk), pl.BlockSpec((1, d, t // d, 2 * GROUP_CH), blk)]
    operands = [x for pair in zip(qs, kvs) for x in pair]
    return pl.pallas_call(
        functools.partial(_attn_prompt_kernel, t=t),
        grid=(nb,),
        in_specs=in_specs + [_const_spec((N_GROUPS, HPG, 2 * Q_BLOCK))],
        out_specs=pl.BlockSpec((1, t, GROUP_CH), lambda n: (n, 0, 0)),
        out_shape=jax.ShapeDtypeStruct((nb, t, GROUP_CH), BF16),
        scratch_shapes=[pltpu.VMEM((N_GROUPS, HPG * Q_BLOCK, 2 * Q_BLOCK), F32),
                        pltpu.VMEM((N_GROUPS - 1, GROUP_TILES, t, LANES), F32),
                        pltpu.VMEM((N_GROUPS - 1, GROUP_TILES, t, LANES), F32)],
        compiler_params=_params(1),
        name="l0_attn_prompt",
    )(*operands, base)


def _ffn_rows(x, mod_ref, g_ref, wgu_ref, wd_ref, fg_ref, mixer, wo_ref, final, chunk):
    rows = x.shape[0]
    if mixer is not None:
        a, attn = mixer
        x = x + _mod(mod_ref, 2, rows) * _mm(jnp.concatenate([a, attn], axis=1), wo_ref[...])
    h = _rms_mod(x, g_ref[...], _mod(mod_ref, 4, rows), _mod(mod_ref, 3, rows)).astype(BF16)
    acc = jnp.zeros(x.shape, F32)
    for c0 in range(0, FFN_HIDDEN, chunk):
        c1 = min(c0 + chunk, FFN_HIDDEN)
        gate = _mm(h, wgu_ref[0, :, c0:c1])
        up = _mm(h, wgu_ref[0, :, FFN_HIDDEN + c0:FFN_HIDDEN + c1])
        acc = acc + _mm((_silu(gate) * up).astype(BF16), wd_ref[0, c0:c1, :])
    y = x + _mod(mod_ref, 5, rows) * acc
    if final:
        y = y * lax.rsqrt(jnp.mean(y * y, axis=-1, keepdims=True) + EPS) * fg_ref[...]
    return y


def _ffn_kernel(*refs, final, chunk, mixer_out, n_shift, shift_steps, n_prompt_steps):
    refs = list(refs)
    x_ref, mod_ref, g_ref, wgu_ref, wd_ref, fg_ref = refs[:6]
    pos = 6
    a_ref = attn_ref = wo_ref = as_ref = attns_ref = None
    if mixer_out:
        a_ref, attn_ref, wo_ref = refs[pos:pos + 3]
        pos += 3
    shift_in = refs[pos:pos + 2 * n_shift]
    pos += 2 * n_shift
    xs_ref, mods_ref = refs[pos:pos + 2]
    pos += 2
    if mixer_out:
        as_ref, attns_ref = refs[pos:pos + 2]
        pos += 2
    y_ref = refs[pos]
    state_refs = refs[pos + 1:pos + 1 + n_shift]
    ys_ref = refs[pos + 1 + n_shift]
    step = pl.program_id(0)

    @pl.when(step < n_prompt_steps)
    def _():
        for k, state_ref in enumerate(state_refs):
            cache_ref, newt_ref = shift_in[2 * k], shift_in[2 * k + 1]
            for r0 in range(0, 2 * GROUP_CH, SHIFT_ROWS):
                new = _new_steps(newt_ref, r0, r0 + SHIFT_ROWS, step, shift_steps)
                state_ref[0, r0:r0 + SHIFT_ROWS, :] = _shift_window(cache_ref[0, r0:r0 + SHIFT_ROWS, :], new,
                                                                    shift_steps)[1]
        mixer = (a_ref[0], attn_ref[0]) if mixer_out else None
        y_ref[0] = _ffn_rows(x_ref[0], mod_ref, g_ref, wgu_ref, wd_ref, fg_ref, mixer, wo_ref, final, chunk)

    @pl.when(step == n_prompt_steps)
    def _():
        mixer = (as_ref[0], attns_ref[0]) if mixer_out else None
        ys_ref[0] = _ffn_rows(xs_ref[0], mods_ref, g_ref, wgu_ref, wd_ref, fg_ref, mixer, wo_ref, final, chunk)


def _ffn(x, mod, xs, mod_s, g, wgu, wd, final_g, layer, final, mixer_out=None, shift=None):
    nb, t, _ = x.shape
    tm = min(FFN_ROW_TILE, t)
    n_tiles = t // tm
    n_steps = nb * n_tiles
    rows_s = xs.shape[1]

    def prompt_map(*tail):
        def index_map(s):
            p = jnp.minimum(s, n_steps - 1)
            return (p // n_tiles, p % n_tiles) + tail
        return index_map

    per_n = lambda s: (jnp.minimum(s, n_steps - 1) // n_tiles, 0, 0, 0)
    in_specs = [pl.BlockSpec((1, tm, D_MODEL), prompt_map(0)),
                pl.BlockSpec((1, 6, 1, D_MODEL), per_n),
                _const_spec((1, D_MODEL)),
                _layer_spec((D_MODEL, 2 * FFN_HIDDEN), layer), _layer_spec((FFN_HIDDEN, D_MODEL), layer),
                _const_spec((1, D_MODEL))]
    operands = [x, mod, g, wgu, wd, final_g]
    sample_specs = [_const_spec((1, rows_s, D_MODEL)), _const_spec((1, 6) + mod_s.shape[2:])]
    sample_operands = [xs, mod_s]
    if mixer_out is not None:
        a, attn, a_s, attn_s, wo = mixer_out
        in_specs += [pl.BlockSpec((1, tm, A_CH), prompt_map(0)), pl.BlockSpec((1, tm, GROUP_CH), prompt_map(0)),
                     _const_spec((A_CH + GROUP_CH, D_MODEL))]
        operands += [a, attn, wo]
        sample_specs += [_const_spec((1, rows_s, A_CH)), _const_spec((1, rows_s, GROUP_CH))]
        sample_operands += [a_s, attn_s]
    out_specs = [pl.BlockSpec((1, tm, D_MODEL), prompt_map(0))]
    out_shape = [jax.ShapeDtypeStruct((nb, t, D_MODEL), F32)]
    shift_groups, shift_steps = (), 0
    if shift is not None:
        shift_groups, new_t, shift_steps = shift
        per_step = lambda s: (jnp.minimum(s, n_steps - 1), 0, 0)
        for cache, grp in shift_groups:
            assert cache.shape[0] == n_steps
            in_specs += [pl.BlockSpec((1,) + cache.shape[1:], per_step),
                         pl.BlockSpec((2 * GROUP_CH, LANES), lambda s, grp=grp: (grp, 0),
                                      pipeline_mode=pl.Buffered(1))]
            operands += [cache, new_t]
            out_specs.append(pl.BlockSpec((1,) + cache.shape[1:], per_step))
            out_shape.append(jax.ShapeDtypeStruct(cache.shape, F32))
    out_specs.append(_const_spec((1, rows_s, D_MODEL)))
    out_shape.append(jax.ShapeDtypeStruct((1, rows_s, D_MODEL), F32))
    return pl.pallas_call(
        functools.partial(_ffn_kernel, final=final, chunk=512, mixer_out=mixer_out is not None,
                          n_shift=len(shift_groups), shift_steps=shift_steps, n_prompt_steps=n_steps),
        grid=(n_steps + 1,),
        in_specs=in_specs + sample_specs,
        out_specs=out_specs,
        out_shape=out_shape,
        compiler_params=_params(1),
        name="ffn_final" if final else "l0_out_ffn",
    )(*operands, *sample_operands)


def _spatial_tril(ws_ref, grp):
    i = lax.broadcasted_iota(jnp.int32, (D_CHUNK, D_CHUNK), 0)
    j = lax.broadcasted_iota(jnp.int32, (D_CHUNK, D_CHUNK), 1)
    return jnp.where(j <= i, ws_ref[grp], 0.0).astype(BF16)


def _in1_prompt_kernel(x_ref, mod_ref, g_ref, w_ref, cw_ref, cb_ref, lg_ref, lb_ref, ws_ref, bs_ref, wo_ref,
                       y_ref, cst_ref, ext_ref, cat_ref, *, tm):
    t = pl.program_id(1)
    last = pl.num_programs(1) - 1
    halo = C_CONV - 1
    pad = 8

    @pl.when(t == 0)
    def _():
        ext_ref[0:pad, :] = jnp.zeros((pad, C_CH), F32)

    x = x_ref[0]
    h = _rms_mod(x, g_ref[...], mod_ref[0, 1], mod_ref[0, 0]).astype(BF16)
    v_in, u_in, xi, cg, bg = (_mm(h, w_ref[:, c0:c0 + C_CH]) for c0 in range(O_IN - C_CH, -1, -C_CH))
    ext_ref[pad:pad + tm, :] = cg * xi
    conv = jnp.broadcast_to(cb_ref[...], (tm, C_CH))
    for k in range(C_CONV):
        conv = conv + cw_ref[k:k + 1, :] * ext_ref[pl.ds(pad - halo + k, tm), :]
    cat_ref[:, :C_CH] = (bg * conv).astype(BF16)
    ext_ref[pl.ds(pad - halo, halo), :] = ext_ref[pl.ds(tm + pad - halo, halo), :]
    u = jax.nn.gelu(u_in, approximate=True)
    v = _layernorm(jax.nn.gelu(v_in, approximate=True), lg_ref[...], lb_ref[...]).astype(BF16)
    for grp in range(D_GROUPS):
        ws = _spatial_tril(ws_ref, grp)
        cols = slice(grp * D_GROUP_CH, (grp + 1) * D_GROUP_CH)
        for c in range(tm // D_CHUNK):
            rows = slice(c * D_CHUNK, (c + 1) * D_CHUNK)
            mix = _mm(ws, v[rows, cols]) + bs_ref[:, cols]
            cat_ref[rows, C_CH + grp * D_GROUP_CH:C_CH + (grp + 1) * D_GROUP_CH] = (u[rows, cols] * mix).astype(BF16)
    y_ref[0] = x + mod_ref[0, 2] * _mm(cat_ref[...], wo_ref[...])

    @pl.when(t == last)
    def _():
        cst_ref[0] = ext_ref[pl.ds(pad - halo, halo), :]


def _in1_prompt(x, mod, g, w, cw, cb, lg, lb, ws, bs_full, wo):
    nb, t, _ = x.shape
    tm = ROW_TILE
    row = lambda n, i: (n, i, 0)
    return pl.pallas_call(
        functools.partial(_in1_prompt_kernel, tm=tm),
        grid=(nb, t // tm),
        in_specs=[pl.BlockSpec((1, tm, D_MODEL), row),
                  pl.BlockSpec((1, 6, 1, D_MODEL), lambda n, i: (n, 0, 0, 0)),
                  _const_spec((1, D_MODEL)), _const_spec((D_MODEL, O_IN)),
                  _const_spec((C_CONV, C_CH)), _const_spec((1, C_CH)),
                  _const_spec((1, D_CH)), _const_spec((1, D_CH)),
                  _const_spec((D_GROUPS, D_CHUNK, D_CHUNK)), _const_spec((D_CHUNK, D_CH)),
                  _const_spec((C_CH + D_CH, D_MODEL))],
        out_specs=[pl.BlockSpec((1, tm, D_MODEL), row),
                   pl.BlockSpec((1, C_CONV - 1, C_CH), lambda n, i: (n, 0, 0))],
        out_shape=[jax.ShapeDtypeStruct((nb, t, D_MODEL), F32),
                   jax.ShapeDtypeStruct((nb, C_CONV - 1, C_CH), F32)],
        scratch_shapes=[pltpu.VMEM((tm + 8, C_CH), F32), pltpu.VMEM((tm, C_CH + D_CH), BF16)],
        compiler_params=_params(2),
        name="l1_mixer_prompt",
    )(x, mod, g, w, cw, cb, lg, lb, ws, bs_full, wo)


def _in0_sample_kernel(x_ref, mod_ref, g_ref, w_ref, abuf_ref, cw_ref, cb_ref, lg_ref, lb_ref,
                       a_ref, ast_ref, q_ref, newt_ref, ext_ref, perm_ref, *, nb, steps):
    rows = nb * steps
    hist = (A_CONV - 1) * nb
    h = _rms_mod(x_ref[0], g_ref[...], _mod(mod_ref, 1, rows), _mod(mod_ref, 0, rows)).astype(BF16)
    proj = _mm(h, w_ref[...])
    ext_ref[0:hist, :] = abuf_ref[...]
    ext_ref[hist:hist + rows, :] = proj[:, :A_CH] * jax.nn.sigmoid(proj[:, A_CH:2 * A_CH])
    acc = jnp.broadcast_to(cb_ref[...], (rows, A_CH))
    for k in range(A_CONV):
        acc = acc + cw_ref[k:k + 1, :] * ext_ref[k * nb:k * nb + rows, :]
    a_ref[0] = _silu(_layernorm(acc, lg_ref[...], lb_ref[...])).astype(BF16)
    ast_ref[...] = ext_ref[rows:rows + hist, :]
    q_ref[...] = proj[:, Q_OFF:K_OFF]
    tile = 0
    for g in range(N_GROUPS):
        for off in (K_OFF, V_OFF):
            for c in range(GROUP_TILES):
                col = off + g * GROUP_CH + c * LANES
                for s in range(steps):
                    perm_ref[tile, pl.ds(s, nb, stride=steps), :] = proj[s * nb:(s + 1) * nb, col:col + LANES]
                newt_ref[tile * LANES:(tile + 1) * LANES, :] = perm_ref[tile].T
                tile += 1


def _in0_sample(x, mod, g, w, abuf_t, cw, cb, lg, lb, nb, steps):
    rows = nb * steps
    assert rows == LANES
    hist = (A_CONV - 1) * nb
    return pl.pallas_call(
        functools.partial(_in0_sample_kernel, nb=nb, steps=steps),
        grid=(1,),
        in_specs=[_const_spec((1, rows, D_MODEL)), _const_spec((1, 6, nb, D_MODEL)),
                  _const_spec((1, D_MODEL)), _const_spec((D_MODEL, E_IN)), _const_spec((hist, A_CH)),
                  _const_spec((A_CONV, A_CH)), _const_spec((1, A_CH)), _const_spec((1, A_CH)),
                  _const_spec((1, A_CH))],
        out_specs=[_const_spec((1, rows, A_CH)), _const_spec((hist, A_CH)),
                   _const_spec((rows, N_GROUPS * GROUP_CH)), _const_spec((2 * N_GROUPS * GROUP_CH, LANES))],
        out_shape=[jax.ShapeDtypeStruct((1, rows, A_CH), BF16), jax.ShapeDtypeStruct((hist, A_CH), F32),
                   jax.ShapeDtypeStruct((rows, N_GROUPS * GROUP_CH), F32),
                   jax.ShapeDtypeStruct((2 * N_GROUPS * GROUP_CH, LANES), F32)],
        scratch_shapes=[pltpu.VMEM((hist + rows, A_CH), F32),
                        pltpu.VMEM((2 * N_GROUPS * GROUP_TILES, LANES, LANES), F32)],
        compiler_params=_params(1),
        name="l0_in_sample",
    )(x, mod, g, w, abuf_t, cw, cb, lg, lb)


def _new_steps(newt_ref, r0, r1, n, steps):
    return pltpu.roll(newt_ref[r0:r1, :], lax.rem(LANES - n * steps, LANES), axis=1)


def _shift_window(cache_rows, new_rows, steps):
    w = cache_rows.shape[-1]
    comb = jnp.concatenate([cache_rows, new_rows], axis=1)
    return comb, pltpu.roll(comb, w + LANES - steps, axis=1)[:, :w]


def _attn_sample_kernel(q_ref, newt_ref, c0_ref, c1_ref, c2_ref, base0_ref, base1_ref, base2_ref,
                        attn_ref, b0_ref, b1_ref, b2_ref, *, steps):
    step = pl.program_id(0)
    rows = HPG * steps

    @pl.when(step == 0)
    def _():
        for base_ref, b_ref in ((base0_ref, b0_ref), (base1_ref, b1_ref), (base2_ref, b2_ref)):
            for h in range(HPG):
                row = jnp.broadcast_to(base_ref[h:h + 1, :], (SUBLANES, base_ref.shape[-1]))
                b_ref[h * steps:(h + 1) * steps, :] = pltpu.roll(row, 0, 1, stride=1, stride_axis=0)[:steps]

    row_head = lax.broadcasted_iota(jnp.int32, (rows, GROUP_CH), 0) // steps
    col_head = lax.broadcasted_iota(jnp.int32, (rows, GROUP_CH), 1) // HEAD_DIM
    step_masks = _head_masks(steps)
    for j in range(SAMPLE_ROWS_PER_STEP):
        n = step * SAMPLE_ROWS_PER_STEP + j
        outs, lses = [], []
        for g, (c_ref, b_ref) in enumerate(zip((c0_ref, c1_ref, c2_ref), (b0_ref, b1_ref, b2_ref))):
            new = _new_steps(newt_ref, 2 * g * GROUP_CH, 2 * (g + 1) * GROUP_CH, n, steps)
            comb_bf = jnp.concatenate([c_ref[j], new], axis=1).astype(BF16)
            q = q_ref[j][:, g * GROUP_CH:(g + 1) * GROUP_CH]
            q_bd = jnp.where(row_head == col_head, jnp.concatenate([q] * HPG, axis=0), 0.0).astype(BF16)
            s = _mm(q_bd, comb_bf[:GROUP_CH]) * (HEAD_DIM ** -0.5) + b_ref[...]
            m = jnp.max(s, axis=-1, keepdims=True)
            e = jnp.exp(s - m)
            l = jnp.sum(e, axis=-1, keepdims=True)
            p = (e * (1.0 / l)).astype(BF16)
            o_all = _mm_nt(p, comb_bf[GROUP_CH:])
            lse_all = jnp.broadcast_to(m + jnp.log(l), (rows, GROUP_CH))
            o = jnp.zeros((steps, GROUP_CH), F32)
            lse = jnp.zeros((steps, GROUP_CH), F32)
            for h in range(HPG):
                o = jnp.where(step_masks[h], o_all[h * steps:(h + 1) * steps], o)
                lse = jnp.where(step_masks[h], lse_all[h * steps:(h + 1) * steps], lse)
            outs.append(o)
            lses.append(lse)
        attn_ref[j] = _mix_groups(outs, lses)


def _attn_sample(q, new_t, caches, biases, steps):
    nb = q.shape[0]
    rps = SAMPLE_ROWS_PER_STEP
    assert nb % rps == 0
    per_n = lambda n: (n, 0, 0)
    cache_specs = [pl.BlockSpec((rps, 2 * GROUP_CH, w), per_n) for w in WINDOWS]
    return pl.pallas_call(
        functools.partial(_attn_sample_kernel, steps=steps),
        grid=(nb // rps,),
        in_specs=[pl.BlockSpec((rps, steps, N_GROUPS * GROUP_CH), per_n),
                  _const_spec((2 * N_GROUPS * GROUP_CH, LANES))] + cache_specs
                 + [_const_spec((HPG, w + LANES)) for w in WINDOWS],
        out_specs=pl.BlockSpec((rps, steps, GROUP_CH), per_n),
        out_shape=jax.ShapeDtypeStruct((nb, steps, GROUP_CH), F32),
        scratch_shapes=[pltpu.VMEM((HPG * steps, w + LANES), F32) for w in WINDOWS],
        compiler_params=_params(1),
        name="l0_attn_sample",
    )(q, new_t, *caches, *biases)


def _in1_sample_kernel(x_ref, mod_ref, g_ref, w_ref, cbuf_ref, cw_ref, cb_ref, lg_ref, lb_ref, coef_ref, bs_ref,
                       wo_ref, y_ref, cst_ref, dv_ref, ext_ref, *, nb, steps):
    rows = nb * steps
    hist = (C_CONV - 1) * nb
    x = x_ref[0]
    h = _rms_mod(x, g_ref[...], _mod(mod_ref, 1, rows), _mod(mod_ref, 0, rows)).astype(BF16)
    proj = _mm(h, w_ref[...])
    ext_ref[0:hist, :] = cbuf_ref[...]
    ext_ref[hist:hist + rows, :] = proj[:, C_CH:2 * C_CH] * proj[:, 2 * C_CH:3 * C_CH]
    conv = jnp.broadcast_to(cb_ref[...], (rows, C_CH))
    for k in range(C_CONV):
        conv = conv + cw_ref[k:k + 1, :] * ext_ref[k * nb:k * nb + rows, :]
    yc = proj[:, :C_CH] * conv
    cst_ref[...] = ext_ref[rows:rows + hist, :]
    uv = jax.nn.gelu(proj[:, 3 * C_CH:], approximate=True)
    v = _layernorm(uv[:, D_CH:], lg_ref[...], lb_ref[...])
    dv_ref[...] = v
    v_r = v.astype(BF16).astype(F32)
    coef = coef_ref[...].astype(BF16).astype(F32)
    mixes = []
    for i in range(steps):
        mix = jnp.broadcast_to(bs_ref[i:i + 1, :], (nb, D_CH))
        for j in range(i + 1):
            mix = mix + coef[i * steps + j:i * steps + j + 1, :] * v_r[j * nb:(j + 1) * nb, :]
        mixes.append(mix)
    yd = uv[:, :D_CH] * jnp.concatenate(mixes, axis=0)
    cat = jnp.concatenate([yc.astype(BF16), yd.astype(BF16)], axis=1)
    y_ref[0] = x + _mod(mod_ref, 2, rows) * _mm(cat, wo_ref[...])


def _in1_sample(x, mod, g, w, cbuf_t, cw, cb, lg, lb, coef, bs_rows, wo, nb, steps):
    rows = nb * steps
    hist = (C_CONV - 1) * nb
    return pl.pallas_call(
        functools.partial(_in1_sample_kernel, nb=nb, steps=steps),
        grid=(1,),
        in_specs=[_const_spec((1, rows, D_MODEL)), _const_spec((1, 6, nb, D_MODEL)),
                  _const_spec((1, D_MODEL)), _const_spec((D_MODEL, O_IN)), _const_spec((hist, C_CH)),
                  _const_spec((C_CONV, C_CH)), _const_spec((1, C_CH)), _const_spec((1, D_CH)),
                  _const_spec((1, D_CH)), _const_spec((steps * steps, D_CH)), _const_spec((steps, D_CH)),
                  _const_spec((C_CH + D_CH, D_MODEL))],
        out_specs=[_const_spec((1, rows, D_MODEL)), _const_spec((hist, C_CH)), _const_spec((rows, D_CH))],
        out_shape=[jax.ShapeDtypeStruct((1, rows, D_MODEL), F32), jax.ShapeDtypeStruct((hist, C_CH), F32),
                   jax.ShapeDtypeStruct((rows, D_CH), F32)],
        scratch_shapes=[pltpu.VMEM((hist + rows, C_CH), F32)],
        compiler_params=_params(1),
        name="l1_mixer_sample",
    )(x, mod, g, w, cbuf_t, cw, cb, lg, lb, coef, bs_rows, wo)


def _t5_bucket(dist):
    n = np.maximum(np.asarray(dist, dtype=np.int64), 0)
    max_exact = N_BUCKETS // 2
    large = max_exact + (np.log(np.maximum(n, 1) / max_exact) / np.log(MAX_DISTANCE / max_exact)
                         * (N_BUCKETS - max_exact)).astype(np.int32)
    return np.where(n < max_exact, n, np.minimum(large, N_BUCKETS - 1)).astype(np.int32)


def _prompt_bias_base(table, grp):
    rel = Q_BLOCK - np.arange(2 * Q_BLOCK)
    band = (rel >= 0) & (rel <= SUB_WINDOW)
    tab = table[:, grp * HPG:(grp + 1) * HPG]
    bias = tab[_t5_bucket(np.clip(rel, 0, None) * DILATIONS[grp])].T
    return jnp.where(band[None], bias, NEG).astype(F32)


def _sample_bias_base(table, grp):
    w, d = WINDOWS[grp], DILATIONS[grp]
    delta = w - np.arange(w + LANES)
    valid = (delta >= 0) & (delta % d == 0) & (delta // d <= SUB_WINDOW)
    tab = table[:, grp * HPG:(grp + 1) * HPG]
    return jnp.where(valid[None], tab[_t5_bucket(np.clip(delta, 0, None))].T, NEG).astype(F32)


def _step_major(x):
    return jnp.swapaxes(x, 0, 1).reshape((x.shape[0] * x.shape[1],) + x.shape[2:])


def _batch_major(x, nb, steps):
    return jnp.swapaxes(x.reshape((steps, nb) + x.shape[1:]), 0, 1)


def kernel(x_prompt, x_sample, c_prompt, c_sample, state_a_conv, cache_b_kv0, cache_b_kv1, cache_b_kv2, state_c_conv, rel_bias_table, ada_w, ada_b, norm_mix_g, norm_ffn_g, ffn_w_gate_up, ffn_w_down, final_norm_g, e_w_in, a_conv_w, a_conv_b, a_ln_g, a_ln_b, e_w_out, o_w_in, c_conv_w, c_conv_b, d_ln_g, d_ln_b, d_spatial_w, d_spatial_b, o_w_out):
    nb_p = x_prompt.shape[0]
    nb_s, steps, _ = x_sample.shape

    w_in0 = e_w_in[0].astype(BF16)
    w_out0 = e_w_out[0].astype(BF16)
    w_in1 = o_w_in[0].astype(BF16)
    w_out1 = o_w_out[0].astype(BF16)
    w_gu = ffn_w_gate_up.astype(BF16)
    w_dn = ffn_w_down.astype(BF16)
    fin_g = final_norm_g.reshape(1, D_MODEL)

    mod = _ada(jnp.concatenate([c_prompt, c_sample], axis=0), ada_w, ada_b)
    mod = mod.reshape(mod.shape[0], nb_p + nb_s, 6, D_MODEL)
    mod_p = [mod[l, :nb_p].reshape(nb_p, 6, 1, D_MODEL) for l in range(2)]
    mod_s = [jnp.swapaxes(mod[l, nb_p:], 0, 1)[None] for l in range(2)]

    xs = _step_major(x_sample)[None]
    abuf_t = _step_major(state_a_conv[0])
    a_s, a_st_s, q_s, new_t = _in0_sample(xs, mod_s[0], norm_mix_g[0:1], w_in0, abuf_t, a_conv_w[0], a_conv_b,
                                          a_ln_g, a_ln_b, nb_s, steps)
    caches = [jnp.transpose(c[0], (0, 2, 3, 4, 1)).reshape(nb_s, 2 * GROUP_CH, c.shape[2])
              for c in (cache_b_kv0, cache_b_kv1, cache_b_kv2)]
    biases = [_sample_bias_base(rel_bias_table, grp) for grp in range(N_GROUPS)]
    attn_s = _attn_sample(_batch_major(q_s, nb_s, steps), new_t, caches, biases, steps)
    attn_s = _step_major(attn_s).astype(BF16)[None]
    (a, a_st_p, q0, q1, q2, kv0, kv1, kv2, st0, st1, st2) = _in0_prompt(
        x_prompt, mod_p[0], norm_mix_g[0:1], w_in0, a_conv_w[0], a_conv_b, a_ln_g, a_ln_b)
    base = jnp.stack([_prompt_bias_base(rel_bias_table, grp) for grp in range(N_GROUPS)])
    attn = _attn_prompt((q0, q1, q2), (kv0, kv1, kv2), base)
    xp, s2, xs = _ffn(x_prompt, mod_p[0], xs, mod_s[0], norm_ffn_g[0:1], w_gu, w_dn, fin_g, 0, False,
                      (a, attn, a_s, attn_s, w_out0), (((caches[2], 2),), new_t, steps))

    bs_full = jnp.repeat(d_spatial_b[0].T, D_GROUP_CH, axis=1)
    xp, c_st_p = _in1_prompt(xp, mod_p[1], norm_mix_g[1:2], w_in1, c_conv_w[0], c_conv_b, d_ln_g, d_ln_b,
                             d_spatial_w[0], bs_full, w_out1)
    cbuf_t = _step_major(state_c_conv[0])
    coef = jnp.repeat(jnp.transpose(d_spatial_w[0][:, :steps, :steps], (1, 2, 0)).reshape(steps * steps, D_GROUPS),
                      D_GROUP_CH, axis=1)
    xs, c_st_s, dv_s = _in1_sample(xs, mod_s[1], norm_mix_g[1:2], w_in1, cbuf_t, c_conv_w[0], c_conv_b, d_ln_g,
                                   d_ln_b, coef, bs_full[:steps], w_out1, nb_s, steps)
    y_prompt, s0, s1, y_sample = _ffn(xp, mod_p[1], xs, mod_s[1], norm_ffn_g[1:2], w_gu, w_dn, fin_g, 1, True,
                                      None, (((caches[0], 0), (caches[1], 1)), new_t, steps))

    def kv_state_prompt(st):
        return st.reshape(1, nb_p, st.shape[1], 2, HPG, HEAD_DIM)

    def kv_state_sample(st):
        return jnp.transpose(st.reshape(nb_s, 2, HPG, HEAD_DIM, st.shape[-1]), (0, 4, 1, 2, 3))[None]

    return (y_prompt, _batch_major(y_sample[0], nb_s, steps),
            a_st_p[None], _batch_major(a_st_s, nb_s, A_CONV - 1)[None],
            kv_state_prompt(st0), kv_state_sample(s0), kv_state_prompt(st1), kv_state_sample(s1),
            kv_state_prompt(st2), kv_state_sample(s2),
            c_st_p[None], _batch_major(c_st_s, nb_s, C_CONV - 1)[None],
            _batch_major(dv_s, nb_s, steps)[None])
```

```python
import functools

import numpy as np
import jax
import jax.numpy as jnp
from jax import lax
from jax.experimental import pallas as pl
from jax.experimental.pallas import tpu as pltpu

F32 = jnp.float32
BF16 = jnp.bfloat16

D_MODEL = 1024
EPS = 1e-6
A_CH = 256
A_CONV = 31
HEAD_DIM = 64
HPG = 4
GROUP_CH = HPG * HEAD_DIM
WINDOWS = (128, 512, 2048)
DILATIONS = (1, 4, 16)
N_GROUPS = 3
SUB_WINDOW = 128
Q_BLOCK = 128
N_BUCKETS = 32
MAX_DISTANCE = 2048
C_CH = 512
C_CONV = 3
D_CH = 512
D_GROUPS = 4
D_GROUP_CH = D_CH // D_GROUPS
D_CHUNK = 128
FFN_HIDDEN = 2816
Q_OFF = 2 * A_CH
K_OFF = Q_OFF + N_GROUPS * GROUP_CH
V_OFF = K_OFF + N_GROUPS * GROUP_CH
E_IN = V_OFF + N_GROUPS * GROUP_CH
O_IN = 3 * C_CH + 2 * D_CH
LANES = 128
SUBLANES = 8
GROUP_TILES = GROUP_CH // LANES
NEG = -1e30
ROW_TILE = 1024
FFN_ROW_TILE = 512
CONV_ROWS = 64
RING_SLOTS = 3
SAMPLE_ROWS_PER_STEP = 2
SHIFT_ROWS = 64
VMEM_LIMIT = 56 * 1024 * 1024


def _params(n_axes):
    return pltpu.CompilerParams(dimension_semantics=("arbitrary",) * n_axes, vmem_limit_bytes=VMEM_LIMIT)


def _const_spec(shape):
    return pl.BlockSpec(shape, lambda *_: (0,) * len(shape), pipeline_mode=pl.Buffered(1))


def _layer_spec(shape, layer):
    return pl.BlockSpec((1,) + shape, lambda *_: (layer,) + (0,) * len(shape), pipeline_mode=pl.Buffered(1))


def _rms_mod(x, g, scale, shift):
    return (x * lax.rsqrt(jnp.mean(x * x, axis=-1, keepdims=True) + EPS)) * (g * (1.0 + scale)) + shift


def _mod(mod_ref, k, rows):
    m = mod_ref[0, k]
    reps = rows // m.shape[0] if m.shape[0] > 1 else 1
    return m if reps == 1 else jnp.concatenate([m] * reps, axis=0)


def _layernorm(x, g, b):
    mu = jnp.mean(x, axis=-1, keepdims=True)
    xc = x - mu
    var = jnp.mean(xc * xc, axis=-1, keepdims=True)
    return xc * lax.rsqrt(var + EPS) * g + b


def _silu(x):
    return x * jax.nn.sigmoid(x)


def _mm(a, b):
    return jnp.dot(a, b, preferred_element_type=F32)


def _mm_nt(a, b):
    return lax.dot_general(a, b, (((1,), (1,)), ((), ())), preferred_element_type=F32)


def _group_cols(proj, off, g):
    return proj[:, off + g * GROUP_CH:off + (g + 1) * GROUP_CH]


def _ada_kernel(c_ref, w_ref, b_ref, o_ref):
    cs = _silu(c_ref[...]).astype(BF16)
    o_ref[0] = _mm(cs, w_ref[0].astype(BF16)) + b_ref[0]


def _ada(c_all, ada_w, ada_b):
    depth, _, width = ada_w.shape
    nb = c_all.shape[0]
    tn = 1536
    return pl.pallas_call(
        _ada_kernel,
        grid=(depth, width // tn),
        in_specs=[_const_spec((nb, D_MODEL)),
                  pl.BlockSpec((1, D_MODEL, tn), lambda l, j: (l, 0, j)),
                  pl.BlockSpec((1, 1, tn), lambda l, j: (l, 0, j))],
        out_specs=pl.BlockSpec((1, nb, tn), lambda l, j: (l, 0, j)),
        out_shape=jax.ShapeDtypeStruct((depth, nb, width), F32),
        compiler_params=_params(2),
        name="ada_mod",
    )(c_all, ada_w, ada_b.reshape(depth, 1, width))


def _ordering_zero(x):
    bits = lax.shift_right_logical(lax.bitcast_convert_type(x, jnp.uint32), jnp.uint32(16))
    return lax.bitcast_convert_type(lax.shift_right_logical(bits, jnp.uint32(16)), F32)


def _conv_ln_silu(ext_ref, shift_ref, base, rows, wb_ref, cb_ref, lg_ref, lb_ref, taps, after):
    ch = cb_ref.shape[-1]
    zero = _ordering_zero(after[-SUBLANES:, :ch])
    acc = jnp.broadcast_to(cb_ref[...], (rows // SUBLANES, SUBLANES, ch))
    for k in range(taps):
        m = (base + k) % SUBLANES
        src = ext_ref if m == 0 else shift_ref.at[m - 1]
        x = src[pl.ds(base + k - m, rows), :].reshape(rows // SUBLANES, SUBLANES, ch)
        acc = acc + (wb_ref[k] + zero)[None] * x
    return _silu(_layernorm(acc.reshape(rows, ch), lg_ref[...], lb_ref[...]))


def _in0_prompt_kernel(x_ref, mod_ref, g_ref, w_ref, cw_ref, cb_ref, lg_ref, lb_ref,
                       a_ref, ast_ref, q0_ref, q1_ref, q2_ref, kv0_ref, kv1_ref, kv2_ref,
                       st0_ref, st1_ref, st2_ref, ext_ref, shift_ref, perm_ref, wb_ref, *, tm):
    t = pl.program_id(1)
    last = pl.num_programs(1) - 1
    halo = A_CONV - 1
    pad = 32

    @pl.when(t == 0)
    def _():
        ext_ref[0:pad, :] = jnp.zeros((pad, A_CH), F32)
        for k in range(A_CONV):
            wb_ref[k] = jnp.broadcast_to(cw_ref[k:k + 1, :], (SUBLANES, A_CH))

    h = _rms_mod(x_ref[0], g_ref[...], mod_ref[0, 1], mod_ref[0, 0]).astype(BF16)
    a_in = _mm(h, w_ref[:, :Q_OFF])
    ext_ref[pad:pad + tm, :] = a_in[:, :A_CH] * jax.nn.sigmoid(a_in[:, A_CH:])
    for m in range(1, SUBLANES):
        shift_ref[m - 1] = ext_ref[pl.ds(m, tm + pad - SUBLANES), :]

    def conv_chunk(c, after):
        r0 = c * CONV_ROWS
        y = _conv_ln_silu(ext_ref, shift_ref, pad - halo + r0, CONV_ROWS, wb_ref, cb_ref, lg_ref, lb_ref, A_CONV,
                          after)
        a_ref[0, r0:r0 + CONV_ROWS, :] = y.astype(BF16)

    def project(g, off, dst, col0, slot0):
        d = DILATIONS[g]
        val = _mm(h, w_ref[:, off + g * GROUP_CH:off + (g + 1) * GROUP_CH])
        if off != Q_OFF:
            st_refs[g][0, :, col0:col0 + GROUP_CH] = val[tm - min(WINDOWS[g], tm):]
        if d == 1:
            dst[0, 0, :, col0:col0 + GROUP_CH] = val.astype(BF16)
            return val
        for c in range(GROUP_TILES):
            tile = perm_ref.at[slot0 + c]
            tile[...] = val[:, c * LANES:(c + 1) * LANES]
            for r in range(d):
                dst[0, r, :, col0 + c * LANES:col0 + (c + 1) * LANES] = (
                    tile[pl.ds(r, tm // d, stride=d), :].astype(BF16))
        return val

    q_refs, kv_refs, st_refs = (q0_ref, q1_ref, q2_ref), (kv0_ref, kv1_ref, kv2_ref), (st0_ref, st1_ref, st2_ref)
    tasks = [(g, off, (q_refs if off == Q_OFF else kv_refs)[g], GROUP_CH if off == V_OFF else 0)
             for g in range(N_GROUPS) for off in (Q_OFF, K_OFF, V_OFF)]
    n_chunks = tm // CONV_ROWS
    per_dot = -(-n_chunks // len(tasks))
    for i, (g, off, dst, col0) in enumerate(tasks):
        val = project(g, off, dst, col0, i * GROUP_TILES)
        for c in range(i * per_dot, min((i + 1) * per_dot, n_chunks)):
            conv_chunk(c, val)
    ext_ref[pl.ds(pad - halo, halo), :] = ext_ref[pl.ds(tm + pad - halo, halo), :]

    @pl.when(t == last)
    def _():
        ast_ref[0] = ext_ref[pl.ds(pad - halo, halo), :]


def _in0_prompt(x, mod, g, w, cw, cb, lg, lb):
    nb, t, _ = x.shape
    tm = ROW_TILE
    assert t % tm == 0 and tm >= WINDOWS[1] and WINDOWS[2] == t
    row = lambda n, i: (n, i, 0)
    per_n = lambda n, i: (n, 0, 0)
    res = lambda n, i: (n, 0, i, 0)
    out_shape = [jax.ShapeDtypeStruct((nb, t, A_CH), BF16), jax.ShapeDtypeStruct((nb, A_CONV - 1, A_CH), F32)]
    out_specs = [pl.BlockSpec((1, tm, A_CH), row), pl.BlockSpec((1, A_CONV - 1, A_CH), per_n)]
    for width in (GROUP_CH, 2 * GROUP_CH):
        for d in DILATIONS:
            out_shape.append(jax.ShapeDtypeStruct((nb, d, t // d, width), BF16))
            out_specs.append(pl.BlockSpec((1, d, tm // d, width), res))
    out_shape += [jax.ShapeDtypeStruct((nb, WINDOWS[0], 2 * GROUP_CH), F32),
                  jax.ShapeDtypeStruct((nb, WINDOWS[1], 2 * GROUP_CH), F32),
                  jax.ShapeDtypeStruct((nb, t, 2 * GROUP_CH), F32)]
    out_specs += [pl.BlockSpec((1, WINDOWS[0], 2 * GROUP_CH), per_n),
                  pl.BlockSpec((1, WINDOWS[1], 2 * GROUP_CH), per_n),
                  pl.BlockSpec((1, tm, 2 * GROUP_CH), row)]
    return pl.pallas_call(
        functools.partial(_in0_prompt_kernel, tm=tm),
        grid=(nb, t // tm),
        in_specs=[pl.BlockSpec((1, tm, D_MODEL), row),
                  pl.BlockSpec((1, 6, 1, D_MODEL), lambda n, i: (n, 0, 0, 0)),
                  _const_spec((1, D_MODEL)), _const_spec((D_MODEL, E_IN)),
                  _const_spec((A_CONV, A_CH)), _const_spec((1, A_CH)),
                  _const_spec((1, A_CH)), _const_spec((1, A_CH))],
        out_specs=out_specs,
        out_shape=out_shape,
        scratch_shapes=[pltpu.VMEM((tm + 32, A_CH), F32), pltpu.VMEM((SUBLANES - 1, tm + 32 - SUBLANES, A_CH), F32),
                        pltpu.VMEM((3 * N_GROUPS * GROUP_TILES, tm, LANES), F32),
                        pltpu.VMEM((A_CONV, SUBLANES, A_CH), F32)],
        compiler_params=_params(2),
        name="l0_in_prompt",
    )(x, mod, g, w, cw, cb, lg, lb)


def _head_masks(rows):
    col_head = lax.broadcasted_iota(jnp.int32, (rows, GROUP_CH), 1) // HEAD_DIM
    return [col_head == h for h in range(HPG)]


def _mix_groups(outs, lses):
    m = jnp.maximum(jnp.maximum(lses[0], lses[1]), lses[2])
    es = [jnp.exp(l - m) for l in lses]
    inv = 1.0 / (es[0] + es[1] + es[2])
    return (es[0] * inv) * outs[0] + (es[1] * inv) * outs[1] + (es[2] * inv) * outs[2]


def _attn_prompt_kernel(q0_ref, kv0_ref, q1_ref, kv1_ref, q2_ref, kv2_ref, base_ref, attn_ref,
                        bias_ref, o_ref, lse_ref, *, t):
    masks = _head_masks(Q_BLOCK)
    masks_bf = [jnp.where(m, HEAD_DIM ** -0.5, 0.0).astype(BF16) for m in masks]
    key_idx = lax.broadcasted_iota(jnp.int32, (HPG * Q_BLOCK, 2 * Q_BLOCK), 1)

    @pl.when(pl.program_id(0) == 0)
    def _():
        for g in range(N_GROUPS):
            for h in range(HPG):
                row = jnp.broadcast_to(base_ref[g, h:h + 1, :], (Q_BLOCK, 2 * Q_BLOCK))
                bias_ref[g, h * Q_BLOCK:(h + 1) * Q_BLOCK, :] = pltpu.roll(row, 0, 1, stride=1, stride_axis=0)

    for g, d, q_ref, kv_ref in ((1, DILATIONS[1], q1_ref, kv1_ref), (2, DILATIONS[2], q2_ref, kv2_ref),
                                (0, DILATIONS[0], q0_ref, kv0_ref)):
        nblk = t // d // Q_BLOCK
        single = nblk == 1

        def block(j, carry, g=g, d=d, q_ref=q_ref, kv_ref=kv_ref, nblk=nblk, single=single):
            r = j // nblk
            i = j % nblk
            rows = pl.ds(pl.multiple_of(i * Q_BLOCK, Q_BLOCK), Q_BLOCK)
            q = q_ref[0, r, rows, :]
            qs = jnp.concatenate([q * mb for mb in masks_bf], axis=0)
            if single:
                kk = kv_ref[0, r, rows, :]
                bias = bias_ref[g, :, Q_BLOCK:]
            else:
                prev = pl.ds(pl.multiple_of(jnp.maximum(i - 1, 0) * Q_BLOCK, Q_BLOCK), Q_BLOCK)
                kk = jnp.concatenate([kv_ref[0, r, prev, :], kv_ref[0, r, rows, :]], axis=0)
                bias = bias_ref[g] + jnp.where(key_idx < jnp.where(i == 0, Q_BLOCK, 0), NEG, 0.0)
            s = _mm_nt(qs, kk[:, :GROUP_CH]) + bias
            m = jnp.max(s, axis=-1, keepdims=True)
            e = jnp.exp(s - m)
            l = jnp.sum(e, axis=-1, keepdims=True)
            o_all = _mm((e * (1.0 / l)).astype(BF16), kk[:, GROUP_CH:])
            lse_all = jnp.broadcast_to(m + jnp.log(l), (HPG * Q_BLOCK, GROUP_CH))
            o = o_all[:Q_BLOCK]
            lse = lse_all[:Q_BLOCK]
            for h in range(1, HPG):
                o = jnp.where(masks[h], o_all[h * Q_BLOCK:(h + 1) * Q_BLOCK], o)
                lse = jnp.where(masks[h], lse_all[h * Q_BLOCK:(h + 1) * Q_BLOCK], lse)
            for c in range(GROUP_TILES):
                cols = slice(c * LANES, (c + 1) * LANES)
                if d == 1:
                    outs = [o[:, cols]] + [o_ref[k, c, rows, :] for k in range(N_GROUPS - 1)]
                    lses = [lse[:, cols]] + [lse_ref[k, c, rows, :] for k in range(N_GROUPS - 1)]
                    attn_ref[0, rows, cols] = _mix_groups(outs, lses).astype(BF16)
                else:
                    tokens = pl.ds(r + i * (Q_BLOCK * d), Q_BLOCK, stride=d)
                    o_ref[g - 1, c, tokens, :] = o[:, cols]
                    lse_ref[g - 1, c, tokens, :] = lse[:, cols]
            return carry

        lax.fori_loop(0, d * nblk, block, 0, unroll=16)


def _attn_prompt(qs, kvs, base):
    nb, _, t, _ = qs[0].shape
    blk = lambda n: (n, 0, 0, 0)
    in_specs = []
    for d in DILATIONS:
        in_specs += [pl.BlockSpec((1, d, t // d, GROUP_CH), blk), pl.BlockSpec((1, d, t // d, 2 * GROUP_CH), blk)]
    operands = [x for pair in zip(qs, kvs) for x in pair]
    return pl.pallas_call(
        functools.partial(_attn_prompt_kernel, t=t),
        grid=(nb,),
        in_specs=in_specs + [_const_spec((N_GROUPS, HPG, 2 * Q_BLOCK))],
        out_specs=pl.BlockSpec((1, t, GROUP_CH), lambda n: (n, 0, 0)),
        out_shape=jax.ShapeDtypeStruct((nb, t, GROUP_CH), BF16),
        scratch_shapes=[pltpu.VMEM((N_GROUPS, HPG * Q_BLOCK, 2 * Q_BLOCK), F32),
                        pltpu.VMEM((N_GROUPS - 1, GROUP_TILES, t, LANES), F32),
                        pltpu.VMEM((N_GROUPS - 1, GROUP_TILES, t, LANES), F32)],
        compiler_params=_params(1),
        name="l0_attn_prompt",
    )(*operands, base)


def _ffn_rows(x, mod_ref, g_ref, wgu_ref, wd_ref, fg_ref, mixer, wo_ref, final, chunk):
    rows = x.shape[0]
    if mixer is not None:
        a, attn = mixer
        x = x + _mod(mod_ref, 2, rows) * _mm(jnp.concatenate([a, attn], axis=1), wo_ref[...])
    h = _rms_mod(x, g_ref[...], _mod(mod_ref, 4, rows), _mod(mod_ref, 3, rows)).astype(BF16)
    acc = jnp.zeros(x.shape, F32)
    for c0 in range(0, FFN_HIDDEN, chunk):
        c1 = min(c0 + chunk, FFN_HIDDEN)
        gate = _mm(h, wgu_ref[0, :, c0:c1])
        up = _mm(h, wgu_ref[0, :, FFN_HIDDEN + c0:FFN_HIDDEN + c1])
        acc = acc + _mm((_silu(gate) * up).astype(BF16), wd_ref[0, c0:c1, :])
    y = x + _mod(mod_ref, 5, rows) * acc
    if final:
        y = y * lax.rsqrt(jnp.mean(y * y, axis=-1, keepdims=True) + EPS) * fg_ref[...]
    return y


def _ffn_kernel(*refs, final, chunk, mixer_out, n_shift, shift_steps, n_prompt_steps):
    refs = list(refs)
    x_ref, mod_ref, g_ref, wgu_ref, wd_ref, fg_ref = refs[:6]
    pos = 6
    a_ref = attn_ref = wo_ref = as_ref = attns_ref = None
    if mixer_out:
        a_ref, attn_ref, wo_ref = refs[pos:pos + 3]
        pos += 3
    shift_in = refs[pos:pos + 2 * n_shift]
    pos += 2 * n_shift
    xs_ref, mods_ref = refs[pos:pos + 2]
    pos += 2
    if mixer_out:
        as_ref, attns_ref = refs[pos:pos + 2]
        pos += 2
    y_ref = refs[pos]
    state_refs = refs[pos + 1:pos + 1 + n_shift]
    ys_ref = refs[pos + 1 + n_shift]
    step = pl.program_id(0)

    @pl.when(step < n_prompt_steps)
    def _():
        for k, state_ref in enumerate(state_refs):
            cache_ref, newt_ref = shift_in[2 * k], shift_in[2 * k + 1]
            for r0 in range(0, 2 * GROUP_CH, SHIFT_ROWS):
                new = _new_steps(newt_ref, r0, r0 + SHIFT_ROWS, step, shift_steps)
                state_ref[0, r0:r0 + SHIFT_ROWS, :] = _shift_window(cache_ref[0, r0:r0 + SHIFT_ROWS, :], new,
                                                                    shift_steps)[1]
        mixer = (a_ref[0], attn_ref[0]) if mixer_out else None
        y_ref[0] = _ffn_rows(x_ref[0], mod_ref, g_ref, wgu_ref, wd_ref, fg_ref, mixer, wo_ref, final, chunk)

    @pl.when(step == n_prompt_steps)
    def _():
        mixer = (as_ref[0], attns_ref[0]) if mixer_out else None
        ys_ref[0] = _ffn_rows(xs_ref[0], mods_ref, g_ref, wgu_ref, wd_ref, fg_ref, mixer, wo_ref, final, chunk)


def _ffn(x, mod, xs, mod_s, g, wgu, wd, final_g, layer, final, mixer_out=None, shift=None):
    nb, t, _ = x.shape
    tm = min(FFN_ROW_TILE, t)
    n_tiles = t // tm
    n_steps = nb * n_tiles
    rows_s = xs.shape[1]

    def prompt_map(*tail):
        def index_map(s):
            p = jnp.minimum(s, n_steps - 1)
            return (p // n_tiles, p % n_tiles) + tail
        return index_map

    per_n = lambda s: (jnp.minimum(s, n_steps - 1) // n_tiles, 0, 0, 0)
    in_specs = [pl.BlockSpec((1, tm, D_MODEL), prompt_map(0)),
                pl.BlockSpec((1, 6, 1, D_MODEL), per_n),
                _const_spec((1, D_MODEL)),
                _layer_spec((D_MODEL, 2 * FFN_HIDDEN), layer), _layer_spec((FFN_HIDDEN, D_MODEL), layer),
                _const_spec((1, D_MODEL))]
    operands = [x, mod, g, wgu, wd, final_g]
    sample_specs = [_const_spec((1, rows_s, D_MODEL)), _const_spec((1, 6) + mod_s.shape[2:])]
    sample_operands = [xs, mod_s]
    if mixer_out is not None:
        a, attn, a_s, attn_s, wo = mixer_out
        in_specs += [pl.BlockSpec((1, tm, A_CH), prompt_map(0)), pl.BlockSpec((1, tm, GROUP_CH), prompt_map(0)),
                     _const_spec((A_CH + GROUP_CH, D_MODEL))]
        operands += [a, attn, wo]
        sample_specs += [_const_spec((1, rows_s, A_CH)), _const_spec((1, rows_s, GROUP_CH))]
        sample_operands += [a_s, attn_s]
    out_specs = [pl.BlockSpec((1, tm, D_MODEL), prompt_map(0))]
    out_shape = [jax.ShapeDtypeStruct((nb, t, D_MODEL), F32)]
    shift_groups, shift_steps = (), 0
    if shift is not None:
        shift_groups, new_t, shift_steps = shift
        per_step = lambda s: (jnp.minimum(s, n_steps - 1), 0, 0)
        for cache, grp in shift_groups:
            assert cache.shape[0] == n_steps
            in_specs += [pl.BlockSpec((1,) + cache.shape[1:], per_step),
                         pl.BlockSpec((2 * GROUP_CH, LANES), lambda s, grp=grp: (grp, 0),
                                      pipeline_mode=pl.Buffered(1))]
            operands += [cache, new_t]
            out_specs.append(pl.BlockSpec((1,) + cache.shape[1:], per_step))
            out_shape.append(jax.ShapeDtypeStruct(cache.shape, F32))
    out_specs.append(_const_spec((1, rows_s, D_MODEL)))
    out_shape.append(jax.ShapeDtypeStruct((1, rows_s, D_MODEL), F32))
    return pl.pallas_call(
        functools.partial(_ffn_kernel, final=final, chunk=512, mixer_out=mixer_out is not None,
                          n_shift=len(shift_groups), shift_steps=shift_steps, n_prompt_steps=n_steps),
        grid=(n_steps + 1,),
        in_specs=in_specs + sample_specs,
        out_specs=out_specs,
        out_shape=out_shape,
        compiler_params=_params(1),
        name="ffn_final" if final else "l0_out_ffn",
    )(*operands, *sample_operands)


def _spatial_tril(ws_ref, grp):
    i = lax.broadcasted_iota(jnp.int32, (D_CHUNK, D_CHUNK), 0)
    j = lax.broadcasted_iota(jnp.int32, (D_CHUNK, D_CHUNK), 1)
    return jnp.where(j <= i, ws_ref[grp], 0.0).astype(BF16)


def _in1_prompt_kernel(x_ref, mod_ref, g_ref, w_ref, cw_ref, cb_ref, lg_ref, lb_ref, ws_ref, bs_ref, wo_ref,
                       y_ref, cst_ref, ext_ref, cat_ref, *, tm):
    t = pl.program_id(1)
    last = pl.num_programs(1) - 1
    halo = C_CONV - 1
    pad = 8

    @pl.when(t == 0)
    def _():
        ext_ref[0:pad, :] = jnp.zeros((pad, C_CH), F32)

    x = x_ref[0]
    h = _rms_mod(x, g_ref[...], mod_ref[0, 1], mod_ref[0, 0]).astype(BF16)
    v_in, u_in, xi, cg, bg = (_mm(h, w_ref[:, c0:c0 + C_CH]) for c0 in range(O_IN - C_CH, -1, -C_CH))
    ext_ref[pad:pad + tm, :] = cg * xi
    conv = jnp.broadcast_to(cb_ref[...], (tm, C_CH))
    for k in range(C_CONV):
        conv = conv + cw_ref[k:k + 1, :] * ext_ref[pl.ds(pad - halo + k, tm), :]
    cat_ref[:, :C_CH] = (bg * conv).astype(BF16)
    ext_ref[pl.ds(pad - halo, halo), :] = ext_ref[pl.ds(tm + pad - halo, halo), :]
    u = jax.nn.gelu(u_in, approximate=True)
    v = _layernorm(jax.nn.gelu(v_in, approximate=True), lg_ref[...], lb_ref[...]).astype(BF16)
    for grp in range(D_GROUPS):
        ws = _spatial_tril(ws_ref, grp)
        cols = slice(grp * D_GROUP_CH, (grp + 1) * D_GROUP_CH)
        for c in range(tm // D_CHUNK):
            rows = slice(c * D_CHUNK, (c + 1) * D_CHUNK)
            mix = _mm(ws, v[rows, cols]) + bs_ref[:, cols]
            cat_ref[rows, C_CH + grp * D_GROUP_CH:C_CH + (grp + 1) * D_GROUP_CH] = (u[rows, cols] * mix).astype(BF16)
    y_ref[0] = x + mod_ref[0, 2] * _mm(cat_ref[...], wo_ref[...])

    @pl.when(t == last)
    def _():
        cst_ref[0] = ext_ref[pl.ds(pad - halo, halo), :]


def _in1_prompt(x, mod, g, w, cw, cb, lg, lb, ws, bs_full, wo):
    nb, t, _ = x.shape
    tm = ROW_TILE
    row = lambda n, i: (n, i, 0)
    return pl.pallas_call(
        functools.partial(_in1_prompt_kernel, tm=tm),
        grid=(nb, t // tm),
        in_specs=[pl.BlockSpec((1, tm, D_MODEL), row),
                  pl.BlockSpec((1, 6, 1, D_MODEL), lambda n, i: (n, 0, 0, 0)),
                  _const_spec((1, D_MODEL)), _const_spec((D_MODEL, O_IN)),
                  _const_spec((C_CONV, C_CH)), _const_spec((1, C_CH)),
                  _const_spec((1, D_CH)), _const_spec((1, D_CH)),
                  _const_spec((D_GROUPS, D_CHUNK, D_CHUNK)), _const_spec((D_CHUNK, D_CH)),
                  _const_spec((C_CH + D_CH, D_MODEL))],
        out_specs=[pl.BlockSpec((1, tm, D_MODEL), row),
                   pl.BlockSpec((1, C_CONV - 1, C_CH), lambda n, i: (n, 0, 0))],
        out_shape=[jax.ShapeDtypeStruct((nb, t, D_MODEL), F32),
                   jax.ShapeDtypeStruct((nb, C_CONV - 1, C_CH), F32)],
        scratch_shapes=[pltpu.VMEM((tm + 8, C_CH), F32), pltpu.VMEM((tm, C_CH + D_CH), BF16)],
        compiler_params=_params(2),
        name="l1_mixer_prompt",
    )(x, mod, g, w, cw, cb, lg, lb, ws, bs_full, wo)


def _in0_sample_kernel(x_ref, mod_ref, g_ref, w_ref, abuf_ref, cw_ref, cb_ref, lg_ref, lb_ref,
                       a_ref, ast_ref, q_ref, newt_ref, ext_ref, perm_ref, *, nb, steps):
    rows = nb * steps
    hist = (A_CONV - 1) * nb
    h = _rms_mod(x_ref[0], g_ref[...], _mod(mod_ref, 1, rows), _mod(mod_ref, 0, rows)).astype(BF16)
    proj = _mm(h, w_ref[...])
    ext_ref[0:hist, :] = abuf_ref[...]
    ext_ref[hist:hist + rows, :] = proj[:, :A_CH] * jax.nn.sigmoid(proj[:, A_CH:2 * A_CH])
    acc = jnp.broadcast_to(cb_ref[...], (rows, A_CH))
    for k in range(A_CONV):
        acc = acc + cw_ref[k:k + 1, :] * ext_ref[k * nb:k * nb + rows, :]
    a_ref[0] = _silu(_layernorm(acc, lg_ref[...], lb_ref[...])).astype(BF16)
    ast_ref[...] = ext_ref[rows:rows + hist, :]
    q_ref[...] = proj[:, Q_OFF:K_OFF]
    tile = 0
    for g in range(N_GROUPS):
        for off in (K_OFF, V_OFF):
            for c in range(GROUP_TILES):
                col = off + g * GROUP_CH + c * LANES
                for s in range(steps):
                    perm_ref[tile, pl.ds(s, nb, stride=steps), :] = proj[s * nb:(s + 1) * nb, col:col + LANES]
                newt_ref[tile * LANES:(tile + 1) * LANES, :] = perm_ref[tile].T
                tile += 1


def _in0_sample(x, mod, g, w, abuf_t, cw, cb, lg, lb, nb, steps):
    rows = nb * steps
    assert rows == LANES
    hist = (A_CONV - 1) * nb
    return pl.pallas_call(
        functools.partial(_in0_sample_kernel, nb=nb, steps=steps),
        grid=(1,),
        in_specs=[_const_spec((1, rows, D_MODEL)), _const_spec((1, 6, nb, D_MODEL)),
                  _const_spec((1, D_MODEL)), _const_spec((D_MODEL, E_IN)), _const_spec((hist, A_CH)),
                  _const_spec((A_CONV, A_CH)), _const_spec((1, A_CH)), _const_spec((1, A_CH)),
                  _const_spec((1, A_CH))],
        out_specs=[_const_spec((1, rows, A_CH)), _const_spec((hist, A_CH)),
                   _const_spec((rows, N_GROUPS * GROUP_CH)), _const_spec((2 * N_GROUPS * GROUP_CH, LANES))],
        out_shape=[jax.ShapeDtypeStruct((1, rows, A_CH), BF16), jax.ShapeDtypeStruct((hist, A_CH), F32),
                   jax.ShapeDtypeStruct((rows, N_GROUPS * GROUP_CH), F32),
                   jax.ShapeDtypeStruct((2 * N_GROUPS * GROUP_CH, LANES), F32)],
        scratch_shapes=[pltpu.VMEM((hist + rows, A_CH), F32),
                        pltpu.VMEM((2 * N_GROUPS * GROUP_TILES, LANES, LANES), F32)],
        compiler_params=_params(1),
        name="l0_in_sample",
    )(x, mod, g, w, abuf_t, cw, cb, lg, lb)


def _new_steps(newt_ref, r0, r1, n, steps):
    return pltpu.roll(newt_ref[r0:r1, :], lax.rem(LANES - n * steps, LANES), axis=1)


def _shift_window(cache_rows, new_rows, steps):
    w = cache_rows.shape[-1]
    comb = jnp.concatenate([cache_rows, new_rows], axis=1)
    return comb, pltpu.roll(comb, w + LANES - steps, axis=1)[:, :w]


def _attn_sample_kernel(q_ref, newt_ref, c0_ref, c1_ref, c2_hbm, base0_ref, base1_ref, base2_ref,
                        attn_ref, b0_ref, b1_ref, b2_ref, ring_ref, sem_ref, *, steps):
    step = pl.program_id(0)
    n_steps = pl.num_programs(0)
    rows = HPG * steps

    def ring_copy(s):
        slot = s % RING_SLOTS if isinstance(s, int) else lax.rem(s, RING_SLOTS)
        return pltpu.make_async_copy(c2_hbm.at[pl.ds(s * SAMPLE_ROWS_PER_STEP, SAMPLE_ROWS_PER_STEP)],
                                     ring_ref.at[slot], sem_ref.at[slot])

    @pl.when(step == 0)
    def _():
        for s in range(RING_SLOTS - 1):
            ring_copy(s).start()

    @pl.when(step + (RING_SLOTS - 1) < n_steps)
    def _():
        ring_copy(step + (RING_SLOTS - 1)).start()

    ring_copy(step).wait()
    c2_ref = ring_ref.at[lax.rem(step, RING_SLOTS)]

    @pl.when(step == 0)
    def _():
        for base_ref, b_ref in ((base0_ref, b0_ref), (base1_ref, b1_ref), (base2_ref, b2_ref)):
            for h in range(HPG):
                row = jnp.broadcast_to(base_ref[h:h + 1, :], (SUBLANES, base_ref.shape[-1]))
                b_ref[h * steps:(h + 1) * steps, :] = pltpu.roll(row, 0, 1, stride=1, stride_axis=0)[:steps]

    row_head = lax.broadcasted_iota(jnp.int32, (rows, GROUP_CH), 0) // steps
    col_head = lax.broadcasted_iota(jnp.int32, (rows, GROUP_CH), 1) // HEAD_DIM
    step_masks = _head_masks(steps)
    for j in range(SAMPLE_ROWS_PER_STEP):
        n = step * SAMPLE_ROWS_PER_STEP + j
        outs, lses = [], []
        for g, (c_ref, b_ref) in enumerate(zip((c0_ref, c1_ref, c2_ref), (b0_ref, b1_ref, b2_ref))):
            new = _new_steps(newt_ref, 2 * g * GROUP_CH, 2 * (g + 1) * GROUP_CH, n, steps)
            comb_bf = jnp.concatenate([c_ref[j], new], axis=1).astype(BF16)
            q = q_ref[j][:, g * GROUP_CH:(g + 1) * GROUP_CH]
            q_bd = jnp.where(row_head == col_head, jnp.concatenate([q] * HPG, axis=0), 0.0).astype(BF16)
            s = _mm(q_bd, comb_bf[:GROUP_CH]) * (HEAD_DIM ** -0.5) + b_ref[...]
            m = jnp.max(s, axis=-1, keepdims=True)
            e = jnp.exp(s - m)
            l = jnp.sum(e, axis=-1, keepdims=True)
            p = (e * (1.0 / l)).astype(BF16)
            o_all = _mm_nt(p, comb_bf[GROUP_CH:])
            lse_all = jnp.broadcast_to(m + jnp.log(l), (rows, GROUP_CH))
            o = jnp.zeros((steps, GROUP_CH), F32)
            lse = jnp.zeros((steps, GROUP_CH), F32)
            for h in range(HPG):
                o = jnp.where(step_masks[h], o_all[h * steps:(h + 1) * steps], o)
                lse = jnp.where(step_masks[h], lse_all[h * steps:(h + 1) * steps], lse)
            outs.append(o)
            lses.append(lse)
        attn_ref[j] = _mix_groups(outs, lses)


def _attn_sample(q, new_t, caches, biases, steps):
    nb = q.shape[0]
    rps = SAMPLE_ROWS_PER_STEP
    assert nb % rps == 0 and nb // rps >= RING_SLOTS - 1
    per_n = lambda n: (n, 0, 0)
    cache_specs = [pl.BlockSpec((rps, 2 * GROUP_CH, w), per_n) for w in WINDOWS[:2]]
    cache_specs.append(pl.BlockSpec(memory_space=pl.ANY))
    return pl.pallas_call(
        functools.partial(_attn_sample_kernel, steps=steps),
        grid=(nb // rps,),
        in_specs=[pl.BlockSpec((rps, steps, N_GROUPS * GROUP_CH), per_n),
                  _const_spec((2 * N_GROUPS * GROUP_CH, LANES))] + cache_specs
                 + [_const_spec((HPG, w + LANES)) for w in WINDOWS],
        out_specs=pl.BlockSpec((rps, steps, GROUP_CH), per_n),
        out_shape=jax.ShapeDtypeStruct((nb, steps, GROUP_CH), F32),
        scratch_shapes=[pltpu.VMEM((HPG * steps, w + LANES), F32) for w in WINDOWS]
                       + [pltpu.VMEM((RING_SLOTS, rps, 2 * GROUP_CH, WINDOWS[2]), F32),
                          pltpu.SemaphoreType.DMA((RING_SLOTS,))],
        compiler_params=_params(1),
        name="l0_attn_sample",
    )(q, new_t, *caches, *biases)


def _in1_sample_kernel(x_ref, mod_ref, g_ref, w_ref, cbuf_ref, cw_ref, cb_ref, lg_ref, lb_ref, coef_ref, bs_ref,
                       wo_ref, y_ref, cst_ref, dv_ref, ext_ref, *, nb, steps):
    rows = nb * steps
    hist = (C_CONV - 1) * nb
    x = x_ref[0]
    h = _rms_mod(x, g_ref[...], _mod(mod_ref, 1, rows), _mod(mod_ref, 0, rows)).astype(BF16)
    proj = _mm(h, w_ref[...])
    ext_ref[0:hist, :] = cbuf_ref[...]
    ext_ref[hist:hist + rows, :] = proj[:, C_CH:2 * C_CH] * proj[:, 2 * C_CH:3 * C_CH]
    conv = jnp.broadcast_to(cb_ref[...], (rows, C_CH))
    for k in range(C_CONV):
        conv = conv + cw_ref[k:k + 1, :] * ext_ref[k * nb:k * nb + rows, :]
    yc = proj[:, :C_CH] * conv
    cst_ref[...] = ext_ref[rows:rows + hist, :]
    uv = jax.nn.gelu(proj[:, 3 * C_CH:], approximate=True)
    v = _layernorm(uv[:, D_CH:], lg_ref[...], lb_ref[...])
    dv_ref[...] = v
    v_r = v.astype(BF16).astype(F32)
    coef = coef_ref[...].astype(BF16).astype(F32)
    mixes = []
    for i in range(steps):
        mix = jnp.broadcast_to(bs_ref[i:i + 1, :], (nb, D_CH))
        for j in range(i + 1):
            mix = mix + coef[i * steps + j:i * steps + j + 1, :] * v_r[j * nb:(j + 1) * nb, :]
        mixes.append(mix)
    yd = uv[:, :D_CH] * jnp.concatenate(mixes, axis=0)
    cat = jnp.concatenate([yc.astype(BF16), yd.astype(BF16)], axis=1)
    y_ref[0] = x + _mod(mod_ref, 2, rows) * _mm(cat, wo_ref[...])


def _in1_sample(x, mod, g, w, cbuf_t, cw, cb, lg, lb, coef, bs_rows, wo, nb, steps):
    rows = nb * steps
    hist = (C_CONV - 1) * nb
    return pl.pallas_call(
        functools.partial(_in1_sample_kernel, nb=nb, steps=steps),
        grid=(1,),
        in_specs=[_const_spec((1, rows, D_MODEL)), _const_spec((1, 6, nb, D_MODEL)),
                  _const_spec((1, D_MODEL)), _const_spec((D_MODEL, O_IN)), _const_spec((hist, C_CH)),
                  _const_spec((C_CONV, C_CH)), _const_spec((1, C_CH)), _const_spec((1, D_CH)),
                  _const_spec((1, D_CH)), _const_spec((steps * steps, D_CH)), _const_spec((steps, D_CH)),
                  _const_spec((C_CH + D_CH, D_MODEL))],
        out_specs=[_const_spec((1, rows, D_MODEL)), _const_spec((hist, C_CH)), _const_spec((rows, D_CH))],
        out_shape=[jax.ShapeDtypeStruct((1, rows, D_MODEL), F32), jax.ShapeDtypeStruct((hist, C_CH), F32),
                   jax.ShapeDtypeStruct((rows, D_CH), F32)],
        scratch_shapes=[pltpu.VMEM((hist + rows, C_CH), F32)],
        compiler_params=_params(1),
        name="l1_mixer_sample",
    )(x, mod, g, w, cbuf_t, cw, cb, lg, lb, coef, bs_rows, wo)


def _t5_bucket(dist):
    n = np.maximum(np.asarray(dist, dtype=np.int64), 0)
    max_exact = N_BUCKETS // 2
    large = max_exact + (np.log(np.maximum(n, 1) / max_exact) / np.log(MAX_DISTANCE / max_exact)
                         * (N_BUCKETS - max_exact)).astype(np.int32)
    return np.where(n < max_exact, n, np.minimum(large, N_BUCKETS - 1)).astype(np.int32)


def _prompt_bias_base(table, grp):
    rel = Q_BLOCK - np.arange(2 * Q_BLOCK)
    band = (rel >= 0) & (rel <= SUB_WINDOW)
    tab = table[:, grp * HPG:(grp + 1) * HPG]
    bias = tab[_t5_bucket(np.clip(rel, 0, None) * DILATIONS[grp])].T
    return jnp.where(band[None], bias, NEG).astype(F32)


def _sample_bias_base(table, grp):
    w, d = WINDOWS[grp], DILATIONS[grp]
    delta = w - np.arange(w + LANES)
    valid = (delta >= 0) & (delta % d == 0) & (delta // d <= SUB_WINDOW)
    tab = table[:, grp * HPG:(grp + 1) * HPG]
    return jnp.where(valid[None], tab[_t5_bucket(np.clip(delta, 0, None))].T, NEG).astype(F32)


def _step_major(x):
    return jnp.swapaxes(x, 0, 1).reshape((x.shape[0] * x.shape[1],) + x.shape[2:])


def _batch_major(x, nb, steps):
    return jnp.swapaxes(x.reshape((steps, nb) + x.shape[1:]), 0, 1)


def kernel(x_prompt, x_sample, c_prompt, c_sample, state_a_conv, cache_b_kv0, cache_b_kv1, cache_b_kv2, state_c_conv, rel_bias_table, ada_w, ada_b, norm_mix_g, norm_ffn_g, ffn_w_gate_up, ffn_w_down, final_norm_g, e_w_in, a_conv_w, a_conv_b, a_ln_g, a_ln_b, e_w_out, o_w_in, c_conv_w, c_conv_b, d_ln_g, d_ln_b, d_spatial_w, d_spatial_b, o_w_out):
    nb_p = x_prompt.shape[0]
    nb_s, steps, _ = x_sample.shape

    w_in0 = e_w_in[0].astype(BF16)
    w_out0 = e_w_out[0].astype(BF16)
    w_in1 = o_w_in[0].astype(BF16)
    w_out1 = o_w_out[0].astype(BF16)
    w_gu = ffn_w_gate_up.astype(BF16)
    w_dn = ffn_w_down.astype(BF16)
    fin_g = final_norm_g.reshape(1, D_MODEL)

    mod = _ada(jnp.concatenate([c_prompt, c_sample], axis=0), ada_w, ada_b)
    mod = mod.reshape(mod.shape[0], nb_p + nb_s, 6, D_MODEL)
    mod_p = [mod[l, :nb_p].reshape(nb_p, 6, 1, D_MODEL) for l in range(2)]
    mod_s = [jnp.swapaxes(mod[l, nb_p:], 0, 1)[None] for l in range(2)]

    xs = _step_major(x_sample)[None]
    abuf_t = _step_major(state_a_conv[0])
    a_s, a_st_s, q_s, new_t = _in0_sample(xs, mod_s[0], norm_mix_g[0:1], w_in0, abuf_t, a_conv_w[0], a_conv_b,
                                          a_ln_g, a_ln_b, nb_s, steps)
    caches = [jnp.transpose(c[0], (0, 2, 3, 4, 1)).reshape(nb_s, 2 * GROUP_CH, c.shape[2])
              for c in (cache_b_kv0, cache_b_kv1, cache_b_kv2)]
    biases = [_sample_bias_base(rel_bias_table, grp) for grp in range(N_GROUPS)]
    attn_s = _attn_sample(_batch_major(q_s, nb_s, steps), new_t, caches, biases, steps)
    attn_s = _step_major(attn_s).astype(BF16)[None]
    (a, a_st_p, q0, q1, q2, kv0, kv1, kv2, st0, st1, st2) = _in0_prompt(
        x_prompt, mod_p[0], norm_mix_g[0:1], w_in0, a_conv_w[0], a_conv_b, a_ln_g, a_ln_b)
    base = jnp.stack([_prompt_bias_base(rel_bias_table, grp) for grp in range(N_GROUPS)])
    attn = _attn_prompt((q0, q1, q2), (kv0, kv1, kv2), base)
    xp, s2, xs = _ffn(x_prompt, mod_p[0], xs, mod_s[0], norm_ffn_g[0:1], w_gu, w_dn, fin_g, 0, False,
                      (a, attn, a_s, attn_s, w_out0), (((caches[2], 2),), new_t, steps))

    bs_full = jnp.repeat(d_spatial_b[0].T, D_GROUP_CH, axis=1)
    xp, c_st_p = _in1_prompt(xp, mod_p[1], norm_mix_g[1:2], w_in1, c_conv_w[0], c_conv_b, d_ln_g, d_ln_b,
                             d_spatial_w[0], bs_full, w_out1)
    cbuf_t = _step_major(state_c_conv[0])
    coef = jnp.repeat(jnp.transpose(d_spatial_w[0][:, :steps, :steps], (1, 2, 0)).reshape(steps * steps, D_GROUPS),
                      D_GROUP_CH, axis=1)
    xs, c_st_s, dv_s = _in1_sample(xs, mod_s[1], norm_mix_g[1:2], w_in1, cbuf_t, c_conv_w[0], c_conv_b, d_ln_g,
                                   d_ln_b, coef, bs_full[:steps], w_out1, nb_s, steps)
    y_prompt, s0, s1, y_sample = _ffn(xp, mod_p[1], xs, mod_s[1], norm_ffn_g[1:2], w_gu, w_dn, fin_g, 1, True,
                                      None, (((caches[0], 0), (caches[1], 1)), new_t, steps))

    def kv_state_prompt(st):
        return st.reshape(1, nb_p, st.shape[1], 2, HPG, HEAD_DIM)

    def kv_state_sample(st):
        return jnp.transpose(st.reshape(nb_s, 2, HPG, HEAD_DIM, st.shape[-1]), (0, 4, 1, 2, 3))[None]

    return (y_prompt, _batch_major(y_sample[0], nb_s, steps),
            a_st_p[None], _batch_major(a_st_s, nb_s, A_CONV - 1)[None],
            kv_state_prompt(st0), kv_state_sample(s0), kv_state_prompt(st1), kv_state_sample(s1),
            kv_state_prompt(st2), kv_state_sample(s2),
            c_st_p[None], _batch_major(c_st_s, nb_s, C_CONV - 1)[None],
            _batch_major(dv_s, nb_s, steps)[None])
```
